```python
import math
import jax, jax.numpy as jnp
from jax import lax
import numpy as np

D_MODEL = 1024
BATCH = 8
SEQ = 8192
DEPTH = 4
DEC_BATCH = 16
DEC_SEQ = 4096
PAST_LEN = 128

N_HEADS = 8
QK_NOPE_DIM = 128
QK_ROPE_DIM = 64
V_DIM = 128
Q_LORA = 384
KV_LORA = 256
MLA_WIDTH = N_HEADS * V_DIM
MLA_IN = Q_LORA + KV_LORA + QK_ROPE_DIM + MLA_WIDTH
ROPE_THETA = 10000.0
Q_BLOCK = 128
ATTN_SCALE = 1.0 / math.sqrt(QK_NOPE_DIM + QK_ROPE_DIM)
S5_WIDTH = D_MODEL
S5_GROUP = 16
S5_GROUPS = S5_WIDTH // S5_GROUP
S5_STATE = 64
DT_MIN = 0.001
DT_MAX = 0.1
N_MLA = (DEPTH + 1) // 2
N_S5 = DEPTH // 2
ALPHA = (2 * DEPTH) ** 0.25
BETA = (8 * DEPTH) ** -0.25
LN_EPS = 1e-5
RMS_EPS = 1e-6

kernel_name = 'hybrid_mla_s5_encoder'


def _layer_norm(x, g, b):
    xf = x.astype(jnp.float32)
    mu = xf.mean(-1, keepdims=True)
    var = jnp.square(xf - mu).mean(-1, keepdims=True)
    y = (xf - mu) * lax.rsqrt(var + LN_EPS) * g.astype(jnp.float32) + b.astype(jnp.float32)
    return y.astype(x.dtype)


def _rms_norm(x, g):
    xf = x.astype(jnp.float32)
    y = xf * lax.rsqrt(jnp.square(xf).mean(-1, keepdims=True) + RMS_EPS) * g.astype(jnp.float32)
    return y.astype(x.dtype)


def _rope_tables(length, dtype):
    inv = ROPE_THETA ** (-jnp.arange(0, QK_ROPE_DIM, 2, dtype=jnp.float32) / QK_ROPE_DIM)
    ang = jnp.arange(length, dtype=jnp.float32)[:, None] * inv[None, :]
    return jnp.cos(ang).astype(dtype), jnp.sin(ang).astype(dtype)


def _rope(t, cos, sin):
    half = QK_ROPE_DIM // 2
    t1, t2 = t[..., :half], t[..., half:]
    return jnp.concatenate([t1 * cos - t2 * sin, t2 * cos + t1 * sin], axis=-1)


def _attend(q_nope, q_rope, k_nope, k_rope, v):
    B, L, H, _ = q_nope.shape
    nb = L // Q_BLOCK

    def block(args):
        qn, qr = args
        s = (jnp.einsum('bqhd,bkhd->bhqk', qn, k_nope, preferred_element_type=jnp.float32)
             + jnp.einsum('bqhr,bkr->bhqk', qr, k_rope, preferred_element_type=jnp.float32)) * ATTN_SCALE
        p = jax.nn.softmax(s, axis=-1).astype(v.dtype)
        return jnp.einsum('bhqk,bkhd->bqhd', p, v)

    qn = q_nope.reshape(B, nb, Q_BLOCK, H, QK_NOPE_DIM).swapaxes(0, 1)
    qr = q_rope.reshape(B, nb, Q_BLOCK, H, QK_ROPE_DIM).swapaxes(0, 1)
    o = lax.map(block, (qn, qr))
    return o.swapaxes(0, 1).reshape(B, L, H * V_DIM)


def _mla(x, w_in, g_q, w_q_up, g_kv, w_kv_up, w_out):
    B, L, _ = x.shape
    h = x @ w_in
    c_q, c_kv, k_rope, gate = jnp.split(h, [Q_LORA, Q_LORA + KV_LORA, Q_LORA + KV_LORA + QK_ROPE_DIM], axis=-1)
    q = (_rms_norm(c_q, g_q) @ w_q_up).reshape(B, L, N_HEADS, QK_NOPE_DIM + QK_ROPE_DIM)
    q_nope, q_rope = q[..., :QK_NOPE_DIM], q[..., QK_NOPE_DIM:]
    kv = (_rms_norm(c_kv, g_kv) @ w_kv_up).reshape(B, L, N_HEADS, QK_NOPE_DIM + V_DIM)
    k_nope, v = kv[..., :QK_NOPE_DIM], kv[..., QK_NOPE_DIM:]
    cos, sin = _rope_tables(L, x.dtype)
    q_rope = _rope(q_rope, cos[:, None, :], sin[:, None, :])
    k_rope = _rope(k_rope, cos, sin)
    o = _attend(q_nope, q_rope, k_nope, k_rope, v)
    return (o * jax.nn.silu(gate)) @ w_out


def _ssm_combine(left, right):
    a1, b1 = left
    a2, b2 = right
    return a1 * a2, a2 * b1 + b2


def _s5_scan(us, lam_bar, b_bar, c):
    bu = lax.complex(jnp.einsum('lgp,gnp->lgn', us, b_bar.real),
                     jnp.einsum('lgp,gnp->lgn', us, b_bar.imag))
    a = jnp.broadcast_to(lam_bar, bu.shape)
    _, h = lax.associative_scan(_ssm_combine, (a, bu), axis=0)
    return jnp.einsum('gpn,lgn->lgp', c, h).real


def _s5(x, w_in, a_re, a_im, log_step, b_re, b_im, c_re, c_im, d, w_glu, b_glu, w_out):
    B, L, _ = x.shape
    f32 = jnp.float32
    u, gate = jnp.split(x @ w_in, 2, axis=-1)
    lam = lax.complex(a_re.astype(f32), a_im.astype(f32))
    step = jnp.exp(log_step.astype(f32))[..., None]
    lam_bar = jnp.exp(lam * step)
    b = lax.complex(b_re.astype(f32), b_im.astype(f32))
    b_bar = ((lam_bar - 1.0) / lam)[..., None] * b
    c = lax.complex(c_re.astype(f32), c_im.astype(f32))
    uf = u.astype(f32)
    ug = uf.reshape(B, L, S5_GROUPS, S5_GROUP)

    def per_seq(us):
        y_f = _s5_scan(us, lam_bar[0], b_bar[0], c[0])
        y_b = _s5_scan(us[::-1], lam_bar[1], b_bar[1], c[1])[::-1]
        return y_f + y_b

    y = lax.map(per_seq, ug).reshape(B, L, S5_WIDTH) + d.astype(f32) * uf
    y = jax.nn.gelu(y).astype(x.dtype)
    y = y * jax.nn.sigmoid(y @ w_glu + b_glu)
    return (y * jax.nn.silu(gate)) @ w_out


def _trunk(x, mla, s5, ln_g, ln_b):
    for i in range(DEPTH):
        j = i // 2
        if i % 2 == 0:
            y = _mla(x, *[w[j] for w in mla])
        else:
            y = _s5(x, *[w[j] for w in s5])
        x = _layer_norm(ALPHA * x + y, ln_g[i], ln_b[i])
    return x


def setup_inputs(seed: int = 0) -> dict:
    key = jax.random.key(seed)
    ks = jax.random.split(key, 24)
    f32 = jnp.float32
    nrm = lambda k, shape, s: jax.random.normal(k, shape, f32) * s
    G, N, P = S5_GROUPS, S5_STATE, S5_GROUP
    return {
        'x_prompt': nrm(ks[0], (BATCH, SEQ, D_MODEL), 1.0),
        'x_sample': nrm(ks[1], (DEC_BATCH, DEC_SEQ, D_MODEL), 1.0),
        'mla_w_in': nrm(ks[2], (N_MLA, D_MODEL, MLA_IN), D_MODEL ** -0.5),
        'mla_g_q': 1.0 + nrm(ks[3], (N_MLA, Q_LORA), 0.01),
        'mla_w_q_up': nrm(ks[4], (N_MLA, Q_LORA, N_HEADS * (QK_NOPE_DIM + QK_ROPE_DIM)), Q_LORA ** -0.5),
        'mla_g_kv': 1.0 + nrm(ks[5], (N_MLA, KV_LORA), 0.01),
        'mla_w_kv_up': nrm(ks[6], (N_MLA, KV_LORA, N_HEADS * (QK_NOPE_DIM + V_DIM)), KV_LORA ** -0.5),
        'mla_w_out': nrm(ks[7], (N_MLA, MLA_WIDTH, D_MODEL), BETA * MLA_WIDTH ** -0.5),
        's5_w_in': nrm(ks[8], (N_S5, D_MODEL, 2 * S5_WIDTH), D_MODEL ** -0.5),
        's5_a_re': -0.5 + nrm(ks[9], (N_S5, 2, G, N), 0.01),
        's5_a_im': jnp.pi * jnp.arange(N, dtype=f32) + nrm(ks[10], (N_S5, 2, G, N), 0.01),
        's5_log_step': jax.random.uniform(ks[11], (N_S5, 2, G), f32, math.log(DT_MIN), math.log(DT_MAX)),
        's5_b_re': nrm(ks[12], (N_S5, 2, G, N, P), (2 * P) ** -0.5),
        's5_b_im': nrm(ks[13], (N_S5, 2, G, N, P), (2 * P) ** -0.5),
        's5_c_re': nrm(ks[14], (N_S5, 2, G, P, N), (2 * N) ** -0.5),
        's5_c_im': nrm(ks[15], (N_S5, 2, G, P, N), (2 * N) ** -0.5),
        's5_d': nrm(ks[16], (N_S5, S5_WIDTH), 1.0),
        's5_w_glu': nrm(ks[17], (N_S5, S5_WIDTH, S5_WIDTH), S5_WIDTH ** -0.5),
        's5_b_glu': nrm(ks[18], (N_S5, S5_WIDTH), 0.01),
        's5_w_out': nrm(ks[19], (N_S5, S5_WIDTH, D_MODEL), BETA * S5_WIDTH ** -0.5),
        'ln_g': 1.0 + nrm(ks[20], (DEPTH, D_MODEL), 0.01),
        'ln_b': nrm(ks[21], (DEPTH, D_MODEL), 0.01),
    }


def reference(x_prompt, x_sample, mla_w_in, mla_g_q, mla_w_q_up, mla_g_kv, mla_w_kv_up, mla_w_out,
              s5_w_in, s5_a_re, s5_a_im, s5_log_step, s5_b_re, s5_b_im, s5_c_re, s5_c_im, s5_d,
              s5_w_glu, s5_b_glu, s5_w_out, ln_g, ln_b):
    mla = (mla_w_in, mla_g_q, mla_w_q_up, mla_g_kv, mla_w_kv_up, mla_w_out)
    s5 = (s5_w_in, s5_a_re, s5_a_im, s5_log_step, s5_b_re, s5_b_im, s5_c_re, s5_c_im, s5_d,
          s5_w_glu, s5_b_glu, s5_w_out)
    y_prompt = _trunk(x_prompt, mla, s5, ln_g, ln_b)
    y_sample = _trunk(x_sample, mla, s5, ln_g, ln_b)
    return (y_prompt, y_sample)
```

```python
import functools
import math

import jax
import jax.numpy as jnp
from jax import lax
from jax.experimental import pallas as pl
from jax.experimental.pallas import tpu as pltpu

F32 = jnp.float32
BF16 = jnp.bfloat16

D_MODEL = 1024
DEPTH = 4
N_HEADS = 8
QK_NOPE_DIM = 128
QK_ROPE_DIM = 64
V_DIM = 128
Q_LORA = 384
KV_LORA = 256
ROPE_THETA = 10000.0
ATTN_SCALE = 1.0 / math.sqrt(QK_NOPE_DIM + QK_ROPE_DIM)
S5_GROUP = 16
S5_GROUPS = D_MODEL // S5_GROUP
S5_STATE = 64
ALPHA = (2 * DEPTH) ** 0.25
LN_EPS = 1e-5
RMS_EPS = 1e-6

LANES = 128
HEAD_PAD = 256
ROPE_HALF = QK_ROPE_DIM // 2
CHUNK = 128
TOK_TILE = 512
VMEM_LIMIT = 56 * 1024 * 1024

NT_DIMS = (((1,), (1,)), ((), ()))


def _dot(a, b):
    return jnp.dot(a, b, preferred_element_type=F32)


def _dot_nt(a, b):
    return lax.dot_general(a, b, NT_DIMS, preferred_element_type=F32)


def _params(*sem):
    return pltpu.CompilerParams(dimension_semantics=sem, vmem_limit_bytes=VMEM_LIMIT)


def _layer_norm(r, g, b):
    mu = jnp.mean(r, axis=-1, keepdims=True)
    d = r - mu
    var = jnp.mean(d * d, axis=-1, keepdims=True)
    return d * lax.rsqrt(var + LN_EPS) * g + b


def _rms_norm(x, g):
    return x * lax.rsqrt(jnp.mean(x * x, axis=-1, keepdims=True) + RMS_EPS) * g


def _silu(x):
    return x * jax.nn.sigmoid(x)


def _mla_proj_kernel(x_ref, wa_ref, gq_ref, gkv_ref, wqT_ref, wkn_ref, wvT_ref, wgT_ref,
                     cosT_ref, sinT_ref, cosk_ref, sina_ref, sinb_ref,
                     qT_ref, k_ref, vT_ref, sgT_ref):
    xb = x_ref[0].astype(BF16)
    ha = _dot(xb, wa_ref[...])
    cq = _rms_norm(ha[:, :Q_LORA], gq_ref[...]).astype(BF16)
    ckv = _rms_norm(ha[:, Q_LORA:Q_LORA + KV_LORA], gkv_ref[...]).astype(BF16)
    kr = ha[:, Q_LORA + KV_LORA:]

    qT = _dot_nt(wqT_ref[...], cq) * ATTN_SCALE
    cosT = cosT_ref[...]
    sinT = sinT_ref[...]
    tm = qT.shape[1]
    zpad = jnp.zeros((HEAD_PAD - QK_NOPE_DIM - QK_ROPE_DIM, tm), F32)
    for h in range(N_HEADS):
        blk = qT[h * HEAD_PAD:(h + 1) * HEAD_PAD]
        t1 = blk[QK_NOPE_DIM:QK_NOPE_DIM + ROPE_HALF]
        t2 = blk[QK_NOPE_DIM + ROPE_HALF:QK_NOPE_DIM + QK_ROPE_DIM]
        out = jnp.concatenate(
            [blk[:QK_NOPE_DIM], t1 * cosT - t2 * sinT, t2 * cosT + t1 * sinT, zpad], axis=0)
        qT_ref[0, 0, h * HEAD_PAD:(h + 1) * HEAD_PAD, :] = out.astype(BF16)

    kn = _dot(ckv, wkn_ref[...])
    krr = (kr * cosk_ref[...]
           + pltpu.roll(kr, ROPE_HALF, 1) * sina_ref[...]
           + pltpu.roll(kr, LANES - ROPE_HALF, 1) * sinb_ref[...]).astype(BF16)
    for h in range(N_HEADS):
        k_ref[0, h, :, :QK_NOPE_DIM] = kn[:, h * QK_NOPE_DIM:(h + 1) * QK_NOPE_DIM].astype(BF16)
        k_ref[0, h, :, QK_NOPE_DIM:] = krr

    vT_ref[0, 0] = _dot_nt(wvT_ref[...], ckv).astype(BF16)
    sgT_ref[0, 0] = _silu(_dot_nt(wgT_ref[...], xb)).astype(BF16)


def _mla_proj(x, w, tabs, tm):
    B, L, _ = x.shape
    nl = L // tm
    const = lambda shape: pl.BlockSpec(shape, lambda b, i: (0,) * len(shape))
    return pl.pallas_call(
        _mla_proj_kernel,
        grid=(B, nl),
        in_specs=[
            pl.BlockSpec((1, tm, D_MODEL), lambda b, i: (b, i, 0)),
            const(w['wa'].shape), const(w['gq'].shape), const(w['gkv'].shape),
            const(w['wqT'].shape), const(w['wkn'].shape), const(w['wvT'].shape), const(w['wgT'].shape),
            pl.BlockSpec((ROPE_HALF, tm), lambda b, i: (0, i)),
            pl.BlockSpec((ROPE_HALF, tm), lambda b, i: (0, i)),
            pl.BlockSpec((tm, LANES), lambda b, i: (i, 0)),
            pl.BlockSpec((tm, LANES), lambda b, i: (i, 0)),
            pl.BlockSpec((tm, LANES), lambda b, i: (i, 0)),
        ],
        out_specs=[
            pl.BlockSpec((1, 1, N_HEADS * HEAD_PAD, tm), lambda b, i: (b, i, 0, 0)),
            pl.BlockSpec((1, N_HEADS, tm, HEAD_PAD), lambda b, i: (b, 0, i, 0)),
            pl.BlockSpec((1, 1, N_HEADS * V_DIM, tm), lambda b, i: (b, i, 0, 0)),
            pl.BlockSpec((1, 1, N_HEADS * V_DIM, tm), lambda b, i: (b, i, 0, 0)),
        ],
        out_shape=[
            jax.ShapeDtypeStruct((B, nl, N_HEADS * HEAD_PAD, tm), BF16),
            jax.ShapeDtypeStruct((B, N_HEADS, L, HEAD_PAD), BF16),
            jax.ShapeDtypeStruct((B, nl, N_HEADS * V_DIM, tm), BF16),
            jax.ShapeDtypeStruct((B, nl, N_HEADS * V_DIM, tm), BF16),
        ],
        compiler_params=_params("parallel", "parallel"),
        name="mla_proj",
    )(x, w['wa'], w['gq'], w['gkv'], w['wqT'], w['wkn'], w['wvT'], w['wgT'],
      tabs['cosT'], tabs['sinT'], tabs['cosk'], tabs['sina'], tabs['sinb'])


def _attn_kernel(qT_ref, k_ref, vT_ref, sgT_ref, o_ref, m_sc, l_sc, acc_sc, *, tk, nk):
    qT = qT_ref[0, 0]
    m_sc[...] = jnp.full(m_sc.shape, -jnp.inf, F32)
    l_sc[...] = jnp.zeros(l_sc.shape, F32)
    acc_sc[...] = jnp.zeros(acc_sc.shape, F32)

    def body(j, carry):
        ks = k_ref[0, 0, pl.ds(pl.multiple_of(j * tk, tk), tk), :]
        sT = _dot(ks, qT)
        m_prev = m_sc[...]
        m_new = jnp.maximum(m_prev, jnp.max(sT, axis=0, keepdims=True))
        alpha = jnp.exp(m_prev - m_new)
        p = jnp.exp(sT - m_new)
        l_sc[...] = alpha * l_sc[...] + jnp.sum(p, axis=0, keepdims=True)
        acc_sc[...] = acc_sc[...] * alpha + _dot(vT_ref[0, j], p.astype(BF16))
        m_sc[...] = m_new
        return carry

    lax.fori_loop(0, nk, body, 0)
    o = acc_sc[...] / l_sc[...] * sgT_ref[0, 0].astype(F32)
    o_ref[0, 0] = o.astype(BF16)


def _attention(qT, k, vT, sgT, tm):
    B, nl, _, _ = qT.shape
    L = nl * tm
    kern = functools.partial(_attn_kernel, tk=tm, nk=nl)
    return pl.pallas_call(
        kern,
        grid=(B, N_HEADS, nl),
        in_specs=[
            pl.BlockSpec((1, 1, HEAD_PAD, tm), lambda b, h, i: (b, i, h, 0)),
            pl.BlockSpec((1, 1, L, HEAD_PAD), lambda b, h, i: (b, h, 0, 0)),
            pl.BlockSpec((1, nl, V_DIM, tm), lambda b, h, i: (b, 0, h, 0)),
            pl.BlockSpec((1, 1, V_DIM, tm), lambda b, h, i: (b, i, h, 0)),
        ],
        out_specs=pl.BlockSpec((1, 1, V_DIM, tm), lambda b, h, i: (b, i, h, 0)),
        out_shape=jax.ShapeDtypeStruct((B, nl, N_HEADS * V_DIM, tm), BF16),
        scratch_shapes=[pltpu.VMEM((1, tm), F32), pltpu.VMEM((1, tm), F32), pltpu.VMEM((V_DIM, tm), F32)],
        compiler_params=_params("parallel", "parallel", "arbitrary"),
        name="mla_attn",
    )(qT, k, vT, sgT)


def _out_kernel(zT_ref, x_ref, woT_ref, g_ref, b_ref, o_ref):
    yT = _dot(woT_ref[...], zT_ref[0, 0])
    r = ALPHA * x_ref[0] + yT.T
    o_ref[0] = _layer_norm(r, g_ref[...], b_ref[...])


def _mla_out(ogT, x, woT, ln_g, ln_b, tm):
    B, L, _ = x.shape
    nl = L // tm
    return pl.pallas_call(
        _out_kernel,
        grid=(B, nl),
        in_specs=[
            pl.BlockSpec((1, 1, D_MODEL, tm), lambda b, i: (b, i, 0, 0)),
            pl.BlockSpec((1, tm, D_MODEL), lambda b, i: (b, i, 0)),
            pl.BlockSpec((D_MODEL, D_MODEL), lambda b, i: (0, 0)),
            pl.BlockSpec((1, D_MODEL), lambda b, i: (0, 0)),
            pl.BlockSpec((1, D_MODEL), lambda b, i: (0, 0)),
        ],
        out_specs=pl.BlockSpec((1, tm, D_MODEL), lambda b, i: (b, i, 0)),
        out_shape=jax.ShapeDtypeStruct((B, L, D_MODEL), F32),
        compiler_params=_params("parallel", "parallel"),
        name="mla_out",
    )(ogT, x, woT, ln_g, ln_b)


def _s5_in_kernel(x_ref, wuT_ref, wgT_ref, uT_ref, sgT_ref):
    xb = x_ref[0].astype(BF16)
    uT_ref[...] = _dot_nt(wuT_ref[...], xb).astype(BF16)
    sgT_ref[...] = _silu(_dot_nt(wgT_ref[...], xb)).astype(BF16)


def _s5_in(x, wuT, wgT, tm):
    B, L, _ = x.shape
    nl = L // tm
    tok = pl.BlockSpec((D_MODEL, tm), lambda b, i: (0, b * nl + i))
    wspec = pl.BlockSpec((D_MODEL, D_MODEL), lambda b, i: (0, 0))
    return pl.pallas_call(
        _s5_in_kernel,
        grid=(B, nl),
        in_specs=[pl.BlockSpec((1, tm, D_MODEL), lambda b, i: (b, i, 0)), wspec, wspec],
        out_specs=[tok, tok],
        out_shape=[jax.ShapeDtypeStruct((D_MODEL, B * L), BF16)] * 2,
        compiler_params=_params("parallel", "parallel"),
        name="s5_in",
    )(x, wuT, wgT)


def _toep_kernel(cbT_ref, pwT_ref, o_ref, kv_sc):
    kv_sc[...] = jnp.dot(cbT_ref[0], pwT_ref[0], preferred_element_type=F32,
                         precision=lax.Precision.HIGHEST)

    def body(pi, carry):
        row0 = pl.multiple_of(pi * CHUNK, CHUNK)
        for po in range(S5_GROUP):
            kv = kv_sc[pl.ds(pi * S5_GROUP + po, 1), :]
            rolled = pltpu.roll(jnp.broadcast_to(kv, (CHUNK, 2 * CHUNK)), 0, 1, stride=1, stride_axis=0)
            o_ref[0, pl.ds(row0, CHUNK), po * CHUNK:(po + 1) * CHUNK] = rolled[:, CHUNK:].astype(BF16)
        return carry

    lax.fori_loop(0, S5_GROUP, body, 0)


def _toeplitz(cbT, pwT):
    G = cbT.shape[0]
    n = S5_GROUP * CHUNK
    blk = pl.BlockSpec((1, 2 * CHUNK, 2 * CHUNK), lambda g: (g, 0, 0))
    return pl.pallas_call(
        _toep_kernel,
        grid=(G,),
        in_specs=[blk, blk],
        out_specs=pl.BlockSpec((1, n, n), lambda g: (g, 0, 0)),
        out_shape=jax.ShapeDtypeStruct((G, n, n), BF16),
        scratch_shapes=[pltpu.VMEM((2 * CHUNK, 2 * CHUNK), F32)],
        compiler_params=_params("parallel"),
        name="s5_toeplitz",
    )(cbT, pwT)


def _scan_kernel(u_ref, tm_ref, rs_ref, ri_ref, ac_ref, y_ref, sf_sc, sb_sc, hf_sc, hb_sc, *, nseq, nchunk):
    lhs = jnp.concatenate([u_ref[p] for p in range(S5_GROUP)], axis=1)
    y = _dot(lhs, tm_ref[0])
    s = _dot(lhs, rs_ref[0])
    half = 2 * S5_STATE
    sf_sc[...] = s[:, :half]
    sb_sc[...] = s[:, half:]
    ac = ac_ref[0]

    def step(state, are, aim, c, s_sc, h_sc):
        rows = pl.ds(c, nseq, stride=nchunk)
        h_sc[rows, :] = state
        return are * state + aim * pltpu.roll(state, S5_STATE, 1) + s_sc[rows, :]

    e = jnp.zeros((nseq, half), F32)
    f = jnp.zeros((nseq, half), F32)
    for c in range(nchunk):
        e = step(e, ac[0:1], ac[1:2], c, sf_sc, hf_sc)
        f = step(f, ac[2:3], ac[3:4], nchunk - 1 - c, sb_sc, hb_sc)

    h = jnp.concatenate([hf_sc[...], hb_sc[...]], axis=1).astype(BF16)
    y = y + _dot(h, ri_ref[0])
    for p in range(S5_GROUP):
        y_ref[p] = y[:, p * CHUNK:(p + 1) * CHUNK].astype(BF16)


def _s5_scan(ubT, toep, rs, ri, ac, nseq, nchunk):
    W, T = ubT.shape
    R = T // CHUNK
    u3 = ubT.reshape(W, R, CHUNK)
    n = S5_GROUP * CHUNK
    kern = functools.partial(_scan_kernel, nseq=nseq, nchunk=nchunk)
    y3 = pl.pallas_call(
        kern,
        grid=(S5_GROUPS,),
        in_specs=[
            pl.BlockSpec((S5_GROUP, R, CHUNK), lambda g: (g, 0, 0)),
            pl.BlockSpec((1, n, n), lambda g: (g, 0, 0)),
            pl.BlockSpec((1, n, 4 * S5_STATE), lambda g: (g, 0, 0)),
            pl.BlockSpec((1, 4 * S5_STATE, n), lambda g: (g, 0, 0)),
            pl.BlockSpec((1, 8, 2 * S5_STATE), lambda g: (g, 0, 0)),
        ],
        out_specs=pl.BlockSpec((S5_GROUP, R, CHUNK), lambda g: (g, 0, 0)),
        out_shape=jax.ShapeDtypeStruct((W, R, CHUNK), BF16),
        scratch_shapes=[pltpu.VMEM((R, 2 * S5_STATE), F32)] * 4,
        compiler_params=_params("parallel"),
        name="s5_scan",
    )(u3, toep, rs, ri, ac)
    return y3.reshape(W, T)


def _s5_out_kernel(ys_ref, u_ref, sg_ref, x_ref, d_ref, wgluT_ref, bglu_ref, woT_ref, g_ref, b_ref, o_ref):
    tm = ys_ref.shape[1]
    rep = tm // LANES
    d = jnp.tile(d_ref[...], (1, rep))
    bglu = jnp.tile(bglu_ref[...], (1, rep))
    y = ys_ref[...].astype(F32) + d * u_ref[...].astype(F32)
    y = jax.nn.gelu(y)
    z = _dot(wgluT_ref[...], y.astype(BF16)) + bglu
    v = (y * jax.nn.sigmoid(z) * sg_ref[...].astype(F32)).astype(BF16)
    oT = _dot(woT_ref[...], v)
    r = ALPHA * x_ref[0] + oT.T
    o_ref[0] = _layer_norm(r, g_ref[...], b_ref[...])


def _s5_out(ysT, ubT, sgT, x, w, ln_g, ln_b, tm):
    B, L, _ = x.shape
    nl = L // tm
    tok = pl.BlockSpec((D_MODEL, tm), lambda b, i: (0, b * nl + i))
    sq = pl.BlockSpec((D_MODEL, D_MODEL), lambda b, i: (0, 0))
    col = pl.BlockSpec((D_MODEL, LANES), lambda b, i: (0, 0))
    row = pl.BlockSpec((1, D_MODEL), lambda b, i: (0, 0))
    return pl.pallas_call(
        _s5_out_kernel,
        grid=(B, nl),
        in_specs=[tok, tok, tok, pl.BlockSpec((1, tm, D_MODEL), lambda b, i: (b, i, 0)),
                  col, sq, col, sq, row, row],
        out_specs=pl.BlockSpec((1, tm, D_MODEL), lambda b, i: (b, i, 0)),
        out_shape=jax.ShapeDtypeStruct((B, L, D_MODEL), F32),
        compiler_params=_params("parallel", "parallel"),
        name="s5_out",
    )(ysT, ubT, sgT, x, w['d'], w['wgluT'], w['bglu'], w['woT'], ln_g, ln_b)


def _prep_mla(w_in, g_q, w_q_up, g_kv, w_kv_up, w_out):
    nkv = Q_LORA + KV_LORA
    wa = jnp.concatenate([w_in[:, :nkv + QK_ROPE_DIM],
                          jnp.zeros((D_MODEL, LANES - QK_ROPE_DIM), F32)], axis=1)
    wq = w_q_up.reshape(Q_LORA, N_HEADS, QK_NOPE_DIM + QK_ROPE_DIM)
    wq = jnp.pad(wq, ((0, 0), (0, 0), (0, HEAD_PAD - QK_NOPE_DIM - QK_ROPE_DIM)))
    wkv = w_kv_up.reshape(KV_LORA, N_HEADS, QK_NOPE_DIM + V_DIM)
    return dict(
        wa=wa.astype(BF16),
        gq=g_q.reshape(1, Q_LORA), gkv=g_kv.reshape(1, KV_LORA),
        wqT=wq.reshape(Q_LORA, N_HEADS * HEAD_PAD).T.astype(BF16),
        wkn=wkv[:, :, :QK_NOPE_DIM].reshape(KV_LORA, N_HEADS * QK_NOPE_DIM).astype(BF16),
        wvT=wkv[:, :, QK_NOPE_DIM:].reshape(KV_LORA, N_HEADS * V_DIM).T.astype(BF16),
        wgT=w_in[:, nkv + QK_ROPE_DIM:].T.astype(BF16),
        woT=w_out.T.astype(BF16),
    )


def _rope_tables(L):
    inv = ROPE_THETA ** (-jnp.arange(0, QK_ROPE_DIM, 2, dtype=F32) / QK_ROPE_DIM)
    ang = jnp.arange(L, dtype=F32)[:, None] * inv[None, :]
    cos, sin = jnp.cos(ang), jnp.sin(ang)
    z = jnp.zeros_like(cos)
    z2 = jnp.zeros((L, LANES - QK_ROPE_DIM), F32)
    return dict(
        cosT=cos.T, sinT=sin.T,
        cosk=jnp.concatenate([cos, cos, z2], axis=1),
        sina=jnp.concatenate([z, sin, z2], axis=1),
        sinb=jnp.concatenate([-sin, z, z2], axis=1),
    )


def _powers(lam_bar, n):
    pw = jnp.stack([jnp.ones_like(lam_bar), lam_bar], axis=-2)
    top = lam_bar
    while pw.shape[-2] - 1 < n:
        pw = jnp.concatenate([pw, pw[..., 1:, :] * top[..., None, :]], axis=-2)
        top = top * top
    return pw


def _reim(z, axis):
    return jnp.concatenate([jnp.real(z), jnp.imag(z)], axis=axis)


def _prep_s5(w_in, a_re, a_im, log_step, b_re, b_im, c_re, c_im, d, w_glu, b_glu, w_out):
    G, N, P, T = S5_GROUPS, S5_STATE, S5_GROUP, CHUNK
    lam = lax.complex(a_re, a_im)
    step = jnp.exp(log_step)[..., None]
    lam_bar = jnp.exp(lam * step)
    b_bar = ((lam_bar - 1.0) / lam)[..., None] * lax.complex(b_re, b_im)
    c = lax.complex(c_re, c_im)
    pw = _powers(lam_bar, T)

    cb = c[:, :, None, :, :] * jnp.swapaxes(b_bar, -1, -2)[:, :, :, None, :]
    cb = cb.reshape(2, G, P * P, N)
    cbT = jnp.concatenate([_reim(cb[0], -1), _reim(cb[1], -1)], axis=-1)
    pf = jnp.swapaxes(pw[0, :, :T], -1, -2)
    pb = jnp.swapaxes(pw[1, :, :T], -1, -2)[..., ::-1]
    zf = jnp.zeros((G, N, T), pf.dtype)
    pf = jnp.concatenate([zf, pf], axis=-1)
    pb = jnp.concatenate([zf[..., :1], pb, zf[..., :T - 1]], axis=-1)
    pwT = jnp.concatenate([jnp.real(pf), -jnp.imag(pf), jnp.real(pb), -jnp.imag(pb)], axis=1)

    sf = pw[0, :, T - 1::-1][:, :T]
    sb = pw[1, :, :T]
    rs_f = jnp.swapaxes(b_bar[0], -1, -2)[:, :, None, :] * sf[:, None, :, :]
    rs_b = jnp.swapaxes(b_bar[1], -1, -2)[:, :, None, :] * sb[:, None, :, :]
    rs = jnp.concatenate([_reim(rs_f, -1), _reim(rs_b, -1)], axis=-1).reshape(G, P * T, 4 * N)

    of = pw[0, :, 1:T + 1]
    ob = pw[1, :, T:0:-1]
    ri_f = c[0][:, :, None, :] * of[:, None, :, :]
    ri_b = c[1][:, :, None, :] * ob[:, None, :, :]
    ri = jnp.concatenate([jnp.real(ri_f), -jnp.imag(ri_f), jnp.real(ri_b), -jnp.imag(ri_b)], axis=-1)
    ri = jnp.swapaxes(ri.reshape(G, P * T, 4 * N), -1, -2)

    a = pw[:, :, T]
    are, aim = jnp.real(a), jnp.imag(a)
    rows = [jnp.concatenate([are[0], are[0]], -1), jnp.concatenate([-aim[0], aim[0]], -1),
            jnp.concatenate([are[1], are[1]], -1), jnp.concatenate([-aim[1], aim[1]], -1)]
    ac = jnp.stack(rows + [jnp.zeros_like(rows[0])] * 4, axis=1)

    return dict(
        wuT=w_in[:, :D_MODEL].T.astype(BF16), wgT=w_in[:, D_MODEL:].T.astype(BF16),
        cbT=cbT, pwT=pwT, rs=rs.astype(BF16), ri=ri.astype(BF16), ac=ac,
        d=jnp.broadcast_to(d[:, None], (D_MODEL, LANES)),
        wgluT=w_glu.T.astype(BF16),
        bglu=jnp.broadcast_to(b_glu[:, None], (D_MODEL, LANES)),
        woT=w_out.T.astype(BF16),
    )


def _mla_layer(x, w, tabs, ln_g, ln_b, tm):
    qT, k, vT, sgT = _mla_proj(x, w, tabs, tm)
    ogT = _attention(qT, k, vT, sgT, tm)
    return _mla_out(ogT, x, w['woT'], ln_g, ln_b, tm)


def _s5_layer(x, w, toep, ln_g, ln_b, tm):
    B, L, _ = x.shape
    ubT, sgT = _s5_in(x, w['wuT'], w['wgT'], tm)
    ysT = _s5_scan(ubT, toep, w['rs'], w['ri'], w['ac'], B, L // CHUNK)
    return _s5_out(ysT, ubT, sgT, x, w, ln_g, ln_b, tm)


def _trunk(x, mla_w, s5_w, toeps, ln_g, ln_b):
    L = x.shape[1]
    tm = min(TOK_TILE, L)
    tabs = _rope_tables(L)
    for i in range(DEPTH):
        g, b = ln_g[i].reshape(1, D_MODEL), ln_b[i].reshape(1, D_MODEL)
        if i % 2 == 0:
            x = _mla_layer(x, mla_w[i // 2], tabs, g, b, tm)
        else:
            x = _s5_layer(x, s5_w[i // 2], toeps[i // 2], g, b, tm)
    return x


def kernel(x_prompt, x_sample, mla_w_in, mla_g_q, mla_w_q_up, mla_g_kv, mla_w_kv_up, mla_w_out,
           s5_w_in, s5_a_re, s5_a_im, s5_log_step, s5_b_re, s5_b_im, s5_c_re, s5_c_im, s5_d,
           s5_w_glu, s5_b_glu, s5_w_out, ln_g, ln_b):
    mla = (mla_w_in, mla_g_q, mla_w_q_up, mla_g_kv, mla_w_kv_up, mla_w_out)
    s5 = (s5_w_in, s5_a_re, s5_a_im, s5_log_step, s5_b_re, s5_b_im, s5_c_re, s5_c_im, s5_d,
          s5_w_glu, s5_b_glu, s5_w_out)
    mla_w = [_prep_mla(*[w[j] for w in mla]) for j in range(mla_w_in.shape[0])]
    s5_w = [_prep_s5(*[w[j] for w in s5]) for j in range(s5_w_in.shape[0])]
    toeps = [_toeplitz(w['cbT'], w['pwT']) for w in s5_w]
    y_prompt = _trunk(x_prompt, mla_w, s5_w, toeps, ln_g, ln_b)
    y_sample = _trunk(x_sample, mla_w, s5_w, toeps, ln_g, ln_b)
    return (y_prompt, y_sample)
```

```python
import functools
import math

import jax
import jax.numpy as jnp
from jax import lax
from jax.experimental import pallas as pl
from jax.experimental.pallas import tpu as pltpu

F32 = jnp.float32
BF16 = jnp.bfloat16

D_MODEL = 1024
DEPTH = 4
N_HEADS = 8
QK_NOPE_DIM = 128
QK_ROPE_DIM = 64
V_DIM = 128
Q_LORA = 384
KV_LORA = 256
ROPE_THETA = 10000.0
ATTN_SCALE = 1.0 / math.sqrt(QK_NOPE_DIM + QK_ROPE_DIM)
S5_GROUP = 16
S5_GROUPS = D_MODEL // S5_GROUP
S5_STATE = 64
ALPHA = (2 * DEPTH) ** 0.25
LN_EPS = 1e-5
RMS_EPS = 1e-6

LANES = 128
HEAD_PAD = 256
ROPE_HALF = QK_ROPE_DIM // 2
V_PAD = V_DIM + 16
LOG2_E = math.log2(math.e)
CHUNK = 128
TOK_TILE = 512
VMEM_LIMIT = 56 * 1024 * 1024

NT_DIMS = (((1,), (1,)), ((), ()))


def _dot(a, b):
    return jnp.dot(a, b, preferred_element_type=F32)


def _dot_nt(a, b):
    return lax.dot_general(a, b, NT_DIMS, preferred_element_type=F32)


def _params(*sem):
    return pltpu.CompilerParams(dimension_semantics=sem, vmem_limit_bytes=VMEM_LIMIT)


def _layer_norm(r, g, b):
    mu = jnp.mean(r, axis=-1, keepdims=True)
    d = r - mu
    var = jnp.mean(d * d, axis=-1, keepdims=True)
    return d * lax.rsqrt(var + LN_EPS) * g + b


def _rms_norm(x, g):
    return x * lax.rsqrt(jnp.mean(x * x, axis=-1, keepdims=True) + RMS_EPS) * g


def _silu(x):
    return x * jax.nn.sigmoid(x)


def _mla_proj_kernel(x_ref, wa_ref, gq_ref, gkv_ref, wqT_ref, wkn_ref, wvT_ref, wgT_ref,
                     cosT_ref, sinT_ref, cosk_ref, sina_ref, sinb_ref,
                     qT_ref, k_ref, vT_ref, sgT_ref):
    xb = x_ref[0].astype(BF16)
    ha = _dot(xb, wa_ref[...])
    cq = _rms_norm(ha[:, :Q_LORA], gq_ref[...]).astype(BF16)
    ckv = _rms_norm(ha[:, Q_LORA:Q_LORA + KV_LORA], gkv_ref[...]).astype(BF16)
    kr = ha[:, Q_LORA + KV_LORA:]

    qT = _dot_nt(wqT_ref[...], cq) * (ATTN_SCALE * LOG2_E)
    cosT = cosT_ref[...]
    sinT = sinT_ref[...]
    tm = qT.shape[1]
    zpad = jnp.zeros((HEAD_PAD - QK_NOPE_DIM - QK_ROPE_DIM, tm), F32)
    for h in range(N_HEADS):
        blk = qT[h * HEAD_PAD:(h + 1) * HEAD_PAD]
        t1 = blk[QK_NOPE_DIM:QK_NOPE_DIM + ROPE_HALF]
        t2 = blk[QK_NOPE_DIM + ROPE_HALF:QK_NOPE_DIM + QK_ROPE_DIM]
        out = jnp.concatenate(
            [blk[:QK_NOPE_DIM], t1 * cosT - t2 * sinT, t2 * cosT + t1 * sinT, zpad], axis=0)
        qT_ref[0, 0, h * HEAD_PAD:(h + 1) * HEAD_PAD, :] = out.astype(BF16)

    kn = _dot(ckv, wkn_ref[...])
    krr = (kr * cosk_ref[...]
           + pltpu.roll(kr, ROPE_HALF, 1) * sina_ref[...]
           + pltpu.roll(kr, LANES - ROPE_HALF, 1) * sinb_ref[...]).astype(BF16)
    for h in range(N_HEADS):
        k_ref[0, h, :, :QK_NOPE_DIM] = kn[:, h * QK_NOPE_DIM:(h + 1) * QK_NOPE_DIM].astype(BF16)
        k_ref[0, h, :, QK_NOPE_DIM:] = krr

    vT = _dot_nt(wvT_ref[...], ckv).astype(BF16)
    ones = jnp.ones((V_PAD - V_DIM, tm), BF16)
    for h in range(N_HEADS):
        vT_ref[0, 0, h * V_PAD:h * V_PAD + V_DIM, :] = vT[h * V_DIM:(h + 1) * V_DIM]
        vT_ref[0, 0, h * V_PAD + V_DIM:(h + 1) * V_PAD, :] = ones
    sgT_ref[0, 0] = _silu(_dot_nt(wgT_ref[...], xb)).astype(BF16)


def _mla_proj(x, w, tabs, tm):
    B, L, _ = x.shape
    nl = L // tm
    const = lambda shape: pl.BlockSpec(shape, lambda b, i: (0,) * len(shape))
    return pl.pallas_call(
        _mla_proj_kernel,
        grid=(B, nl),
        in_specs=[
            pl.BlockSpec((1, tm, D_MODEL), lambda b, i: (b, i, 0)),
            const(w['wa'].shape), const(w['gq'].shape), const(w['gkv'].shape),
            const(w['wqT'].shape), const(w['wkn'].shape), const(w['wvT'].shape), const(w['wgT'].shape),
            pl.BlockSpec((ROPE_HALF, tm), lambda b, i: (0, i)),
            pl.BlockSpec((ROPE_HALF, tm), lambda b, i: (0, i)),
            pl.BlockSpec((tm, LANES), lambda b, i: (i, 0)),
            pl.BlockSpec((tm, LANES), lambda b, i: (i, 0)),
            pl.BlockSpec((tm, LANES), lambda b, i: (i, 0)),
        ],
        out_specs=[
            pl.BlockSpec((1, 1, N_HEADS * HEAD_PAD, tm), lambda b, i: (b, i, 0, 0)),
            pl.BlockSpec((1, N_HEADS, tm, HEAD_PAD), lambda b, i: (b, 0, i, 0)),
            pl.BlockSpec((1, 1, N_HEADS * V_PAD, tm), lambda b, i: (b, i, 0, 0)),
            pl.BlockSpec((1, 1, N_HEADS * V_DIM, tm), lambda b, i: (b, i, 0, 0)),
        ],
        out_shape=[
            jax.ShapeDtypeStruct((B, nl, N_HEADS * HEAD_PAD, tm), BF16),
            jax.ShapeDtypeStruct((B, N_HEADS, L, HEAD_PAD), BF16),
            jax.ShapeDtypeStruct((B, nl, N_HEADS * V_PAD, tm), BF16),
            jax.ShapeDtypeStruct((B, nl, N_HEADS * V_DIM, tm), BF16),
        ],
        compiler_params=_params("parallel", "parallel"),
        name="mla_proj",
    )(x, w['wa'], w['gq'], w['gkv'], w['wqT'], w['wkn'], w['wvT'], w['wgT'],
      tabs['cosT'], tabs['sinT'], tabs['cosk'], tabs['sina'], tabs['sinb'])


def _attn_kernel(qT_ref, k_ref, vT_ref, sgT_ref, o_ref, m_sc, acc_sc, sa_sc, sb_sc, *, tk, nk):
    qT = qT_ref[0, 0]
    m_sc[...] = jnp.full(m_sc.shape, -jnp.inf, F32)
    acc_sc[...] = jnp.zeros(acc_sc.shape, F32)

    def scores(j, s_ref):
        ks = k_ref[0, 0, pl.ds(pl.multiple_of(j * tk, tk), tk), :]
        s_ref[...] = _dot(ks, qT)

    def update(j, s_ref):
        sT = s_ref[...]
        m_prev = m_sc[...]
        m_new = jnp.maximum(m_prev, jnp.max(sT, axis=0, keepdims=True))
        alpha = jnp.exp2(m_prev - m_new)
        p = jnp.exp2((sT - m_new).astype(BF16))
        acc_sc[...] = acc_sc[...] * alpha + _dot(vT_ref[0, j], p)
        m_sc[...] = m_new

    scores(0, sa_sc)

    def body(jj, carry):
        j = 2 * jj
        scores(j + 1, sb_sc)
        update(j, sa_sc)
        scores(j + 2, sa_sc)
        update(j + 1, sb_sc)
        return carry

    lax.fori_loop(0, nk // 2 - 1, body, 0)
    scores(nk - 1, sb_sc)
    update(nk - 2, sa_sc)
    update(nk - 1, sb_sc)
    o = acc_sc[:V_DIM] / acc_sc[V_DIM:V_DIM + 1] * sgT_ref[0, 0].astype(F32)
    o_ref[0, 0] = o.astype(BF16)


def _attention(qT, k, vT, sgT, tm):
    B, nl, _, _ = qT.shape
    L = nl * tm
    kern = functools.partial(_attn_kernel, tk=tm, nk=nl)
    return pl.pallas_call(
        kern,
        grid=(B, N_HEADS, nl),
        in_specs=[
            pl.BlockSpec((1, 1, HEAD_PAD, tm), lambda b, h, i: (b, i, h, 0)),
            pl.BlockSpec((1, 1, L, HEAD_PAD), lambda b, h, i: (b, h, 0, 0)),
            pl.BlockSpec((1, nl, V_PAD, tm), lambda b, h, i: (b, 0, h, 0)),
            pl.BlockSpec((1, 1, V_DIM, tm), lambda b, h, i: (b, i, h, 0)),
        ],
        out_specs=pl.BlockSpec((1, 1, V_DIM, tm), lambda b, h, i: (b, i, h, 0)),
        out_shape=jax.ShapeDtypeStruct((B, nl, N_HEADS * V_DIM, tm), BF16),
        scratch_shapes=[pltpu.VMEM((1, tm), F32), pltpu.VMEM((V_PAD, tm), F32),
                        pltpu.VMEM((tm, tm), F32), pltpu.VMEM((tm, tm), F32)],
        compiler_params=_params("parallel", "parallel", "arbitrary"),
        name="mla_attn",
    )(qT, k, vT, sgT)


def _out_kernel(zT_ref, x_ref, woT_ref, g_ref, b_ref, o_ref):
    yT = _dot(woT_ref[...], zT_ref[0, 0])
    r = ALPHA * x_ref[0] + yT.T
    o_ref[0] = _layer_norm(r, g_ref[...], b_ref[...])


def _mla_out(ogT, x, woT, ln_g, ln_b, tm):
    B, L, _ = x.shape
    nl = L // tm
    return pl.pallas_call(
        _out_kernel,
        grid=(B, nl),
        in_specs=[
            pl.BlockSpec((1, 1, D_MODEL, tm), lambda b, i: (b, i, 0, 0)),
            pl.BlockSpec((1, tm, D_MODEL), lambda b, i: (b, i, 0)),
            pl.BlockSpec((D_MODEL, D_MODEL), lambda b, i: (0, 0)),
            pl.BlockSpec((1, D_MODEL), lambda b, i: (0, 0)),
            pl.BlockSpec((1, D_MODEL), lambda b, i: (0, 0)),
        ],
        out_specs=pl.BlockSpec((1, tm, D_MODEL), lambda b, i: (b, i, 0)),
        out_shape=jax.ShapeDtypeStruct((B, L, D_MODEL), F32),
        compiler_params=_params("parallel", "parallel"),
        name="mla_out",
    )(ogT, x, woT, ln_g, ln_b)


def _s5_in_kernel(x_ref, wuT_ref, wgT_ref, uT_ref, sgT_ref):
    xb = x_ref[0].astype(BF16)
    uT_ref[...] = _dot_nt(wuT_ref[...], xb).astype(BF16)
    sgT_ref[...] = _silu(_dot_nt(wgT_ref[...], xb)).astype(BF16)


def _s5_in(x, wuT, wgT, tm):
    B, L, _ = x.shape
    nl = L // tm
    tok = pl.BlockSpec((D_MODEL, tm), lambda b, i: (0, b * nl + i))
    wspec = pl.BlockSpec((D_MODEL, D_MODEL), lambda b, i: (0, 0))
    return pl.pallas_call(
        _s5_in_kernel,
        grid=(B, nl),
        in_specs=[pl.BlockSpec((1, tm, D_MODEL), lambda b, i: (b, i, 0)), wspec, wspec],
        out_specs=[tok, tok],
        out_shape=[jax.ShapeDtypeStruct((D_MODEL, B * L), BF16)] * 2,
        compiler_params=_params("parallel", "parallel"),
        name="s5_in",
    )(x, wuT, wgT)


def _toep_kernel(cbT_ref, pwT_ref, o_ref, kv_sc):
    kv_sc[...] = jnp.dot(cbT_ref[0], pwT_ref[0], preferred_element_type=F32,
                         precision=lax.Precision.HIGHEST)

    def body(pi, carry):
        row0 = pl.multiple_of(pi * CHUNK, CHUNK)
        for po in range(S5_GROUP):
            kv = kv_sc[pl.ds(pi * S5_GROUP + po, 1), :]
            rolled = pltpu.roll(jnp.broadcast_to(kv, (CHUNK, 2 * CHUNK)), 0, 1, stride=1, stride_axis=0)
            o_ref[0, pl.ds(row0, CHUNK), po * CHUNK:(po + 1) * CHUNK] = rolled[:, CHUNK:].astype(BF16)
        return carry

    lax.fori_loop(0, S5_GROUP, body, 0)


def _toeplitz(cbT, pwT):
    G = cbT.shape[0]
    n = S5_GROUP * CHUNK
    blk = pl.BlockSpec((1, 2 * CHUNK, 2 * CHUNK), lambda g: (g, 0, 0))
    return pl.pallas_call(
        _toep_kernel,
        grid=(G,),
        in_specs=[blk, blk],
        out_specs=pl.BlockSpec((1, n, n), lambda g: (g, 0, 0)),
        out_shape=jax.ShapeDtypeStruct((G, n, n), BF16),
        scratch_shapes=[pltpu.VMEM((2 * CHUNK, 2 * CHUNK), F32)],
        compiler_params=_params("parallel"),
        name="s5_toeplitz",
    )(cbT, pwT)


def _scan_kernel(u_ref, tm_ref, rs_ref, ri_ref, ac_ref, y_ref, sf_sc, sb_sc, hf_sc, hb_sc, *, nseq, nchunk):
    lhs = jnp.concatenate([u_ref[p] for p in range(S5_GROUP)], axis=1)
    y = _dot(lhs, tm_ref[0])
    s = _dot(lhs, rs_ref[0])
    half = 2 * S5_STATE
    sf_sc[...] = s[:, :half]
    sb_sc[...] = s[:, half:]
    ac = ac_ref[0]

    def step(state, are, aim, c, s_sc, h_sc):
        rows = pl.ds(c, nseq, stride=nchunk)
        h_sc[rows, :] = state
        return are * state + aim * pltpu.roll(state, S5_STATE, 1) + s_sc[rows, :]

    e = jnp.zeros((nseq, half), F32)
    f = jnp.zeros((nseq, half), F32)
    for c in range(nchunk):
        e = step(e, ac[0:1], ac[1:2], c, sf_sc, hf_sc)
        f = step(f, ac[2:3], ac[3:4], nchunk - 1 - c, sb_sc, hb_sc)

    h = jnp.concatenate([hf_sc[...], hb_sc[...]], axis=1).astype(BF16)
    y = y + _dot(h, ri_ref[0])
    for p in range(S5_GROUP):
        y_ref[p] = y[:, p * CHUNK:(p + 1) * CHUNK].astype(BF16)


def _s5_scan(ubT, toep, rs, ri, ac, nseq, nchunk):
    W, T = ubT.shape
    R = T // CHUNK
    u3 = ubT.reshape(W, R, CHUNK)
    n = S5_GROUP * CHUNK
    kern = functools.partial(_scan_kernel, nseq=nseq, nchunk=nchunk)
    y3 = pl.pallas_call(
        kern,
        grid=(S5_GROUPS,),
        in_specs=[
            pl.BlockSpec((S5_GROUP, R, CHUNK), lambda g: (g, 0, 0)),
            pl.BlockSpec((1, n, n), lambda g: (g, 0, 0)),
            pl.BlockSpec((1, n, 4 * S5_STATE), lambda g: (g, 0, 0)),
            pl.BlockSpec((1, 4 * S5_STATE, n), lambda g: (g, 0, 0)),
            pl.BlockSpec((1, 8, 2 * S5_STATE), lambda g: (g, 0, 0)),
        ],
        out_specs=pl.BlockSpec((S5_GROUP, R, CHUNK), lambda g: (g, 0, 0)),
        out_shape=jax.ShapeDtypeStruct((W, R, CHUNK), BF16),
        scratch_shapes=[pltpu.VMEM((R, 2 * S5_STATE), F32)] * 4,
        compiler_params=_params("parallel"),
        name="s5_scan",
    )(u3, toep, rs, ri, ac)
    return y3.reshape(W, T)


def _s5_out_kernel(ys_ref, u_ref, sg_ref, x_ref, d_ref, wgluT_ref, bglu_ref, woT_ref, g_ref, b_ref, o_ref):
    tm = ys_ref.shape[1]
    rep = tm // LANES
    d = jnp.tile(d_ref[...], (1, rep))
    bglu = jnp.tile(bglu_ref[...], (1, rep))
    y = ys_ref[...].astype(F32) + d * u_ref[...].astype(F32)
    y = jax.nn.gelu(y)
    z = _dot(wgluT_ref[...], y.astype(BF16)) + bglu
    v = (y * jax.nn.sigmoid(z) * sg_ref[...].astype(F32)).astype(BF16)
    oT = _dot(woT_ref[...], v)
    r = ALPHA * x_ref[0] + oT.T
    o_ref[0] = _layer_norm(r, g_ref[...], b_ref[...])


def _s5_out(ysT, ubT, sgT, x, w, ln_g, ln_b, tm):
    B, L, _ = x.shape
    nl = L // tm
    tok = pl.BlockSpec((D_MODEL, tm), lambda b, i: (0, b * nl + i))
    sq = pl.BlockSpec((D_MODEL, D_MODEL), lambda b, i: (0, 0))
    col = pl.BlockSpec((D_MODEL, LANES), lambda b, i: (0, 0))
    row = pl.BlockSpec((1, D_MODEL), lambda b, i: (0, 0))
    return pl.pallas_call(
        _s5_out_kernel,
        grid=(B, nl),
        in_specs=[tok, tok, tok, pl.BlockSpec((1, tm, D_MODEL), lambda b, i: (b, i, 0)),
                  col, sq, col, sq, row, row],
        out_specs=pl.BlockSpec((1, tm, D_MODEL), lambda b, i: (b, i, 0)),
        out_shape=jax.ShapeDtypeStruct((B, L, D_MODEL), F32),
        compiler_params=_params("parallel", "parallel"),
        name="s5_out",
    )(ysT, ubT, sgT, x, w['d'], w['wgluT'], w['bglu'], w['woT'], ln_g, ln_b)


def _prep_mla(w_in, g_q, w_q_up, g_kv, w_kv_up, w_out):
    nkv = Q_LORA + KV_LORA
    wa = jnp.concatenate([w_in[:, :nkv + QK_ROPE_DIM],
                          jnp.zeros((D_MODEL, LANES - QK_ROPE_DIM), F32)], axis=1)
    wq = w_q_up.reshape(Q_LORA, N_HEADS, QK_NOPE_DIM + QK_ROPE_DIM)
    wq = jnp.pad(wq, ((0, 0), (0, 0), (0, HEAD_PAD - QK_NOPE_DIM - QK_ROPE_DIM)))
    wkv = w_kv_up.reshape(KV_LORA, N_HEADS, QK_NOPE_DIM + V_DIM)
    return dict(
        wa=wa.astype(BF16),
        gq=g_q.reshape(1, Q_LORA), gkv=g_kv.reshape(1, KV_LORA),
        wqT=wq.reshape(Q_LORA, N_HEADS * HEAD_PAD).T.astype(BF16),
        wkn=wkv[:, :, :QK_NOPE_DIM].reshape(KV_LORA, N_HEADS * QK_NOPE_DIM).astype(BF16),
        wvT=wkv[:, :, QK_NOPE_DIM:].reshape(KV_LORA, N_HEADS * V_DIM).T.astype(BF16),
        wgT=w_in[:, nkv + QK_ROPE_DIM:].T.astype(BF16),
        woT=w_out.T.astype(BF16),
    )


def _rope_tables(L):
    inv = ROPE_THETA ** (-jnp.arange(0, QK_ROPE_DIM, 2, dtype=F32) / QK_ROPE_DIM)
    ang = jnp.arange(L, dtype=F32)[:, None] * inv[None, :]
    cos, sin = jnp.cos(ang), jnp.sin(ang)
    z = jnp.zeros_like(cos)
    z2 = jnp.zeros((L, LANES - QK_ROPE_DIM), F32)
    return dict(
        cosT=cos.T, sinT=sin.T,
        cosk=jnp.concatenate([cos, cos, z2], axis=1),
        sina=jnp.concatenate([z, sin, z2], axis=1),
        sinb=jnp.concatenate([-sin, z, z2], axis=1),
    )


def _powers(lam_bar, n):
    pw = jnp.stack([jnp.ones_like(lam_bar), lam_bar], axis=-2)
    top = lam_bar
    while pw.shape[-2] - 1 < n:
        pw = jnp.concatenate([pw, pw[..., 1:, :] * top[..., None, :]], axis=-2)
        top = top * top
    return pw


def _reim(z, axis):
    return jnp.concatenate([jnp.real(z), jnp.imag(z)], axis=axis)


def _prep_s5(w_in, a_re, a_im, log_step, b_re, b_im, c_re, c_im, d, w_glu, b_glu, w_out):
    G, N, P, T = S5_GROUPS, S5_STATE, S5_GROUP, CHUNK
    lam = lax.complex(a_re, a_im)
    step = jnp.exp(log_step)[..., None]
    lam_bar = jnp.exp(lam * step)
    b_bar = ((lam_bar - 1.0) / lam)[..., None] * lax.complex(b_re, b_im)
    c = lax.complex(c_re, c_im)
    pw = _powers(lam_bar, T)

    cb = c[:, :, None, :, :] * jnp.swapaxes(b_bar, -1, -2)[:, :, :, None, :]
    cb = cb.reshape(2, G, P * P, N)
    cbT = jnp.concatenate([_reim(cb[0], -1), _reim(cb[1], -1)], axis=-1)
    pf = jnp.swapaxes(pw[0, :, :T], -1, -2)
    pb = jnp.swapaxes(pw[1, :, :T], -1, -2)[..., ::-1]
    zf = jnp.zeros((G, N, T), pf.dtype)
    pf = jnp.concatenate([zf, pf], axis=-1)
    pb = jnp.concatenate([zf[..., :1], pb, zf[..., :T - 1]], axis=-1)
    pwT = jnp.concatenate([jnp.real(pf), -jnp.imag(pf), jnp.real(pb), -jnp.imag(pb)], axis=1)

    sf = pw[0, :, T - 1::-1][:, :T]
    sb = pw[1, :, :T]
    rs_f = jnp.swapaxes(b_bar[0], -1, -2)[:, :, None, :] * sf[:, None, :, :]
    rs_b = jnp.swapaxes(b_bar[1], -1, -2)[:, :, None, :] * sb[:, None, :, :]
    rs = jnp.concatenate([_reim(rs_f, -1), _reim(rs_b, -1)], axis=-1).reshape(G, P * T, 4 * N)

    of = pw[0, :, 1:T + 1]
    ob = pw[1, :, T:0:-1]
    ri_f = c[0][:, :, None, :] * of[:, None, :, :]
    ri_b = c[1][:, :, None, :] * ob[:, None, :, :]
    ri = jnp.concatenate([jnp.real(ri_f), -jnp.imag(ri_f), jnp.real(ri_b), -jnp.imag(ri_b)], axis=-1)
    ri = jnp.swapaxes(ri.reshape(G, P * T, 4 * N), -1, -2)

    a = pw[:, :, T]
    are, aim = jnp.real(a), jnp.imag(a)
    rows = [jnp.concatenate([are[0], are[0]], -1), jnp.concatenate([-aim[0], aim[0]], -1),
            jnp.concatenate([are[1], are[1]], -1), jnp.concatenate([-aim[1], aim[1]], -1)]
    ac = jnp.stack(rows + [jnp.zeros_like(rows[0])] * 4, axis=1)

    return dict(
        wuT=w_in[:, :D_MODEL].T.astype(BF16), wgT=w_in[:, D_MODEL:].T.astype(BF16),
        cbT=cbT, pwT=pwT, rs=rs.astype(BF16), ri=ri.astype(BF16), ac=ac,
        d=jnp.broadcast_to(d[:, None], (D_MODEL, LANES)),
        wgluT=w_glu.T.astype(BF16),
        bglu=jnp.broadcast_to(b_glu[:, None], (D_MODEL, LANES)),
        woT=w_out.T.astype(BF16),
    )


def _mla_layer(x, w, tabs, ln_g, ln_b, tm):
    qT, k, vT, sgT = _mla_proj(x, w, tabs, tm)
    ogT = _attention(qT, k, vT, sgT, tm)
    return _mla_out(ogT, x, w['woT'], ln_g, ln_b, tm)


def _s5_layer(x, w, toep, ln_g, ln_b, tm):
    B, L, _ = x.shape
    ubT, sgT = _s5_in(x, w['wuT'], w['wgT'], tm)
    ysT = _s5_scan(ubT, toep, w['rs'], w['ri'], w['ac'], B, L // CHUNK)
    return _s5_out(ysT, ubT, sgT, x, w, ln_g, ln_b, tm)


def _trunk(x, mla_w, s5_w, toeps, ln_g, ln_b):
    L = x.shape[1]
    tm = min(TOK_TILE, L)
    tabs = _rope_tables(L)
    for i in range(DEPTH):
        g, b = ln_g[i].reshape(1, D_MODEL), ln_b[i].reshape(1, D_MODEL)
        if i % 2 == 0:
            x = _mla_layer(x, mla_w[i // 2], tabs, g, b, tm)
        else:
            x = _s5_layer(x, s5_w[i // 2], toeps[i // 2], g, b, tm)
    return x


def kernel(x_prompt, x_sample, mla_w_in, mla_g_q, mla_w_q_up, mla_g_kv, mla_w_kv_up, mla_w_out,
           s5_w_in, s5_a_re, s5_a_im, s5_log_step, s5_b_re, s5_b_im, s5_c_re, s5_c_im, s5_d,
           s5_w_glu, s5_b_glu, s5_w_out, ln_g, ln_b):
    mla = (mla_w_in, mla_g_q, mla_w_q_up, mla_g_kv, mla_w_kv_up, mla_w_out)
    s5 = (s5_w_in, s5_a_re, s5_a_im, s5_log_step, s5_b_re, s5_b_im, s5_c_re, s5_c_im, s5_d,
          s5_w_glu, s5_b_glu, s5_w_out)
    mla_w = [_prep_mla(*[w[j] for w in mla]) for j in range(mla_w_in.shape[0])]
    s5_w = [_prep_s5(*[w[j] for w in s5]) for j in range(s5_w_in.shape[0])]
    toeps = [_toeplitz(w['cbT'], w['pwT']) for w in s5_w]
    y_prompt = _trunk(x_prompt, mla_w, s5_w, toeps, ln_g, ln_b)
    y_sample = _trunk(x_sample, mla_w, s5_w, toeps, ln_g, ln_b)
    return (y_prompt, y_sample)
```

```python
import functools
import math

import jax
import jax.numpy as jnp
from jax import lax
from jax.experimental import pallas as pl
from jax.experimental.pallas import tpu as pltpu

F32 = jnp.float32
BF16 = jnp.bfloat16

D_MODEL = 1024
DEPTH = 4
N_HEADS = 8
QK_NOPE_DIM = 128
QK_ROPE_DIM = 64
V_DIM = 128
Q_LORA = 384
KV_LORA = 256
ROPE_THETA = 10000.0
ATTN_SCALE = 1.0 / math.sqrt(QK_NOPE_DIM + QK_ROPE_DIM)
S5_GROUP = 16
S5_GROUPS = D_MODEL // S5_GROUP
S5_STATE = 64
ALPHA = (2 * DEPTH) ** 0.25
LN_EPS = 1e-5
RMS_EPS = 1e-6

LANES = 128
HEAD_PAD = 256
ROPE_HALF = QK_ROPE_DIM // 2
V_PAD = V_DIM + 16
LOG2_E = math.log2(math.e)
CHUNK = 128
TOK_TILE = 512
ATTN_Q_TILES = 2
VMEM_LIMIT = 56 * 1024 * 1024

NT_DIMS = (((1,), (1,)), ((), ()))
TN_DIMS = (((0,), (0,)), ((), ()))


def _dot(a, b):
    return jnp.dot(a, b, preferred_element_type=F32)


def _dot_nt(a, b):
    return lax.dot_general(a, b, NT_DIMS, preferred_element_type=F32)


def _dot_tn(a, b):
    return lax.dot_general(a, b, TN_DIMS, preferred_element_type=F32)


def _params(*sem):
    return pltpu.CompilerParams(dimension_semantics=sem, vmem_limit_bytes=VMEM_LIMIT)


def _layer_norm(r, g, b):
    mu = jnp.mean(r, axis=-1, keepdims=True)
    d = r - mu
    var = jnp.mean(d * d, axis=-1, keepdims=True)
    return d * lax.rsqrt(var + LN_EPS) * g + b


def _rms_norm(x, g):
    return x * lax.rsqrt(jnp.mean(x * x, axis=-1, keepdims=True) + RMS_EPS) * g


def _silu(x):
    return x * jax.nn.sigmoid(x)


def _mla_proj_kernel(x_ref, wa_ref, gq_ref, gkv_ref, wqT_ref, wkn_ref, wvT_ref, wgT_ref,
                     cosT_ref, sinT_ref, cosk_ref, sina_ref, sinb_ref,
                     qT_ref, k_ref, vT_ref, sgT_ref):
    xb = x_ref[0].astype(BF16)
    ha = _dot(xb, wa_ref[...])
    cq = _rms_norm(ha[:, :Q_LORA], gq_ref[...]).astype(BF16)
    ckv = _rms_norm(ha[:, Q_LORA:Q_LORA + KV_LORA], gkv_ref[...]).astype(BF16)
    kr = ha[:, Q_LORA + KV_LORA:]

    qT = _dot_nt(wqT_ref[...], cq) * (ATTN_SCALE * LOG2_E)
    cosT = cosT_ref[...]
    sinT = sinT_ref[...]
    tm = qT.shape[1]
    zpad = jnp.zeros((HEAD_PAD - QK_NOPE_DIM - QK_ROPE_DIM, tm), F32)
    for h in range(N_HEADS):
        blk = qT[h * HEAD_PAD:(h + 1) * HEAD_PAD]
        t1 = blk[QK_NOPE_DIM:QK_NOPE_DIM + ROPE_HALF]
        t2 = blk[QK_NOPE_DIM + ROPE_HALF:QK_NOPE_DIM + QK_ROPE_DIM]
        out = jnp.concatenate(
            [blk[:QK_NOPE_DIM], t1 * cosT - t2 * sinT, t2 * cosT + t1 * sinT, zpad], axis=0)
        qT_ref[0, 0, h * HEAD_PAD:(h + 1) * HEAD_PAD, :] = out.astype(BF16)

    kn = _dot(ckv, wkn_ref[...])
    krr = (kr * cosk_ref[...]
           + pltpu.roll(kr, ROPE_HALF, 1) * sina_ref[...]
           + pltpu.roll(kr, LANES - ROPE_HALF, 1) * sinb_ref[...]).astype(BF16)
    for h in range(N_HEADS):
        k_ref[0, h, :, :QK_NOPE_DIM] = kn[:, h * QK_NOPE_DIM:(h + 1) * QK_NOPE_DIM].astype(BF16)
        k_ref[0, h, :, QK_NOPE_DIM:] = krr

    vT = _dot_nt(wvT_ref[...], ckv).astype(BF16)
    ones = jnp.ones((V_PAD - V_DIM, tm), BF16)
    for h in range(N_HEADS):
        vT_ref[0, 0, h * V_PAD:h * V_PAD + V_DIM, :] = vT[h * V_DIM:(h + 1) * V_DIM]
        vT_ref[0, 0, h * V_PAD + V_DIM:(h + 1) * V_PAD, :] = ones
    sgT_ref[0, 0] = _silu(_dot_nt(wgT_ref[...], xb)).astype(BF16)


def _mla_proj(x, w, tabs, tm):
    B, L, _ = x.shape
    nl = L // tm
    const = lambda shape: pl.BlockSpec(shape, lambda b, i: (0,) * len(shape))
    return pl.pallas_call(
        _mla_proj_kernel,
        grid=(B, nl),
        in_specs=[
            pl.BlockSpec((1, tm, D_MODEL), lambda b, i: (b, i, 0)),
            const(w['wa'].shape), const(w['gq'].shape), const(w['gkv'].shape),
            const(w['wqT'].shape), const(w['wkn'].shape), const(w['wvT'].shape), const(w['wgT'].shape),
            pl.BlockSpec((ROPE_HALF, tm), lambda b, i: (0, i)),
            pl.BlockSpec((ROPE_HALF, tm), lambda b, i: (0, i)),
            pl.BlockSpec((tm, LANES), lambda b, i: (i, 0)),
            pl.BlockSpec((tm, LANES), lambda b, i: (i, 0)),
            pl.BlockSpec((tm, LANES), lambda b, i: (i, 0)),
        ],
        out_specs=[
            pl.BlockSpec((1, 1, N_HEADS * HEAD_PAD, tm), lambda b, i: (b, i, 0, 0)),
            pl.BlockSpec((1, N_HEADS, tm, HEAD_PAD), lambda b, i: (b, 0, i, 0)),
            pl.BlockSpec((1, 1, N_HEADS * V_PAD, tm), lambda b, i: (b, i, 0, 0)),
            pl.BlockSpec((1, 1, N_HEADS * V_DIM, tm), lambda b, i: (b, i, 0, 0)),
        ],
        out_shape=[
            jax.ShapeDtypeStruct((B, nl, N_HEADS * HEAD_PAD, tm), BF16),
            jax.ShapeDtypeStruct((B, N_HEADS, L, HEAD_PAD), BF16),
            jax.ShapeDtypeStruct((B, nl, N_HEADS * V_PAD, tm), BF16),
            jax.ShapeDtypeStruct((B, nl, N_HEADS * V_DIM, tm), BF16),
        ],
        compiler_params=_params("parallel", "parallel"),
        name="mla_proj",
    )(x, w['wa'], w['gq'], w['gkv'], w['wqT'], w['wkn'], w['wvT'], w['wgT'],
      tabs['cosT'], tabs['sinT'], tabs['cosk'], tabs['sina'], tabs['sinb'])


def _attn_kernel(qT_ref, k_ref, vT_ref, sgT_ref, o_ref, m_sc, acc_sc, sa_sc, sb_sc, ma_sc, mb_sc, *, tk, nk, nq):
    tq = qT_ref.shape[3]
    qT = jnp.concatenate([qT_ref[0, t] for t in range(nq)], axis=1)
    m_sc[...] = jnp.full(m_sc.shape, -jnp.inf, F32)
    acc_sc[...] = jnp.zeros(acc_sc.shape, F32)

    def scores(j, s_ref, mx_ref):
        ks = k_ref[0, 0, pl.ds(pl.multiple_of(j * tk, tk), tk), :]
        sT = _dot(ks, qT)
        s_ref[...] = sT
        mx_ref[...] = jnp.max(sT, axis=0, keepdims=True)

    def update(j, s_ref, mx_ref):
        m_prev = m_sc[...]
        m_new = jnp.maximum(m_prev, mx_ref[...])
        alpha = jnp.exp2(m_prev - m_new)
        p = jnp.exp2((s_ref[...] - m_new).astype(BF16))
        acc_sc[...] = acc_sc[...] * alpha + _dot(vT_ref[0, j], p)
        m_sc[...] = m_new

    scores(0, sa_sc, ma_sc)

    def body(jj, carry):
        j = 2 * jj
        scores(j + 1, sb_sc, mb_sc)
        update(j, sa_sc, ma_sc)
        scores(j + 2, sa_sc, ma_sc)
        update(j + 1, sb_sc, mb_sc)
        return carry

    lax.fori_loop(0, nk // 2 - 1, body, 0)
    scores(nk - 1, sb_sc, mb_sc)
    update(nk - 2, sa_sc, ma_sc)
    update(nk - 1, sb_sc, mb_sc)
    for t in range(nq):
        cols = slice(t * tq, (t + 1) * tq)
        o = acc_sc[:V_DIM, cols] / acc_sc[V_DIM:V_DIM + 1, cols] * sgT_ref[0, t].astype(F32)
        o_ref[0, t] = o.astype(BF16)


def _attention(qT, k, vT, sgT, tm):
    B, nl, _, _ = qT.shape
    L = nl * tm
    nq = ATTN_Q_TILES
    kern = functools.partial(_attn_kernel, tk=tm, nk=nl, nq=nq)
    return pl.pallas_call(
        kern,
        grid=(B, N_HEADS, nl // nq),
        in_specs=[
            pl.BlockSpec((1, nq, HEAD_PAD, tm), lambda b, h, i: (b, i, h, 0)),
            pl.BlockSpec((1, 1, L, HEAD_PAD), lambda b, h, i: (b, h, 0, 0)),
            pl.BlockSpec((1, nl, V_PAD, tm), lambda b, h, i: (b, 0, h, 0)),
            pl.BlockSpec((1, nq, V_DIM, tm), lambda b, h, i: (b, i, h, 0)),
        ],
        out_specs=pl.BlockSpec((1, nq, V_DIM, tm), lambda b, h, i: (b, i, h, 0)),
        out_shape=jax.ShapeDtypeStruct((B, nl, N_HEADS * V_DIM, tm), BF16),
        scratch_shapes=[pltpu.VMEM((1, nq * tm), F32), pltpu.VMEM((V_PAD, nq * tm), F32),
                        pltpu.VMEM((tm, nq * tm), F32), pltpu.VMEM((tm, nq * tm), F32),
                        pltpu.VMEM((1, nq * tm), F32), pltpu.VMEM((1, nq * tm), F32)],
        compiler_params=_params("parallel", "parallel", "arbitrary"),
        name="mla_attn",
    )(qT, k, vT, sgT)


def _out_kernel(zT_ref, x_ref, wo_ref, g_ref, b_ref, o_ref):
    y = _dot_tn(zT_ref[0, 0], wo_ref[...])
    r = ALPHA * x_ref[0] + y
    o_ref[0] = _layer_norm(r, g_ref[...], b_ref[...])


def _mla_out(ogT, x, woT, ln_g, ln_b, tm):
    B, L, _ = x.shape
    nl = L // tm
    return pl.pallas_call(
        _out_kernel,
        grid=(B, nl),
        in_specs=[
            pl.BlockSpec((1, 1, D_MODEL, tm), lambda b, i: (b, i, 0, 0)),
            pl.BlockSpec((1, tm, D_MODEL), lambda b, i: (b, i, 0)),
            pl.BlockSpec((D_MODEL, D_MODEL), lambda b, i: (0, 0)),
            pl.BlockSpec((1, D_MODEL), lambda b, i: (0, 0)),
            pl.BlockSpec((1, D_MODEL), lambda b, i: (0, 0)),
        ],
        out_specs=pl.BlockSpec((1, tm, D_MODEL), lambda b, i: (b, i, 0)),
        out_shape=jax.ShapeDtypeStruct((B, L, D_MODEL), F32),
        compiler_params=_params("parallel", "parallel"),
        name="mla_out",
    )(ogT, x, woT, ln_g, ln_b)


def _s5_in_kernel(x_ref, wuT_ref, wgT_ref, uT_ref, sgT_ref):
    xb = x_ref[0].astype(BF16)
    uT_ref[...] = _dot_nt(wuT_ref[...], xb).astype(BF16)
    sgT_ref[...] = _silu(_dot_nt(wgT_ref[...], xb)).astype(BF16)


def _s5_in(x, wuT, wgT, tm):
    B, L, _ = x.shape
    nl = L // tm
    tok = pl.BlockSpec((D_MODEL, tm), lambda b, i: (0, b * nl + i))
    wspec = pl.BlockSpec((D_MODEL, D_MODEL), lambda b, i: (0, 0))
    return pl.pallas_call(
        _s5_in_kernel,
        grid=(B, nl),
        in_specs=[pl.BlockSpec((1, tm, D_MODEL), lambda b, i: (b, i, 0)), wspec, wspec],
        out_specs=[tok, tok],
        out_shape=[jax.ShapeDtypeStruct((D_MODEL, B * L), BF16)] * 2,
        compiler_params=_params("parallel", "parallel"),
        name="s5_in",
    )(x, wuT, wgT)


def _toep_kernel(cbT_ref, pwT_ref, o_ref, kv_sc):
    kv_sc[...] = jnp.dot(cbT_ref[0], pwT_ref[0], preferred_element_type=F32,
                         precision=lax.Precision.HIGHEST)

    def body(pi, carry):
        row0 = pl.multiple_of(pi * CHUNK, CHUNK)
        for po in range(S5_GROUP):
            kv = kv_sc[pl.ds(pi * S5_GROUP + po, 1), :]
            rolled = pltpu.roll(jnp.broadcast_to(kv, (CHUNK, 2 * CHUNK)), 0, 1, stride=1, stride_axis=0)
            o_ref[0, pl.ds(row0, CHUNK), po * CHUNK:(po + 1) * CHUNK] = rolled[:, CHUNK:].astype(BF16)
        return carry

    lax.fori_loop(0, S5_GROUP, body, 0)


def _toeplitz(cbT, pwT):
    G = cbT.shape[0]
    n = S5_GROUP * CHUNK
    blk = pl.BlockSpec((1, 2 * CHUNK, 2 * CHUNK), lambda g: (g, 0, 0))
    return pl.pallas_call(
        _toep_kernel,
        grid=(G,),
        in_specs=[blk, blk],
        out_specs=pl.BlockSpec((1, n, n), lambda g: (g, 0, 0)),
        out_shape=jax.ShapeDtypeStruct((G, n, n), BF16),
        scratch_shapes=[pltpu.VMEM((2 * CHUNK, 2 * CHUNK), F32)],
        compiler_params=_params("parallel"),
        name="s5_toeplitz",
    )(cbT, pwT)


def _scan_kernel(u_ref, tm_ref, rs_ref, ri_ref, ac_ref, y_ref,
                 sre_sc, sim_sc, hfre_sc, hfim_sc, hbre_sc, hbim_sc, *, nseq, nchunk):
    lhs = jnp.concatenate([u_ref[p] for p in range(S5_GROUP)], axis=1)
    y = _dot(lhs, tm_ref[0])
    s = _dot(lhs, rs_ref[0])
    tile = 2 * S5_STATE
    sre_sc[...] = s[:, :tile]
    sim_sc[...] = s[:, tile:]
    ac = ac_ref[0]
    are, aim = ac[0:1], ac[1:2]
    is_fwd = lax.broadcasted_iota(jnp.int32, (nseq, tile), 1) < S5_STATE

    ere = jnp.zeros((nseq, tile), F32)
    eim = jnp.zeros((nseq, tile), F32)
    for c in range(nchunk):
        rf = pl.ds(c, nseq, stride=nchunk)
        rb = pl.ds(nchunk - 1 - c, nseq, stride=nchunk)
        hfre_sc[rf, :] = ere
        hfim_sc[rf, :] = eim
        hbre_sc[rb, :] = ere
        hbim_sc[rb, :] = eim
        s_re = jnp.where(is_fwd, sre_sc[rf, :], sre_sc[rb, :])
        s_im = jnp.where(is_fwd, sim_sc[rf, :], sim_sc[rb, :])
        ere, eim = are * ere - aim * eim + s_re, are * eim + aim * ere + s_im

    fwd_rows = lax.broadcasted_iota(jnp.int32, hfre_sc.shape, 1) < S5_STATE
    h = jnp.concatenate([jnp.where(fwd_rows, hfre_sc[...], hbre_sc[...]),
                         jnp.where(fwd_rows, hfim_sc[...], hbim_sc[...])], axis=1).astype(BF16)
    y = y + _dot(h, ri_ref[0])
    for p in range(S5_GROUP):
        y_ref[p] = y[:, p * CHUNK:(p + 1) * CHUNK].astype(BF16)


def _s5_scan(ubT, toep, rs, ri, ac, nseq, nchunk):
    W, T = ubT.shape
    R = T // CHUNK
    u3 = ubT.reshape(W, R, CHUNK)
    n = S5_GROUP * CHUNK
    kern = functools.partial(_scan_kernel, nseq=nseq, nchunk=nchunk)
    y3 = pl.pallas_call(
        kern,
        grid=(S5_GROUPS,),
        in_specs=[
            pl.BlockSpec((S5_GROUP, R, CHUNK), lambda g: (g, 0, 0)),
            pl.BlockSpec((1, n, n), lambda g: (g, 0, 0)),
            pl.BlockSpec((1, n, 4 * S5_STATE), lambda g: (g, 0, 0)),
            pl.BlockSpec((1, 4 * S5_STATE, n), lambda g: (g, 0, 0)),
            pl.BlockSpec((1, 8, 2 * S5_STATE), lambda g: (g, 0, 0)),
        ],
        out_specs=pl.BlockSpec((S5_GROUP, R, CHUNK), lambda g: (g, 0, 0)),
        out_shape=jax.ShapeDtypeStruct((W, R, CHUNK), BF16),
        scratch_shapes=[pltpu.VMEM((R, 2 * S5_STATE), F32)] * 6,
        compiler_params=_params("parallel"),
        name="s5_scan",
    )(u3, toep, rs, ri, ac)
    return y3.reshape(W, T)


def _s5_out_kernel(ys_ref, u_ref, sg_ref, x_ref, d_ref, wgluT_ref, bglu_ref, wo_ref, g_ref, b_ref, o_ref):
    tm = ys_ref.shape[1]
    rep = tm // LANES
    d = jnp.tile(d_ref[...], (1, rep))
    bglu = jnp.tile(bglu_ref[...], (1, rep))
    y = ys_ref[...].astype(F32) + d * u_ref[...].astype(F32)
    y = jax.nn.gelu(y)
    z = _dot(wgluT_ref[...], y.astype(BF16)) + bglu
    v = (y * jax.nn.sigmoid(z) * sg_ref[...].astype(F32)).astype(BF16)
    r = ALPHA * x_ref[0] + _dot_tn(v, wo_ref[...])
    o_ref[0] = _layer_norm(r, g_ref[...], b_ref[...])


def _s5_out(ysT, ubT, sgT, x, w, ln_g, ln_b, tm):
    B, L, _ = x.shape
    nl = L // tm
    tok = pl.BlockSpec((D_MODEL, tm), lambda b, i: (0, b * nl + i))
    sq = pl.BlockSpec((D_MODEL, D_MODEL), lambda b, i: (0, 0))
    col = pl.BlockSpec((D_MODEL, LANES), lambda b, i: (0, 0))
    row = pl.BlockSpec((1, D_MODEL), lambda b, i: (0, 0))
    return pl.pallas_call(
        _s5_out_kernel,
        grid=(B, nl),
        in_specs=[tok, tok, tok, pl.BlockSpec((1, tm, D_MODEL), lambda b, i: (b, i, 0)),
                  col, sq, col, sq, row, row],
        out_specs=pl.BlockSpec((1, tm, D_MODEL), lambda b, i: (b, i, 0)),
        out_shape=jax.ShapeDtypeStruct((B, L, D_MODEL), F32),
        compiler_params=_params("parallel", "parallel"),
        name="s5_out",
    )(ysT, ubT, sgT, x, w['d'], w['wgluT'], w['bglu'], w['wo'], ln_g, ln_b)


def _prep_mla(w_in, g_q, w_q_up, g_kv, w_kv_up, w_out):
    nkv = Q_LORA + KV_LORA
    wa = jnp.concatenate([w_in[:, :nkv + QK_ROPE_DIM],
                          jnp.zeros((D_MODEL, LANES - QK_ROPE_DIM), F32)], axis=1)
    wq = w_q_up.reshape(Q_LORA, N_HEADS, QK_NOPE_DIM + QK_ROPE_DIM)
    wq = jnp.pad(wq, ((0, 0), (0, 0), (0, HEAD_PAD - QK_NOPE_DIM - QK_ROPE_DIM)))
    wkv = w_kv_up.reshape(KV_LORA, N_HEADS, QK_NOPE_DIM + V_DIM)
    return dict(
        wa=wa.astype(BF16),
        gq=g_q.reshape(1, Q_LORA), gkv=g_kv.reshape(1, KV_LORA),
        wqT=wq.reshape(Q_LORA, N_HEADS * HEAD_PAD).T.astype(BF16),
        wkn=wkv[:, :, :QK_NOPE_DIM].reshape(KV_LORA, N_HEADS * QK_NOPE_DIM).astype(BF16),
        wvT=wkv[:, :, QK_NOPE_DIM:].reshape(KV_LORA, N_HEADS * V_DIM).T.astype(BF16),
        wgT=w_in[:, nkv + QK_ROPE_DIM:].T.astype(BF16),
        wo=w_out.astype(BF16),
    )


def _rope_tables(L):
    inv = ROPE_THETA ** (-jnp.arange(0, QK_ROPE_DIM, 2, dtype=F32) / QK_ROPE_DIM)
    ang = jnp.arange(L, dtype=F32)[:, None] * inv[None, :]
    cos, sin = jnp.cos(ang), jnp.sin(ang)
    z = jnp.zeros_like(cos)
    z2 = jnp.zeros((L, LANES - QK_ROPE_DIM), F32)
    return dict(
        cosT=cos.T, sinT=sin.T,
        cosk=jnp.concatenate([cos, cos, z2], axis=1),
        sina=jnp.concatenate([z, sin, z2], axis=1),
        sinb=jnp.concatenate([-sin, z, z2], axis=1),
    )


def _powers(lam_bar, n):
    pw = jnp.stack([jnp.ones_like(lam_bar), lam_bar], axis=-2)
    top = lam_bar
    while pw.shape[-2] - 1 < n:
        pw = jnp.concatenate([pw, pw[..., 1:, :] * top[..., None, :]], axis=-2)
        top = top * top
    return pw


def _reim(z, axis):
    return jnp.concatenate([jnp.real(z), jnp.imag(z)], axis=axis)


def _prep_s5(w_in, a_re, a_im, log_step, b_re, b_im, c_re, c_im, d, w_glu, b_glu, w_out):
    G, N, P, T = S5_GROUPS, S5_STATE, S5_GROUP, CHUNK
    lam = lax.complex(a_re, a_im)
    step = jnp.exp(log_step)[..., None]
    lam_bar = jnp.exp(lam * step)
    b_bar = ((lam_bar - 1.0) / lam)[..., None] * lax.complex(b_re, b_im)
    c = lax.complex(c_re, c_im)
    pw = _powers(lam_bar, T)

    cb = c[:, :, None, :, :] * jnp.swapaxes(b_bar, -1, -2)[:, :, :, None, :]
    cb = cb.reshape(2, G, P * P, N)
    cbT = jnp.concatenate([_reim(cb[0], -1), _reim(cb[1], -1)], axis=-1)
    pf = jnp.swapaxes(pw[0, :, :T], -1, -2)
    pb = jnp.swapaxes(pw[1, :, :T], -1, -2)[..., ::-1]
    zf = jnp.zeros((G, N, T), pf.dtype)
    pf = jnp.concatenate([zf, pf], axis=-1)
    pb = jnp.concatenate([zf[..., :1], pb, zf[..., :T - 1]], axis=-1)
    pwT = jnp.concatenate([jnp.real(pf), -jnp.imag(pf), jnp.real(pb), -jnp.imag(pb)], axis=1)

    sf = pw[0, :, T - 1::-1][:, :T]
    sb = pw[1, :, :T]
    rs_f = jnp.swapaxes(b_bar[0], -1, -2)[:, :, None, :] * sf[:, None, :, :]
    rs_b = jnp.swapaxes(b_bar[1], -1, -2)[:, :, None, :] * sb[:, None, :, :]
    rs = jnp.concatenate([jnp.real(rs_f), jnp.real(rs_b), jnp.imag(rs_f), jnp.imag(rs_b)], axis=-1)
    rs = rs.reshape(G, P * T, 4 * N)

    of = pw[0, :, 1:T + 1]
    ob = pw[1, :, T:0:-1]
    ri_f = c[0][:, :, None, :] * of[:, None, :, :]
    ri_b = c[1][:, :, None, :] * ob[:, None, :, :]
    ri = jnp.concatenate([jnp.real(ri_f), jnp.real(ri_b), -jnp.imag(ri_f), -jnp.imag(ri_b)], axis=-1)
    ri = jnp.swapaxes(ri.reshape(G, P * T, 4 * N), -1, -2)

    a = pw[:, :, T]
    rows = [jnp.concatenate([jnp.real(a[0]), jnp.real(a[1])], -1),
            jnp.concatenate([jnp.imag(a[0]), jnp.imag(a[1])], -1)]
    ac = jnp.stack(rows + [jnp.zeros_like(rows[0])] * 6, axis=1)

    return dict(
        wuT=w_in[:, :D_MODEL].T.astype(BF16), wgT=w_in[:, D_MODEL:].T.astype(BF16),
        cbT=cbT, pwT=pwT, rs=rs.astype(BF16), ri=ri.astype(BF16), ac=ac,
        d=jnp.broadcast_to(d[:, None], (D_MODEL, LANES)),
        wgluT=w_glu.T.astype(BF16),
        bglu=jnp.broadcast_to(b_glu[:, None], (D_MODEL, LANES)),
        wo=w_out.astype(BF16),
    )


def _mla_layer(x, w, tabs, ln_g, ln_b, tm):
    qT, k, vT, sgT = _mla_proj(x, w, tabs, tm)
    ogT = _attention(qT, k, vT, sgT, tm)
    return _mla_out(ogT, x, w['wo'], ln_g, ln_b, tm)


def _s5_layer(x, w, toep, ln_g, ln_b, tm):
    B, L, _ = x.shape
    ubT, sgT = _s5_in(x, w['wuT'], w['wgT'], tm)
    ysT = _s5_scan(ubT, toep, w['rs'], w['ri'], w['ac'], B, L // CHUNK)
    return _s5_out(ysT, ubT, sgT, x, w, ln_g, ln_b, tm)


def _trunk(x, mla_w, s5_w, toeps, ln_g, ln_b):
    L = x.shape[1]
    tm = min(TOK_TILE, L)
    tabs = _rope_tables(L)
    for i in range(DEPTH):
        g, b = ln_g[i].reshape(1, D_MODEL), ln_b[i].reshape(1, D_MODEL)
        if i % 2 == 0:
            x = _mla_layer(x, mla_w[i // 2], tabs, g, b, tm)
        else:
            x = _s5_layer(x, s5_w[i // 2], toeps[i // 2], g, b, tm)
    return x


def kernel(x_prompt, x_sample, mla_w_in, mla_g_q, mla_w_q_up, mla_g_kv, mla_w_kv_up, mla_w_out,
           s5_w_in, s5_a_re, s5_a_im, s5_log_step, s5_b_re, s5_b_im, s5_c_re, s5_c_im, s5_d,
           s5_w_glu, s5_b_glu, s5_w_out, ln_g, ln_b):
    mla = (mla_w_in, mla_g_q, mla_w_q_up, mla_g_kv, mla_w_kv_up, mla_w_out)
    s5 = (s5_w_in, s5_a_re, s5_a_im, s5_log_step, s5_b_re, s5_b_im, s5_c_re, s5_c_im, s5_d,
          s5_w_glu, s5_b_glu, s5_w_out)
    mla_w = [_prep_mla(*[w[j] for w in mla]) for j in range(mla_w_in.shape[0])]
    s5_w = [_prep_s5(*[w[j] for w in s5]) for j in range(s5_w_in.shape[0])]
    toeps = [_toeplitz(w['cbT'], w['pwT']) for w in s5_w]
    y_prompt = _trunk(x_prompt, mla_w, s5_w, toeps, ln_g, ln_b)
    y_sample = _trunk(x_sample, mla_w, s5_w, toeps, ln_g, ln_b)
    return (y_prompt, y_sample)
```

```python
import functools
import math

import jax
import jax.numpy as jnp
from jax import lax
from jax.experimental import pallas as pl
from jax.experimental.pallas import tpu as pltpu

F32 = jnp.float32
BF16 = jnp.bfloat16

D_MODEL = 1024
DEPTH = 4
N_HEADS = 8
QK_NOPE_DIM = 128
QK_ROPE_DIM = 64
V_DIM = 128
Q_LORA = 384
KV_LORA = 256
ROPE_THETA = 10000.0
ATTN_SCALE = 1.0 / math.sqrt(QK_NOPE_DIM + QK_ROPE_DIM)
S5_GROUP = 16
S5_GROUPS = D_MODEL // S5_GROUP
S5_STATE = 64
ALPHA = (2 * DEPTH) ** 0.25
LN_EPS = 1e-5
RMS_EPS = 1e-6

LANES = 128
HEAD_PAD = 256
ROPE_HALF = QK_ROPE_DIM // 2
V_PAD = V_DIM + 16
LOG2_E = math.log2(math.e)
CHUNK = 128
TOK_TILE = 512
ATTN_Q_TILES = 2
VMEM_LIMIT = 56 * 1024 * 1024

NT_DIMS = (((1,), (1,)), ((), ()))
TN_DIMS = (((0,), (0,)), ((), ()))


def _dot(a, b):
    return jnp.dot(a, b, preferred_element_type=F32)


def _dot_nt(a, b):
    return lax.dot_general(a, b, NT_DIMS, preferred_element_type=F32)


def _dot_tn(a, b):
    return lax.dot_general(a, b, TN_DIMS, preferred_element_type=F32)


def _params(*sem):
    return pltpu.CompilerParams(dimension_semantics=sem, vmem_limit_bytes=VMEM_LIMIT)


def _layer_norm(r, g, b):
    mu = jnp.mean(r, axis=-1, keepdims=True)
    d = r - mu
    var = jnp.mean(d * d, axis=-1, keepdims=True)
    return d * lax.rsqrt(var + LN_EPS) * g + b


def _rms_norm(x, g):
    return x * lax.rsqrt(jnp.mean(x * x, axis=-1, keepdims=True) + RMS_EPS) * g


def _sigmoid(x):
    return 0.5 * jnp.tanh(0.5 * x) + 0.5


def _silu(x):
    h = 0.5 * x
    return h * jnp.tanh(h) + h


def _mla_proj_kernel(x_ref, wa_ref, gq_ref, gkv_ref, wqT_ref, wkn_ref, wvT_ref, wgT_ref,
                     cosT_ref, sinT_ref, cosk_ref, sina_ref, sinb_ref,
                     qT_ref, k_ref, vT_ref, sgT_ref):
    xb = x_ref[0].astype(BF16)
    ha = _dot(xb, wa_ref[...])
    cq = _rms_norm(ha[:, :Q_LORA], gq_ref[...]).astype(BF16)
    ckv = _rms_norm(ha[:, Q_LORA:Q_LORA + KV_LORA], gkv_ref[...]).astype(BF16)
    kr = ha[:, Q_LORA + KV_LORA:]

    qT = _dot_nt(wqT_ref[...], cq) * (ATTN_SCALE * LOG2_E)
    cosT = cosT_ref[...]
    sinT = sinT_ref[...]
    tm = qT.shape[1]
    zpad = jnp.zeros((HEAD_PAD - QK_NOPE_DIM - QK_ROPE_DIM, tm), F32)
    for h in range(N_HEADS):
        blk = qT[h * HEAD_PAD:(h + 1) * HEAD_PAD]
        t1 = blk[QK_NOPE_DIM:QK_NOPE_DIM + ROPE_HALF]
        t2 = blk[QK_NOPE_DIM + ROPE_HALF:QK_NOPE_DIM + QK_ROPE_DIM]
        out = jnp.concatenate(
            [blk[:QK_NOPE_DIM], t1 * cosT - t2 * sinT, t2 * cosT + t1 * sinT, zpad], axis=0)
        qT_ref[0, 0, h * HEAD_PAD:(h + 1) * HEAD_PAD, :] = out.astype(BF16)

    kn = _dot(ckv, wkn_ref[...])
    krr = (kr * cosk_ref[...]
           + pltpu.roll(kr, ROPE_HALF, 1) * sina_ref[...]
           + pltpu.roll(kr, LANES - ROPE_HALF, 1) * sinb_ref[...]).astype(BF16)
    for h in range(N_HEADS):
        k_ref[0, h, :, :QK_NOPE_DIM] = kn[:, h * QK_NOPE_DIM:(h + 1) * QK_NOPE_DIM].astype(BF16)
        k_ref[0, h, :, QK_NOPE_DIM:] = krr

    vT = _dot_nt(wvT_ref[...], ckv).astype(BF16)
    ones = jnp.ones((V_PAD - V_DIM, tm), BF16)
    for h in range(N_HEADS):
        vT_ref[0, 0, h * V_PAD:h * V_PAD + V_DIM, :] = vT[h * V_DIM:(h + 1) * V_DIM]
        vT_ref[0, 0, h * V_PAD + V_DIM:(h + 1) * V_PAD, :] = ones
    sgT_ref[0, 0] = _silu(_dot_nt(wgT_ref[...], xb)).astype(BF16)


def _mla_proj(x, w, tabs, tm):
    B, L, _ = x.shape
    nl = L // tm
    const = lambda shape: pl.BlockSpec(shape, lambda b, i: (0,) * len(shape))
    return pl.pallas_call(
        _mla_proj_kernel,
        grid=(B, nl),
        in_specs=[
            pl.BlockSpec((1, tm, D_MODEL), lambda b, i: (b, i, 0)),
            const(w['wa'].shape), const(w['gq'].shape), const(w['gkv'].shape),
            const(w['wqT'].shape), const(w['wkn'].shape), const(w['wvT'].shape), const(w['wgT'].shape),
            pl.BlockSpec((ROPE_HALF, tm), lambda b, i: (0, i)),
            pl.BlockSpec((ROPE_HALF, tm), lambda b, i: (0, i)),
            pl.BlockSpec((tm, LANES), lambda b, i: (i, 0)),
            pl.BlockSpec((tm, LANES), lambda b, i: (i, 0)),
            pl.BlockSpec((tm, LANES), lambda b, i: (i, 0)),
        ],
        out_specs=[
            pl.BlockSpec((1, 1, N_HEADS * HEAD_PAD, tm), lambda b, i: (b, i, 0, 0)),
            pl.BlockSpec((1, N_HEADS, tm, HEAD_PAD), lambda b, i: (b, 0, i, 0)),
            pl.BlockSpec((1, 1, N_HEADS * V_PAD, tm), lambda b, i: (b, i, 0, 0)),
            pl.BlockSpec((1, 1, N_HEADS * V_DIM, tm), lambda b, i: (b, i, 0, 0)),
        ],
        out_shape=[
            jax.ShapeDtypeStruct((B, nl, N_HEADS * HEAD_PAD, tm), BF16),
            jax.ShapeDtypeStruct((B, N_HEADS, L, HEAD_PAD), BF16),
            jax.ShapeDtypeStruct((B, nl, N_HEADS * V_PAD, tm), BF16),
            jax.ShapeDtypeStruct((B, nl, N_HEADS * V_DIM, tm), BF16),
        ],
        compiler_params=_params("parallel", "parallel"),
        name="mla_proj",
    )(x, w['wa'], w['gq'], w['gkv'], w['wqT'], w['wkn'], w['wvT'], w['wgT'],
      tabs['cosT'], tabs['sinT'], tabs['cosk'], tabs['sina'], tabs['sinb'])


def _attn_kernel(qT_ref, k_ref, vT_ref, sgT_ref, o_ref, m_sc, acc_sc, sa_sc, sb_sc, ma_sc, mb_sc, *, tk, nk, nq):
    tq = qT_ref.shape[3]
    qT = jnp.concatenate([qT_ref[0, t] for t in range(nq)], axis=1)
    m_sc[...] = jnp.full(m_sc.shape, -jnp.inf, F32)
    acc_sc[...] = jnp.zeros(acc_sc.shape, F32)

    def scores(j, s_ref, mx_ref):
        ks = k_ref[0, 0, pl.ds(pl.multiple_of(j * tk, tk), tk), :]
        sT = _dot(ks, qT)
        s_ref[...] = sT
        mx_ref[...] = jnp.max(sT, axis=0, keepdims=True)

    def update(j, s_ref, mx_ref):
        m_prev = m_sc[...]
        m_new = jnp.maximum(m_prev, mx_ref[...])
        alpha = jnp.exp2(m_prev - m_new)
        p = jnp.exp2((s_ref[...] - m_new).astype(BF16))
        acc_sc[...] = acc_sc[...] * alpha + _dot(vT_ref[0, j], p)
        m_sc[...] = m_new

    scores(0, sa_sc, ma_sc)

    def body(jj, carry):
        j = 2 * jj
        scores(j + 1, sb_sc, mb_sc)
        update(j, sa_sc, ma_sc)
        scores(j + 2, sa_sc, ma_sc)
        update(j + 1, sb_sc, mb_sc)
        return carry

    lax.fori_loop(0, nk // 2 - 1, body, 0)
    scores(nk - 1, sb_sc, mb_sc)
    update(nk - 2, sa_sc, ma_sc)
    update(nk - 1, sb_sc, mb_sc)
    for t in range(nq):
        cols = slice(t * tq, (t + 1) * tq)
        o = acc_sc[:V_DIM, cols] / acc_sc[V_DIM:V_DIM + 1, cols] * sgT_ref[0, t].astype(F32)
        o_ref[0, t] = o.astype(BF16)


def _attention(qT, k, vT, sgT, tm):
    B, nl, _, _ = qT.shape
    L = nl * tm
    nq = ATTN_Q_TILES
    kern = functools.partial(_attn_kernel, tk=tm, nk=nl, nq=nq)
    return pl.pallas_call(
        kern,
        grid=(B, N_HEADS, nl // nq),
        in_specs=[
            pl.BlockSpec((1, nq, HEAD_PAD, tm), lambda b, h, i: (b, i, h, 0)),
            pl.BlockSpec((1, 1, L, HEAD_PAD), lambda b, h, i: (b, h, 0, 0)),
            pl.BlockSpec((1, nl, V_PAD, tm), lambda b, h, i: (b, 0, h, 0)),
            pl.BlockSpec((1, nq, V_DIM, tm), lambda b, h, i: (b, i, h, 0)),
        ],
        out_specs=pl.BlockSpec((1, nq, V_DIM, tm), lambda b, h, i: (b, i, h, 0)),
        out_shape=jax.ShapeDtypeStruct((B, nl, N_HEADS * V_DIM, tm), BF16),
        scratch_shapes=[pltpu.VMEM((1, nq * tm), F32), pltpu.VMEM((V_PAD, nq * tm), F32),
                        pltpu.VMEM((tm, nq * tm), F32), pltpu.VMEM((tm, nq * tm), F32),
                        pltpu.VMEM((1, nq * tm), F32), pltpu.VMEM((1, nq * tm), F32)],
        compiler_params=_params("parallel", "parallel", "arbitrary"),
        name="mla_attn",
    )(qT, k, vT, sgT)


def _out_kernel(zT_ref, x_ref, wo_ref, g_ref, b_ref, o_ref):
    y = _dot_tn(zT_ref[0, 0], wo_ref[...])
    r = ALPHA * x_ref[0] + y
    o_ref[0] = _layer_norm(r, g_ref[...], b_ref[...])


def _mla_out(ogT, x, woT, ln_g, ln_b, tm):
    B, L, _ = x.shape
    nl = L // tm
    return pl.pallas_call(
        _out_kernel,
        grid=(B, nl),
        in_specs=[
            pl.BlockSpec((1, 1, D_MODEL, tm), lambda b, i: (b, i, 0, 0)),
            pl.BlockSpec((1, tm, D_MODEL), lambda b, i: (b, i, 0)),
            pl.BlockSpec((D_MODEL, D_MODEL), lambda b, i: (0, 0)),
            pl.BlockSpec((1, D_MODEL), lambda b, i: (0, 0)),
            pl.BlockSpec((1, D_MODEL), lambda b, i: (0, 0)),
        ],
        out_specs=pl.BlockSpec((1, tm, D_MODEL), lambda b, i: (b, i, 0)),
        out_shape=jax.ShapeDtypeStruct((B, L, D_MODEL), F32),
        compiler_params=_params("parallel", "parallel"),
        name="mla_out",
    )(ogT, x, woT, ln_g, ln_b)


def _s5_in_kernel(x_ref, wuT_ref, wgT_ref, uT_ref, sgT_ref):
    xb = x_ref[0].astype(BF16)
    uT_ref[...] = _dot_nt(wuT_ref[...], xb).astype(BF16)
    sgT_ref[...] = _silu(_dot_nt(wgT_ref[...], xb)).astype(BF16)


def _s5_in(x, wuT, wgT, tm):
    B, L, _ = x.shape
    nl = L // tm
    tok = pl.BlockSpec((D_MODEL, tm), lambda b, i: (0, b * nl + i))
    wspec = pl.BlockSpec((D_MODEL, D_MODEL), lambda b, i: (0, 0))
    return pl.pallas_call(
        _s5_in_kernel,
        grid=(B, nl),
        in_specs=[pl.BlockSpec((1, tm, D_MODEL), lambda b, i: (b, i, 0)), wspec, wspec],
        out_specs=[tok, tok],
        out_shape=[jax.ShapeDtypeStruct((D_MODEL, B * L), BF16)] * 2,
        compiler_params=_params("parallel", "parallel"),
        name="s5_in",
    )(x, wuT, wgT)


def _toep_kernel(cbT_ref, pwT_ref, o_ref, kv_sc):
    kv_sc[...] = jnp.dot(cbT_ref[0], pwT_ref[0], preferred_element_type=F32,
                         precision=lax.Precision.HIGHEST)

    def body(pi, carry):
        row0 = pl.multiple_of(pi * CHUNK, CHUNK)
        for po in range(S5_GROUP):
            kv = kv_sc[pl.ds(pi * S5_GROUP + po, 1), :]
            rolled = pltpu.roll(jnp.broadcast_to(kv, (CHUNK, 2 * CHUNK)), 0, 1, stride=1, stride_axis=0)
            o_ref[0, pl.ds(row0, CHUNK), po * CHUNK:(po + 1) * CHUNK] = rolled[:, CHUNK:].astype(BF16)
        return carry

    lax.fori_loop(0, S5_GROUP, body, 0)


def _toeplitz(cbT, pwT):
    G = cbT.shape[0]
    n = S5_GROUP * CHUNK
    blk = pl.BlockSpec((1, 2 * CHUNK, 2 * CHUNK), lambda g: (g, 0, 0))
    return pl.pallas_call(
        _toep_kernel,
        grid=(G,),
        in_specs=[blk, blk],
        out_specs=pl.BlockSpec((1, n, n), lambda g: (g, 0, 0)),
        out_shape=jax.ShapeDtypeStruct((G, n, n), BF16),
        scratch_shapes=[pltpu.VMEM((2 * CHUNK, 2 * CHUNK), F32)],
        compiler_params=_params("parallel"),
        name="s5_toeplitz",
    )(cbT, pwT)


def _scan_kernel(u_ref, tm_ref, rs_ref, ri_ref, ac_ref, y_ref,
                 sre_sc, sim_sc, hfre_sc, hfim_sc, hbre_sc, hbim_sc, *, nseq, nchunk):
    lhs = jnp.concatenate([u_ref[p] for p in range(S5_GROUP)], axis=1)
    s = _dot(lhs, rs_ref[0])
    y = _dot(lhs, tm_ref[0])
    tile = 2 * S5_STATE
    sre_sc[...] = s[:, :tile]
    sim_sc[...] = s[:, tile:]
    ac = ac_ref[0]
    are, aim = ac[0:1], ac[1:2]
    is_fwd = lax.broadcasted_iota(jnp.int32, (nseq, tile), 1) < S5_STATE

    ere = jnp.zeros((nseq, tile), F32)
    eim = jnp.zeros((nseq, tile), F32)
    for c in range(nchunk):
        rf = pl.ds(c * nseq, nseq)
        rb = pl.ds((nchunk - 1 - c) * nseq, nseq)
        hfre_sc[rf, :] = ere
        hfim_sc[rf, :] = eim
        hbre_sc[rb, :] = ere
        hbim_sc[rb, :] = eim
        s_re = jnp.where(is_fwd, sre_sc[rf, :], sre_sc[rb, :])
        s_im = jnp.where(is_fwd, sim_sc[rf, :], sim_sc[rb, :])
        ere, eim = are * ere - aim * eim + s_re, are * eim + aim * ere + s_im

    fwd_rows = lax.broadcasted_iota(jnp.int32, hfre_sc.shape, 1) < S5_STATE
    h = jnp.concatenate([jnp.where(fwd_rows, hfre_sc[...], hbre_sc[...]),
                         jnp.where(fwd_rows, hfim_sc[...], hbim_sc[...])], axis=1).astype(BF16)
    y = y + _dot(h, ri_ref[0])
    for p in range(S5_GROUP):
        y_ref[p] = y[:, p * CHUNK:(p + 1) * CHUNK].astype(BF16)


def _s5_scan(ubT, toep, rs, ri, ac, nseq, nchunk):
    W, T = ubT.shape
    R = T // CHUNK
    u3 = ubT.reshape(W, nseq, nchunk, CHUNK).swapaxes(1, 2).reshape(W, R, CHUNK)
    n = S5_GROUP * CHUNK
    kern = functools.partial(_scan_kernel, nseq=nseq, nchunk=nchunk)
    y3 = pl.pallas_call(
        kern,
        grid=(S5_GROUPS,),
        in_specs=[
            pl.BlockSpec((S5_GROUP, R, CHUNK), lambda g: (g, 0, 0)),
            pl.BlockSpec((1, n, n), lambda g: (g, 0, 0)),
            pl.BlockSpec((1, n, 4 * S5_STATE), lambda g: (g, 0, 0)),
            pl.BlockSpec((1, 4 * S5_STATE, n), lambda g: (g, 0, 0)),
            pl.BlockSpec((1, 8, 2 * S5_STATE), lambda g: (g, 0, 0)),
        ],
        out_specs=pl.BlockSpec((S5_GROUP, R, CHUNK), lambda g: (g, 0, 0)),
        out_shape=jax.ShapeDtypeStruct((W, R, CHUNK), BF16),
        scratch_shapes=[pltpu.VMEM((R, 2 * S5_STATE), F32)] * 6,
        compiler_params=_params("parallel"),
        name="s5_scan",
    )(u3, toep, rs, ri, ac)
    return y3.reshape(W, nchunk, nseq, CHUNK).swapaxes(1, 2).reshape(W, T)


def _s5_out_kernel(ys_ref, u_ref, sg_ref, x_ref, d_ref, wgluT_ref, bglu_ref, wo_ref, g_ref, b_ref, o_ref):
    tm = ys_ref.shape[1]
    rep = tm // LANES
    d = jnp.tile(d_ref[...], (1, rep))
    bglu = jnp.tile(bglu_ref[...], (1, rep))
    y = ys_ref[...].astype(F32) + d * u_ref[...].astype(F32)
    y = jax.nn.gelu(y)
    z = _dot(wgluT_ref[...], y.astype(BF16)) + bglu
    v = (y * _sigmoid(z) * sg_ref[...].astype(F32)).astype(BF16)
    r = ALPHA * x_ref[0] + _dot_tn(v, wo_ref[...])
    o_ref[0] = _layer_norm(r, g_ref[...], b_ref[...])


def _s5_out(ysT, ubT, sgT, x, w, ln_g, ln_b, tm):
    B, L, _ = x.shape
    nl = L // tm
    tok = pl.BlockSpec((D_MODEL, tm), lambda b, i: (0, b * nl + i))
    sq = pl.BlockSpec((D_MODEL, D_MODEL), lambda b, i: (0, 0))
    col = pl.BlockSpec((D_MODEL, LANES), lambda b, i: (0, 0))
    row = pl.BlockSpec((1, D_MODEL), lambda b, i: (0, 0))
    return pl.pallas_call(
        _s5_out_kernel,
        grid=(B, nl),
        in_specs=[tok, tok, tok, pl.BlockSpec((1, tm, D_MODEL), lambda b, i: (b, i, 0)),
                  col, sq, col, sq, row, row],
        out_specs=pl.BlockSpec((1, tm, D_MODEL), lambda b, i: (b, i, 0)),
        out_shape=jax.ShapeDtypeStruct((B, L, D_MODEL), F32),
        compiler_params=_params("parallel", "parallel"),
        name="s5_out",
    )(ysT, ubT, sgT, x, w['d'], w['wgluT'], w['bglu'], w['wo'], ln_g, ln_b)


def _prep_mla(w_in, g_q, w_q_up, g_kv, w_kv_up, w_out):
    nkv = Q_LORA + KV_LORA
    wa = jnp.concatenate([w_in[:, :nkv + QK_ROPE_DIM],
                          jnp.zeros((D_MODEL, LANES - QK_ROPE_DIM), F32)], axis=1)
    wq = w_q_up.reshape(Q_LORA, N_HEADS, QK_NOPE_DIM + QK_ROPE_DIM)
    wq = jnp.pad(wq, ((0, 0), (0, 0), (0, HEAD_PAD - QK_NOPE_DIM - QK_ROPE_DIM)))
    wkv = w_kv_up.reshape(KV_LORA, N_HEADS, QK_NOPE_DIM + V_DIM)
    return dict(
        wa=wa.astype(BF16),
        gq=g_q.reshape(1, Q_LORA), gkv=g_kv.reshape(1, KV_LORA),
        wqT=wq.reshape(Q_LORA, N_HEADS * HEAD_PAD).T.astype(BF16),
        wkn=wkv[:, :, :QK_NOPE_DIM].reshape(KV_LORA, N_HEADS * QK_NOPE_DIM).astype(BF16),
        wvT=wkv[:, :, QK_NOPE_DIM:].reshape(KV_LORA, N_HEADS * V_DIM).T.astype(BF16),
        wgT=w_in[:, nkv + QK_ROPE_DIM:].T.astype(BF16),
        wo=w_out.astype(BF16),
    )


def _rope_tables(L):
    inv = ROPE_THETA ** (-jnp.arange(0, QK_ROPE_DIM, 2, dtype=F32) / QK_ROPE_DIM)
    ang = jnp.arange(L, dtype=F32)[:, None] * inv[None, :]
    cos, sin = jnp.cos(ang), jnp.sin(ang)
    z = jnp.zeros_like(cos)
    z2 = jnp.zeros((L, LANES - QK_ROPE_DIM), F32)
    return dict(
        cosT=cos.T, sinT=sin.T,
        cosk=jnp.concatenate([cos, cos, z2], axis=1),
        sina=jnp.concatenate([z, sin, z2], axis=1),
        sinb=jnp.concatenate([-sin, z, z2], axis=1),
    )


def _powers(lam_bar, n):
    pw = jnp.stack([jnp.ones_like(lam_bar), lam_bar], axis=-2)
    top = lam_bar
    while pw.shape[-2] - 1 < n:
        pw = jnp.concatenate([pw, pw[..., 1:, :] * top[..., None, :]], axis=-2)
        top = top * top
    return pw


def _reim(z, axis):
    return jnp.concatenate([jnp.real(z), jnp.imag(z)], axis=axis)


def _prep_s5(w_in, a_re, a_im, log_step, b_re, b_im, c_re, c_im, d, w_glu, b_glu, w_out):
    G, N, P, T = S5_GROUPS, S5_STATE, S5_GROUP, CHUNK
    lam = lax.complex(a_re, a_im)
    step = jnp.exp(log_step)[..., None]
    lam_bar = jnp.exp(lam * step)
    b_bar = ((lam_bar - 1.0) / lam)[..., None] * lax.complex(b_re, b_im)
    c = lax.complex(c_re, c_im)
    pw = _powers(lam_bar, T)

    cb = c[:, :, None, :, :] * jnp.swapaxes(b_bar, -1, -2)[:, :, :, None, :]
    cb = cb.reshape(2, G, P * P, N)
    cbT = jnp.concatenate([_reim(cb[0], -1), _reim(cb[1], -1)], axis=-1)
    pf = jnp.swapaxes(pw[0, :, :T], -1, -2)
    pb = jnp.swapaxes(pw[1, :, :T], -1, -2)[..., ::-1]
    zf = jnp.zeros((G, N, T), pf.dtype)
    pf = jnp.concatenate([zf, pf], axis=-1)
    pb = jnp.concatenate([zf[..., :1], pb, zf[..., :T - 1]], axis=-1)
    pwT = jnp.concatenate([jnp.real(pf), -jnp.imag(pf), jnp.real(pb), -jnp.imag(pb)], axis=1)

    def outer(x, y, xs, ys):
        xr, xi, yr, yi = jnp.real(x)[xs], jnp.imag(x)[xs], jnp.real(y)[ys], jnp.imag(y)[ys]
        return xr * yr - xi * yi, xr * yi + xi * yr

    bT = jnp.swapaxes(b_bar, -1, -2)
    sf = pw[0, :, T - 1::-1]
    sb = pw[1, :, :T]
    bcast_p, bcast_t = (slice(None), slice(None), None), (slice(None), None)
    f_re, f_im = outer(bT[0], sf, bcast_p, bcast_t)
    b_re_, b_im_ = outer(bT[1], sb, bcast_p, bcast_t)
    rs = jnp.stack([f_re, b_re_, f_im, b_im_], axis=-2).astype(BF16).reshape(G, P * T, 4 * N)

    cT = jnp.swapaxes(c, -1, -2)
    of = jnp.swapaxes(pw[0, :, 1:T + 1], -1, -2)
    ob = jnp.swapaxes(pw[1, :, T:0:-1], -1, -2)
    bcast_t2, bcast_p2 = (slice(None), slice(None), slice(None), None), (slice(None), slice(None), None)
    f_re, f_im = outer(cT[0], of, bcast_t2, bcast_p2)
    b_re_, b_im_ = outer(cT[1], ob, bcast_t2, bcast_p2)
    ri = jnp.stack([f_re, b_re_, -f_im, -b_im_], axis=1).astype(BF16).reshape(G, 4 * N, P * T)

    a = pw[:, :, T]
    rows = [jnp.concatenate([jnp.real(a[0]), jnp.real(a[1])], -1),
            jnp.concatenate([jnp.imag(a[0]), jnp.imag(a[1])], -1)]
    ac = jnp.stack(rows + [jnp.zeros_like(rows[0])] * 6, axis=1)

    return dict(
        wuT=w_in[:, :D_MODEL].T.astype(BF16), wgT=w_in[:, D_MODEL:].T.astype(BF16),
        cbT=cbT, pwT=pwT, rs=rs, ri=ri, ac=ac,
        d=jnp.broadcast_to(d[:, None], (D_MODEL, LANES)),
        wgluT=w_glu.T.astype(BF16),
        bglu=jnp.broadcast_to(b_glu[:, None], (D_MODEL, LANES)),
        wo=w_out.astype(BF16),
    )


def _mla_layer(x, w, tabs, ln_g, ln_b, tm):
    qT, k, vT, sgT = _mla_proj(x, w, tabs, tm)
    ogT = _attention(qT, k, vT, sgT, tm)
    return _mla_out(ogT, x, w['wo'], ln_g, ln_b, tm)


def _s5_layer(x, w, toep, ln_g, ln_b, tm):
    B, L, _ = x.shape
    ubT, sgT = _s5_in(x, w['wuT'], w['wgT'], tm)
    ysT = _s5_scan(ubT, toep, w['rs'], w['ri'], w['ac'], B, L // CHUNK)
    return _s5_out(ysT, ubT, sgT, x, w, ln_g, ln_b, tm)


def _trunk(x, mla_w, s5_w, toeps, ln_g, ln_b):
    L = x.shape[1]
    tm = min(TOK_TILE, L)
    tabs = _rope_tables(L)
    for i in range(DEPTH):
        g, b = ln_g[i].reshape(1, D_MODEL), ln_b[i].reshape(1, D_MODEL)
        if i % 2 == 0:
            x = _mla_layer(x, mla_w[i // 2], tabs, g, b, tm)
        else:
            x = _s5_layer(x, s5_w[i // 2], toeps[i // 2], g, b, tm)
    return x


def kernel(x_prompt, x_sample, mla_w_in, mla_g_q, mla_w_q_up, mla_g_kv, mla_w_kv_up, mla_w_out,
           s5_w_in, s5_a_re, s5_a_im, s5_log_step, s5_b_re, s5_b_im, s5_c_re, s5_c_im, s5_d,
           s5_w_glu, s5_b_glu, s5_w_out, ln_g, ln_b):
    mla = (mla_w_in, mla_g_q, mla_w_q_up, mla_g_kv, mla_w_kv_up, mla_w_out)
    s5 = (s5_w_in, s5_a_re, s5_a_im, s5_log_step, s5_b_re, s5_b_im, s5_c_re, s5_c_im, s5_d,
          s5_w_glu, s5_b_glu, s5_w_out)
    mla_w = [_prep_mla(*[w[j] for w in mla]) for j in range(mla_w_in.shape[0])]
    s5_w = [_prep_s5(*[w[j] for w in s5]) for j in range(s5_w_in.shape[0])]
    toeps = [_toeplitz(w['cbT'], w['pwT']) for w in s5_w]
    y_prompt = _trunk(x_prompt, mla_w, s5_w, toeps, ln_g, ln_b)
    y_sample = _trunk(x_sample, mla_w, s5_w, toeps, ln_g, ln_b)
    return (y_prompt, y_sample)
```

```python
import functools
import math

import jax
import jax.numpy as jnp
from jax import lax
from jax.experimental import pallas as pl
from jax.experimental.pallas import tpu as pltpu

F32 = jnp.float32
BF16 = jnp.bfloat16

D_MODEL = 1024
DEPTH = 4
N_HEADS = 8
QK_NOPE_DIM = 128
QK_ROPE_DIM = 64
V_DIM = 128
Q_LORA = 384
KV_LORA = 256
ROPE_THETA = 10000.0
ATTN_SCALE = 1.0 / math.sqrt(QK_NOPE_DIM + QK_ROPE_DIM)
S5_GROUP = 16
S5_GROUPS = D_MODEL // S5_GROUP
S5_STATE = 64
ALPHA = (2 * DEPTH) ** 0.25
LN_EPS = 1e-5
RMS_EPS = 1e-6

LANES = 128
HEAD_PAD = 256
ROPE_HALF = QK_ROPE_DIM // 2
V_PAD = V_DIM + 16
LOG2_E = math.log2(math.e)
CHUNK = 128
TOK_TILE = 512
ATTN_Q_TILES = 2
VMEM_LIMIT = 56 * 1024 * 1024

NT_DIMS = (((1,), (1,)), ((), ()))
TN_DIMS = (((0,), (0,)), ((), ()))


def _dot(a, b):
    return jnp.dot(a, b, preferred_element_type=F32)


def _dot_nt(a, b):
    return lax.dot_general(a, b, NT_DIMS, preferred_element_type=F32)


def _dot_tn(a, b):
    return lax.dot_general(a, b, TN_DIMS, preferred_element_type=F32)


def _params(*sem):
    return pltpu.CompilerParams(dimension_semantics=sem, vmem_limit_bytes=VMEM_LIMIT)


def _layer_norm(r, g, b):
    mu = jnp.mean(r, axis=-1, keepdims=True)
    d = r - mu
    var = jnp.mean(d * d, axis=-1, keepdims=True)
    return d * lax.rsqrt(var + LN_EPS) * g + b


def _rms_norm(x, g):
    return x * lax.rsqrt(jnp.mean(x * x, axis=-1, keepdims=True) + RMS_EPS) * g


def _sigmoid(x):
    return 0.5 * jnp.tanh(0.5 * x) + 0.5


def _silu(x):
    h = 0.5 * x
    return h * jnp.tanh(h) + h


def _mla_proj_kernel(x_ref, wa_ref, gq_ref, gkv_ref, wqT_ref, wkn_ref, wvT_ref, wgT_ref,
                     cosT_ref, sinT_ref, cosk_ref, sina_ref, sinb_ref,
                     qT_ref, k_ref, vT_ref, sgT_ref):
    xb = x_ref[0].astype(BF16)
    ha = _dot(xb, wa_ref[...])
    cq = _rms_norm(ha[:, :Q_LORA], gq_ref[...]).astype(BF16)
    ckv = _rms_norm(ha[:, Q_LORA:Q_LORA + KV_LORA], gkv_ref[...]).astype(BF16)
    kr = ha[:, Q_LORA + KV_LORA:]

    qT = _dot_nt(wqT_ref[...], cq) * (ATTN_SCALE * LOG2_E)
    cosT = cosT_ref[...]
    sinT = sinT_ref[...]
    tm = qT.shape[1]
    zpad = jnp.zeros((HEAD_PAD - QK_NOPE_DIM - QK_ROPE_DIM, tm), F32)
    for h in range(N_HEADS):
        blk = qT[h * HEAD_PAD:(h + 1) * HEAD_PAD]
        t1 = blk[QK_NOPE_DIM:QK_NOPE_DIM + ROPE_HALF]
        t2 = blk[QK_NOPE_DIM + ROPE_HALF:QK_NOPE_DIM + QK_ROPE_DIM]
        out = jnp.concatenate(
            [blk[:QK_NOPE_DIM], t1 * cosT - t2 * sinT, t2 * cosT + t1 * sinT, zpad], axis=0)
        qT_ref[0, 0, h * HEAD_PAD:(h + 1) * HEAD_PAD, :] = out.astype(BF16)

    kn = _dot(ckv, wkn_ref[...])
    krr = (kr * cosk_ref[...]
           + pltpu.roll(kr, ROPE_HALF, 1) * sina_ref[...]
           + pltpu.roll(kr, LANES - ROPE_HALF, 1) * sinb_ref[...]).astype(BF16)
    for h in range(N_HEADS):
        k_ref[0, h, :, :QK_NOPE_DIM] = kn[:, h * QK_NOPE_DIM:(h + 1) * QK_NOPE_DIM].astype(BF16)
        k_ref[0, h, :, QK_NOPE_DIM:] = krr

    vT = _dot_nt(wvT_ref[...], ckv).astype(BF16)
    ones = jnp.ones((V_PAD - V_DIM, tm), BF16)
    for h in range(N_HEADS):
        vT_ref[0, 0, h * V_PAD:h * V_PAD + V_DIM, :] = vT[h * V_DIM:(h + 1) * V_DIM]
        vT_ref[0, 0, h * V_PAD + V_DIM:(h + 1) * V_PAD, :] = ones
    sgT_ref[0, 0] = _silu(_dot_nt(wgT_ref[...], xb)).astype(BF16)


def _mla_proj(x, w, tabs, tm):
    B, L, _ = x.shape
    nl = L // tm
    const = lambda shape: pl.BlockSpec(shape, lambda b, i: (0,) * len(shape))
    return pl.pallas_call(
        _mla_proj_kernel,
        grid=(B, nl),
        in_specs=[
            pl.BlockSpec((1, tm, D_MODEL), lambda b, i: (b, i, 0)),
            const(w['wa'].shape), const(w['gq'].shape), const(w['gkv'].shape),
            const(w['wqT'].shape), const(w['wkn'].shape), const(w['wvT'].shape), const(w['wgT'].shape),
            pl.BlockSpec((ROPE_HALF, tm), lambda b, i: (0, i)),
            pl.BlockSpec((ROPE_HALF, tm), lambda b, i: (0, i)),
            pl.BlockSpec((tm, LANES), lambda b, i: (i, 0)),
            pl.BlockSpec((tm, LANES), lambda b, i: (i, 0)),
            pl.BlockSpec((tm, LANES), lambda b, i: (i, 0)),
        ],
        out_specs=[
            pl.BlockSpec((1, 1, N_HEADS * HEAD_PAD, tm), lambda b, i: (b, i, 0, 0)),
            pl.BlockSpec((1, N_HEADS, tm, HEAD_PAD), lambda b, i: (b, 0, i, 0)),
            pl.BlockSpec((1, 1, N_HEADS * V_PAD, tm), lambda b, i: (b, i, 0, 0)),
            pl.BlockSpec((1, 1, N_HEADS * V_DIM, tm), lambda b, i: (b, i, 0, 0)),
        ],
        out_shape=[
            jax.ShapeDtypeStruct((B, nl, N_HEADS * HEAD_PAD, tm), BF16),
            jax.ShapeDtypeStruct((B, N_HEADS, L, HEAD_PAD), BF16),
            jax.ShapeDtypeStruct((B, nl, N_HEADS * V_PAD, tm), BF16),
            jax.ShapeDtypeStruct((B, nl, N_HEADS * V_DIM, tm), BF16),
        ],
        compiler_params=_params("parallel", "parallel"),
        name="mla_proj",
    )(x, w['wa'], w['gq'], w['gkv'], w['wqT'], w['wkn'], w['wvT'], w['wgT'],
      tabs['cosT'], tabs['sinT'], tabs['cosk'], tabs['sina'], tabs['sinb'])


def _attn_kernel(qT_ref, k_ref, vT_ref, sgT_ref, o_ref, m_sc, acc_sc, sa_sc, sb_sc, ma_sc, mb_sc, *, tk, nk, nq):
    tq = qT_ref.shape[3]
    qT = jnp.concatenate([qT_ref[0, t] for t in range(nq)], axis=1)
    m_sc[...] = jnp.full(m_sc.shape, -jnp.inf, F32)
    acc_sc[...] = jnp.zeros(acc_sc.shape, F32)

    def scores(j, s_ref, mx_ref):
        ks = k_ref[0, 0, pl.ds(pl.multiple_of(j * tk, tk), tk), :]
        sT = _dot(ks, qT)
        s_ref[...] = sT
        mx_ref[...] = jnp.max(sT, axis=0, keepdims=True)

    def update(j, s_ref, mx_ref):
        m_prev = m_sc[...]
        m_new = jnp.maximum(m_prev, mx_ref[...])
        alpha = jnp.exp2(m_prev - m_new)
        p = jnp.exp2((s_ref[...] - m_new).astype(BF16))
        acc_sc[...] = acc_sc[...] * alpha + _dot(vT_ref[0, j], p)
        m_sc[...] = m_new

    scores(0, sa_sc, ma_sc)

    def body(jj, carry):
        j = 2 * jj
        scores(j + 1, sb_sc, mb_sc)
        update(j, sa_sc, ma_sc)
        scores(j + 2, sa_sc, ma_sc)
        update(j + 1, sb_sc, mb_sc)
        return carry

    lax.fori_loop(0, nk // 2 - 1, body, 0)
    scores(nk - 1, sb_sc, mb_sc)
    update(nk - 2, sa_sc, ma_sc)
    update(nk - 1, sb_sc, mb_sc)
    for t in range(nq):
        cols = slice(t * tq, (t + 1) * tq)
        o = acc_sc[:V_DIM, cols] / acc_sc[V_DIM:V_DIM + 1, cols] * sgT_ref[0, t].astype(F32)
        o_ref[0, t] = o.astype(BF16)


def _attention(qT, k, vT, sgT, tm):
    B, nl, _, _ = qT.shape
    L = nl * tm
    nq = ATTN_Q_TILES
    kern = functools.partial(_attn_kernel, tk=tm, nk=nl, nq=nq)
    return pl.pallas_call(
        kern,
        grid=(B, N_HEADS, nl // nq),
        in_specs=[
            pl.BlockSpec((1, nq, HEAD_PAD, tm), lambda b, h, i: (b, i, h, 0)),
            pl.BlockSpec((1, 1, L, HEAD_PAD), lambda b, h, i: (b, h, 0, 0)),
            pl.BlockSpec((1, nl, V_PAD, tm), lambda b, h, i: (b, 0, h, 0)),
            pl.BlockSpec((1, nq, V_DIM, tm), lambda b, h, i: (b, i, h, 0)),
        ],
        out_specs=pl.BlockSpec((1, nq, V_DIM, tm), lambda b, h, i: (b, i, h, 0)),
        out_shape=jax.ShapeDtypeStruct((B, nl, N_HEADS * V_DIM, tm), BF16),
        scratch_shapes=[pltpu.VMEM((1, nq * tm), F32), pltpu.VMEM((V_PAD, nq * tm), F32),
                        pltpu.VMEM((tm, nq * tm), F32), pltpu.VMEM((tm, nq * tm), F32),
                        pltpu.VMEM((1, nq * tm), F32), pltpu.VMEM((1, nq * tm), F32)],
        compiler_params=_params("parallel", "parallel", "arbitrary"),
        name="mla_attn",
    )(qT, k, vT, sgT)


def _out_kernel(zT_ref, x_ref, wo_ref, g_ref, b_ref, o_ref):
    y = _dot_tn(zT_ref[0, 0], wo_ref[...])
    r = ALPHA * x_ref[0] + y
    o_ref[0] = _layer_norm(r, g_ref[...], b_ref[...])


def _mla_out(ogT, x, woT, ln_g, ln_b, tm):
    B, L, _ = x.shape
    nl = L // tm
    return pl.pallas_call(
        _out_kernel,
        grid=(B, nl),
        in_specs=[
            pl.BlockSpec((1, 1, D_MODEL, tm), lambda b, i: (b, i, 0, 0)),
            pl.BlockSpec((1, tm, D_MODEL), lambda b, i: (b, i, 0)),
            pl.BlockSpec((D_MODEL, D_MODEL), lambda b, i: (0, 0)),
            pl.BlockSpec((1, D_MODEL), lambda b, i: (0, 0)),
            pl.BlockSpec((1, D_MODEL), lambda b, i: (0, 0)),
        ],
        out_specs=pl.BlockSpec((1, tm, D_MODEL), lambda b, i: (b, i, 0)),
        out_shape=jax.ShapeDtypeStruct((B, L, D_MODEL), F32),
        compiler_params=_params("parallel", "parallel"),
        name="mla_out",
    )(ogT, x, woT, ln_g, ln_b)


def _s5_in_kernel(x_ref, wuT_ref, wgT_ref, u_ref, sgT_ref):
    xb = x_ref[0].astype(BF16)
    uT = _dot_nt(wuT_ref[...], xb)
    for c in range(u_ref.shape[0]):
        u_ref[c, 0] = uT[:, c * CHUNK:(c + 1) * CHUNK]
    sgT_ref[...] = _silu(_dot_nt(wgT_ref[...], xb)).astype(BF16)


def _chunk_spec(tm):
    return pl.BlockSpec((tm // CHUNK, 1, D_MODEL, CHUNK), lambda b, i: (i, b, 0, 0))


def _s5_in(x, wuT, wgT, tm):
    B, L, _ = x.shape
    nl = L // tm
    tok = pl.BlockSpec((D_MODEL, tm), lambda b, i: (0, b * nl + i))
    wspec = pl.BlockSpec((D_MODEL, D_MODEL), lambda b, i: (0, 0))
    return pl.pallas_call(
        _s5_in_kernel,
        grid=(B, nl),
        in_specs=[pl.BlockSpec((1, tm, D_MODEL), lambda b, i: (b, i, 0)), wspec, wspec],
        out_specs=[_chunk_spec(tm), tok],
        out_shape=[jax.ShapeDtypeStruct((L // CHUNK, B, D_MODEL, CHUNK), F32),
                   jax.ShapeDtypeStruct((D_MODEL, B * L), BF16)],
        compiler_params=_params("parallel", "parallel"),
        name="s5_in",
    )(x, wuT, wgT)


def _toep_kernel(cbT_ref, pwT_ref, o_ref, kv_sc):
    kv_sc[...] = jnp.dot(cbT_ref[0], pwT_ref[0], preferred_element_type=F32,
                         precision=lax.Precision.HIGHEST)

    def body(pi, carry):
        row0 = pl.multiple_of(pi * CHUNK, CHUNK)
        for po in range(S5_GROUP):
            kv = kv_sc[pl.ds(pi * S5_GROUP + po, 1), :]
            rolled = pltpu.roll(jnp.broadcast_to(kv, (CHUNK, 2 * CHUNK)), 0, 1, stride=1, stride_axis=0)
            o_ref[0, pl.ds(row0, CHUNK), po * CHUNK:(po + 1) * CHUNK] = rolled[:, CHUNK:].astype(BF16)
        return carry

    lax.fori_loop(0, S5_GROUP, body, 0)


def _toeplitz(cbT, pwT):
    G = cbT.shape[0]
    n = S5_GROUP * CHUNK
    blk = pl.BlockSpec((1, 2 * CHUNK, 2 * CHUNK), lambda g: (g, 0, 0))
    return pl.pallas_call(
        _toep_kernel,
        grid=(G,),
        in_specs=[blk, blk],
        out_specs=pl.BlockSpec((1, n, n), lambda g: (g, 0, 0)),
        out_shape=jax.ShapeDtypeStruct((G, n, n), BF16),
        scratch_shapes=[pltpu.VMEM((2 * CHUNK, 2 * CHUNK), F32)],
        compiler_params=_params("parallel"),
        name="s5_toeplitz",
    )(cbT, pwT)


def _scan_kernel(u_ref, tm_ref, rs_ref, ri_ref, ac_ref, y_ref,
                 sre_sc, sim_sc, hfre_sc, hfim_sc, hbre_sc, hbim_sc, *, nseq, nchunk):
    lhs = jnp.concatenate([u_ref[:, p, :] for p in range(S5_GROUP)], axis=1).astype(BF16)
    s = _dot(lhs, rs_ref[0])
    y = _dot(lhs, tm_ref[0])
    tile = 2 * S5_STATE
    sre_sc[...] = s[:, :tile]
    sim_sc[...] = s[:, tile:]
    ac = ac_ref[0]
    are, aim = ac[0:1], ac[1:2]
    is_fwd = lax.broadcasted_iota(jnp.int32, (nseq, tile), 1) < S5_STATE

    ere = jnp.zeros((nseq, tile), F32)
    eim = jnp.zeros((nseq, tile), F32)
    for c in range(nchunk):
        rf = pl.ds(c * nseq, nseq)
        rb = pl.ds((nchunk - 1 - c) * nseq, nseq)
        hfre_sc[rf, :] = ere
        hfim_sc[rf, :] = eim
        hbre_sc[rb, :] = ere
        hbim_sc[rb, :] = eim
        s_re = jnp.where(is_fwd, sre_sc[rf, :], sre_sc[rb, :])
        s_im = jnp.where(is_fwd, sim_sc[rf, :], sim_sc[rb, :])
        ere, eim = are * ere - aim * eim + s_re, are * eim + aim * ere + s_im

    fwd_rows = lax.broadcasted_iota(jnp.int32, hfre_sc.shape, 1) < S5_STATE
    h = jnp.concatenate([jnp.where(fwd_rows, hfre_sc[...], hbre_sc[...]),
                         jnp.where(fwd_rows, hfim_sc[...], hbim_sc[...])], axis=1).astype(BF16)
    y = y + _dot(h, ri_ref[0])
    for p in range(S5_GROUP):
        y_ref[:, p, :] = y[:, p * CHUNK:(p + 1) * CHUNK]


def _s5_scan(u4, toep, rs, ri, ac):
    nchunk, nseq, W, _ = u4.shape
    R = nchunk * nseq
    n = S5_GROUP * CHUNK
    kern = functools.partial(_scan_kernel, nseq=nseq, nchunk=nchunk)
    y3 = pl.pallas_call(
        kern,
        grid=(S5_GROUPS,),
        in_specs=[
            pl.BlockSpec((R, S5_GROUP, CHUNK), lambda g: (0, g, 0)),
            pl.BlockSpec((1, n, n), lambda g: (g, 0, 0)),
            pl.BlockSpec((1, n, 4 * S5_STATE), lambda g: (g, 0, 0)),
            pl.BlockSpec((1, 4 * S5_STATE, n), lambda g: (g, 0, 0)),
            pl.BlockSpec((1, 8, 2 * S5_STATE), lambda g: (g, 0, 0)),
        ],
        out_specs=pl.BlockSpec((R, S5_GROUP, CHUNK), lambda g: (0, g, 0)),
        out_shape=jax.ShapeDtypeStruct((R, W, CHUNK), F32),
        scratch_shapes=[pltpu.VMEM((R, 2 * S5_STATE), F32)] * 6,
        compiler_params=_params("parallel"),
        name="s5_scan",
    )(u4.reshape(R, W, CHUNK), toep, rs, ri, ac)
    return y3.reshape(nchunk, nseq, W, CHUNK)


def _s5_out_kernel(ys_ref, u_ref, sg_ref, x_ref, d_ref, wgluT_ref, bglu_ref, wo_ref, g_ref, b_ref, o_ref):
    rep = ys_ref.shape[0]
    d = jnp.tile(d_ref[...], (1, rep))
    bglu = jnp.tile(bglu_ref[...], (1, rep))
    ys = jnp.concatenate([ys_ref[c, 0] for c in range(rep)], axis=1)
    u = jnp.concatenate([u_ref[c, 0] for c in range(rep)], axis=1)
    y = ys + d * u
    y = jax.nn.gelu(y)
    z = _dot(wgluT_ref[...], y.astype(BF16)) + bglu
    v = (y * _sigmoid(z) * sg_ref[...].astype(F32)).astype(BF16)
    r = ALPHA * x_ref[0] + _dot_tn(v, wo_ref[...])
    o_ref[0] = _layer_norm(r, g_ref[...], b_ref[...])


def _s5_out(ys4, u4, sgT, x, w, ln_g, ln_b, tm):
    B, L, _ = x.shape
    nl = L // tm
    tok = pl.BlockSpec((D_MODEL, tm), lambda b, i: (0, b * nl + i))
    sq = pl.BlockSpec((D_MODEL, D_MODEL), lambda b, i: (0, 0))
    col = pl.BlockSpec((D_MODEL, LANES), lambda b, i: (0, 0))
    row = pl.BlockSpec((1, D_MODEL), lambda b, i: (0, 0))
    return pl.pallas_call(
        _s5_out_kernel,
        grid=(B, nl),
        in_specs=[_chunk_spec(tm), _chunk_spec(tm), tok, pl.BlockSpec((1, tm, D_MODEL), lambda b, i: (b, i, 0)),
                  col, sq, col, sq, row, row],
        out_specs=pl.BlockSpec((1, tm, D_MODEL), lambda b, i: (b, i, 0)),
        out_shape=jax.ShapeDtypeStruct((B, L, D_MODEL), F32),
        compiler_params=_params("parallel", "parallel"),
        name="s5_out",
    )(ys4, u4, sgT, x, w['d'], w['wgluT'], w['bglu'], w['wo'], ln_g, ln_b)


def _prep_mla(w_in, g_q, w_q_up, g_kv, w_kv_up, w_out):
    nkv = Q_LORA + KV_LORA
    wa = jnp.concatenate([w_in[:, :nkv + QK_ROPE_DIM],
                          jnp.zeros((D_MODEL, LANES - QK_ROPE_DIM), F32)], axis=1)
    wq = w_q_up.reshape(Q_LORA, N_HEADS, QK_NOPE_DIM + QK_ROPE_DIM)
    wq = jnp.pad(wq, ((0, 0), (0, 0), (0, HEAD_PAD - QK_NOPE_DIM - QK_ROPE_DIM)))
    wkv = w_kv_up.reshape(KV_LORA, N_HEADS, QK_NOPE_DIM + V_DIM)
    return dict(
        wa=wa.astype(BF16),
        gq=g_q.reshape(1, Q_LORA), gkv=g_kv.reshape(1, KV_LORA),
        wqT=wq.reshape(Q_LORA, N_HEADS * HEAD_PAD).T.astype(BF16),
        wkn=wkv[:, :, :QK_NOPE_DIM].reshape(KV_LORA, N_HEADS * QK_NOPE_DIM).astype(BF16),
        wvT=wkv[:, :, QK_NOPE_DIM:].reshape(KV_LORA, N_HEADS * V_DIM).T.astype(BF16),
        wgT=w_in[:, nkv + QK_ROPE_DIM:].T.astype(BF16),
        wo=w_out.astype(BF16),
    )


def _rope_tables(L):
    inv = ROPE_THETA ** (-jnp.arange(0, QK_ROPE_DIM, 2, dtype=F32) / QK_ROPE_DIM)
    ang = jnp.arange(L, dtype=F32)[:, None] * inv[None, :]
    cos, sin = jnp.cos(ang), jnp.sin(ang)
    z = jnp.zeros_like(cos)
    z2 = jnp.zeros((L, LANES - QK_ROPE_DIM), F32)
    return dict(
        cosT=cos.T, sinT=sin.T,
        cosk=jnp.concatenate([cos, cos, z2], axis=1),
        sina=jnp.concatenate([z, sin, z2], axis=1),
        sinb=jnp.concatenate([-sin, z, z2], axis=1),
    )


def _powers(lam_bar, n):
    pw = jnp.stack([jnp.ones_like(lam_bar), lam_bar], axis=-2)
    top = lam_bar
    while pw.shape[-2] - 1 < n:
        pw = jnp.concatenate([pw, pw[..., 1:, :] * top[..., None, :]], axis=-2)
        top = top * top
    return pw


def _reim(z, axis):
    return jnp.concatenate([jnp.real(z), jnp.imag(z)], axis=axis)


def _prep_s5(w_in, a_re, a_im, log_step, b_re, b_im, c_re, c_im, d, w_glu, b_glu, w_out):
    G, N, P, T = S5_GROUPS, S5_STATE, S5_GROUP, CHUNK
    lam = lax.complex(a_re, a_im)
    step = jnp.exp(log_step)[..., None]
    lam_bar = jnp.exp(lam * step)
    b_bar = ((lam_bar - 1.0) / lam)[..., None] * lax.complex(b_re, b_im)
    c = lax.complex(c_re, c_im)
    pw = _powers(lam_bar, T)

    cb = c[:, :, None, :, :] * jnp.swapaxes(b_bar, -1, -2)[:, :, :, None, :]
    cb = cb.reshape(2, G, P * P, N)
    cbT = jnp.concatenate([_reim(cb[0], -1), _reim(cb[1], -1)], axis=-1)
    pf = jnp.swapaxes(pw[0, :, :T], -1, -2)
    pb = jnp.swapaxes(pw[1, :, :T], -1, -2)[..., ::-1]
    zf = jnp.zeros((G, N, T), pf.dtype)
    pf = jnp.concatenate([zf, pf], axis=-1)
    pb = jnp.concatenate([zf[..., :1], pb, zf[..., :T - 1]], axis=-1)
    pwT = jnp.concatenate([jnp.real(pf), -jnp.imag(pf), jnp.real(pb), -jnp.imag(pb)], axis=1)

    sf = pw[0, :, T - 1::-1]
    sb = pw[1, :, :T]
    rs_f = jnp.swapaxes(b_bar[0], -1, -2)[:, :, None, :] * sf[:, None, :, :]
    rs_b = jnp.swapaxes(b_bar[1], -1, -2)[:, :, None, :] * sb[:, None, :, :]
    rs = jnp.concatenate([jnp.real(rs_f), jnp.real(rs_b), jnp.imag(rs_f), jnp.imag(rs_b)], axis=-1)
    rs = rs.reshape(G, P * T, 4 * N).astype(BF16)

    of = pw[0, :, 1:T + 1]
    ob = pw[1, :, T:0:-1]
    ri_f = c[0][:, :, None, :] * of[:, None, :, :]
    ri_b = c[1][:, :, None, :] * ob[:, None, :, :]
    ri = jnp.concatenate([jnp.real(ri_f), jnp.real(ri_b), -jnp.imag(ri_f), -jnp.imag(ri_b)], axis=-1)
    ri = jnp.swapaxes(ri.reshape(G, P * T, 4 * N), -1, -2).astype(BF16)

    a = pw[:, :, T]
    rows = [jnp.concatenate([jnp.real(a[0]), jnp.real(a[1])], -1),
            jnp.concatenate([jnp.imag(a[0]), jnp.imag(a[1])], -1)]
    ac = jnp.stack(rows + [jnp.zeros_like(rows[0])] * 6, axis=1)

    return dict(
        wuT=w_in[:, :D_MODEL].T.astype(BF16), wgT=w_in[:, D_MODEL:].T.astype(BF16),
        cbT=cbT, pwT=pwT, rs=rs, ri=ri, ac=ac,
        d=jnp.broadcast_to(d[:, None], (D_MODEL, LANES)),
        wgluT=w_glu.T.astype(BF16),
        bglu=jnp.broadcast_to(b_glu[:, None], (D_MODEL, LANES)),
        wo=w_out.astype(BF16),
    )


def _mla_layer(x, w, tabs, ln_g, ln_b, tm):
    qT, k, vT, sgT = _mla_proj(x, w, tabs, tm)
    ogT = _attention(qT, k, vT, sgT, tm)
    return _mla_out(ogT, x, w['wo'], ln_g, ln_b, tm)


def _s5_layer(x, w, toep, ln_g, ln_b, tm):
    u4, sgT = _s5_in(x, w['wuT'], w['wgT'], tm)
    ys4 = _s5_scan(u4, toep, w['rs'], w['ri'], w['ac'])
    return _s5_out(ys4, u4, sgT, x, w, ln_g, ln_b, tm)


def _trunk(x, mla_w, s5_w, toeps, ln_g, ln_b):
    L = x.shape[1]
    tm = min(TOK_TILE, L)
    tabs = _rope_tables(L)
    for i in range(DEPTH):
        g, b = ln_g[i].reshape(1, D_MODEL), ln_b[i].reshape(1, D_MODEL)
        if i % 2 == 0:
            x = _mla_layer(x, mla_w[i // 2], tabs, g, b, tm)
        else:
            x = _s5_layer(x, s5_w[i // 2], toeps[i // 2], g, b, tm)
    return x


def kernel(x_prompt, x_sample, mla_w_in, mla_g_q, mla_w_q_up, mla_g_kv, mla_w_kv_up, mla_w_out,
           s5_w_in, s5_a_re, s5_a_im, s5_log_step, s5_b_re, s5_b_im, s5_c_re, s5_c_im, s5_d,
           s5_w_glu, s5_b_glu, s5_w_out, ln_g, ln_b):
    mla = (mla_w_in, mla_g_q, mla_w_q_up, mla_g_kv, mla_w_kv_up, mla_w_out)
    s5 = (s5_w_in, s5_a_re, s5_a_im, s5_log_step, s5_b_re, s5_b_im, s5_c_re, s5_c_im, s5_d,
          s5_w_glu, s5_b_glu, s5_w_out)
    mla_w = [_prep_mla(*[w[j] for w in mla]) for j in range(mla_w_in.shape[0])]
    s5_w = [_prep_s5(*[w[j] for w in s5]) for j in range(s5_w_in.shape[0])]
    toeps = [_toeplitz(w['cbT'], w['pwT']) for w in s5_w]
    y_prompt = _trunk(x_prompt, mla_w, s5_w, toeps, ln_g, ln_b)
    y_sample = _trunk(x_sample, mla_w, s5_w, toeps, ln_g, ln_b)
    return (y_prompt, y_sample)
```

```python
import functools
import math

import jax
import jax.numpy as jnp
from jax import lax
from jax.experimental import pallas as pl
from jax.experimental.pallas import tpu as pltpu

F32 = jnp.float32
BF16 = jnp.bfloat16

D_MODEL = 1024
DEPTH = 4
N_HEADS = 8
QK_NOPE_DIM = 128
QK_ROPE_DIM = 64
V_DIM = 128
Q_LORA = 384
KV_LORA = 256
ROPE_THETA = 10000.0
ATTN_SCALE = 1.0 / math.sqrt(QK_NOPE_DIM + QK_ROPE_DIM)
S5_GROUP = 16
S5_GROUPS = D_MODEL // S5_GROUP
S5_STATE = 64
ALPHA = (2 * DEPTH) ** 0.25
LN_EPS = 1e-5
RMS_EPS = 1e-6

LANES = 128
HEAD_PAD = 256
ROPE_HALF = QK_ROPE_DIM // 2
V_PAD = V_DIM + 16
LOG2_E = math.log2(math.e)
CHUNK = 128
TOK_TILE = 512
ATTN_Q_TILES = 2
ATTN_FAST_MARGIN = 60.0
VMEM_LIMIT = 56 * 1024 * 1024

NT_DIMS = (((1,), (1,)), ((), ()))
TN_DIMS = (((0,), (0,)), ((), ()))


def _dot(a, b):
    return jnp.dot(a, b, preferred_element_type=F32)


def _dot_nt(a, b):
    return lax.dot_general(a, b, NT_DIMS, preferred_element_type=F32)


def _dot_tn(a, b):
    return lax.dot_general(a, b, TN_DIMS, preferred_element_type=F32)


def _params(*sem):
    return pltpu.CompilerParams(dimension_semantics=sem, vmem_limit_bytes=VMEM_LIMIT)


def _layer_norm(r, g, b):
    mu = jnp.mean(r, axis=-1, keepdims=True)
    d = r - mu
    var = jnp.mean(d * d, axis=-1, keepdims=True)
    return d * lax.rsqrt(var + LN_EPS) * g + b


def _rms_norm(x, g):
    return x * lax.rsqrt(jnp.mean(x * x, axis=-1, keepdims=True) + RMS_EPS) * g


def _sigmoid(x):
    return 0.5 * jnp.tanh(0.5 * x) + 0.5


def _silu(x):
    h = 0.5 * x
    return h * jnp.tanh(h) + h


def _mla_proj_kernel(x_ref, wa_ref, gq_ref, gkv_ref, wqT_ref, wkn_ref, wvT_ref, wgT_ref,
                     cosT_ref, sinT_ref, cosk_ref, sina_ref, sinb_ref,
                     qT_ref, k_ref, vT_ref, sgT_ref, kn2_ref):
    xb = x_ref[0].astype(BF16)
    ha = _dot(xb, wa_ref[...])
    cq = _rms_norm(ha[:, :Q_LORA], gq_ref[...]).astype(BF16)
    ckv = _rms_norm(ha[:, Q_LORA:Q_LORA + KV_LORA], gkv_ref[...]).astype(BF16)
    kr = ha[:, Q_LORA + KV_LORA:]

    qT = _dot_nt(wqT_ref[...], cq) * (ATTN_SCALE * LOG2_E)
    cosT = cosT_ref[...]
    sinT = sinT_ref[...]
    tm = qT.shape[1]
    zpad = jnp.zeros((HEAD_PAD - QK_NOPE_DIM - QK_ROPE_DIM, tm), F32)
    for h in range(N_HEADS):
        blk = qT[h * HEAD_PAD:(h + 1) * HEAD_PAD]
        t1 = blk[QK_NOPE_DIM:QK_NOPE_DIM + ROPE_HALF]
        t2 = blk[QK_NOPE_DIM + ROPE_HALF:QK_NOPE_DIM + QK_ROPE_DIM]
        out = jnp.concatenate(
            [blk[:QK_NOPE_DIM], t1 * cosT - t2 * sinT, t2 * cosT + t1 * sinT, zpad], axis=0)
        qT_ref[0, 0, h * HEAD_PAD:(h + 1) * HEAD_PAD, :] = out.astype(BF16)

    kn = _dot(ckv, wkn_ref[...])
    krr = (kr * cosk_ref[...]
           + pltpu.roll(kr, ROPE_HALF, 1) * sina_ref[...]
           + pltpu.roll(kr, LANES - ROPE_HALF, 1) * sinb_ref[...]).astype(BF16)
    krf = krr.astype(F32)
    kr2 = jnp.sum(krf * krf, axis=1, keepdims=True)
    kn2_rows = []
    for h in range(N_HEADS):
        kb = kn[:, h * QK_NOPE_DIM:(h + 1) * QK_NOPE_DIM].astype(BF16)
        k_ref[0, h, :, :QK_NOPE_DIM] = kb
        k_ref[0, h, :, QK_NOPE_DIM:] = krr
        kf = kb.astype(F32)
        n2 = jnp.max(jnp.sum(kf * kf, axis=1, keepdims=True) + kr2, axis=0, keepdims=True)
        kn2_rows.append(jnp.broadcast_to(n2, (1, LANES)))
    kn2_ref[0, 0] = jnp.concatenate(kn2_rows, axis=0)

    vT = _dot_nt(wvT_ref[...], ckv).astype(BF16)
    ones = jnp.ones((V_PAD - V_DIM, tm), BF16)
    for h in range(N_HEADS):
        vT_ref[0, 0, h * V_PAD:h * V_PAD + V_DIM, :] = vT[h * V_DIM:(h + 1) * V_DIM]
        vT_ref[0, 0, h * V_PAD + V_DIM:(h + 1) * V_PAD, :] = ones
    sgT_ref[0, 0] = _silu(_dot_nt(wgT_ref[...], xb)).astype(BF16)


def _mla_proj(x, w, tabs, tm):
    B, L, _ = x.shape
    nl = L // tm
    const = lambda shape: pl.BlockSpec(shape, lambda b, i: (0,) * len(shape))
    return pl.pallas_call(
        _mla_proj_kernel,
        grid=(B, nl),
        in_specs=[
            pl.BlockSpec((1, tm, D_MODEL), lambda b, i: (b, i, 0)),
            const(w['wa'].shape), const(w['gq'].shape), const(w['gkv'].shape),
            const(w['wqT'].shape), const(w['wkn'].shape), const(w['wvT'].shape), const(w['wgT'].shape),
            pl.BlockSpec((ROPE_HALF, tm), lambda b, i: (0, i)),
            pl.BlockSpec((ROPE_HALF, tm), lambda b, i: (0, i)),
            pl.BlockSpec((tm, LANES), lambda b, i: (i, 0)),
            pl.BlockSpec((tm, LANES), lambda b, i: (i, 0)),
            pl.BlockSpec((tm, LANES), lambda b, i: (i, 0)),
        ],
        out_specs=[
            pl.BlockSpec((1, 1, N_HEADS * HEAD_PAD, tm), lambda b, i: (b, i, 0, 0)),
            pl.BlockSpec((1, N_HEADS, tm, HEAD_PAD), lambda b, i: (b, 0, i, 0)),
            pl.BlockSpec((1, 1, N_HEADS * V_PAD, tm), lambda b, i: (b, i, 0, 0)),
            pl.BlockSpec((1, 1, N_HEADS * V_DIM, tm), lambda b, i: (b, i, 0, 0)),
            pl.BlockSpec((1, 1, N_HEADS, LANES), lambda b, i: (b, i, 0, 0)),
        ],
        out_shape=[
            jax.ShapeDtypeStruct((B, nl, N_HEADS * HEAD_PAD, tm), BF16),
            jax.ShapeDtypeStruct((B, N_HEADS, L, HEAD_PAD), BF16),
            jax.ShapeDtypeStruct((B, nl, N_HEADS * V_PAD, tm), BF16),
            jax.ShapeDtypeStruct((B, nl, N_HEADS * V_DIM, tm), BF16),
            jax.ShapeDtypeStruct((B, nl, N_HEADS, LANES), F32),
        ],
        compiler_params=_params("parallel", "parallel"),
        name="mla_proj",
    )(x, w['wa'], w['gq'], w['gkv'], w['wqT'], w['wkn'], w['wvT'], w['wgT'],
      tabs['cosT'], tabs['sinT'], tabs['cosk'], tabs['sina'], tabs['sinb'])


def _attn_kernel(qT_ref, k_ref, vT_ref, sgT_ref, kn2_ref, o_ref, m_sc, acc_sc, *, tk, nk, nq):
    tq = qT_ref.shape[3]
    qTs = [qT_ref[0, t] for t in range(nq)]

    def k_tile(j):
        return k_ref[0, 0, pl.ds(pl.multiple_of(j * tk, tk), tk), :]

    q2 = []
    for t in range(nq):
        cols = slice(t * tq, (t + 1) * tq)
        s0 = _dot(k_tile(0), qTs[t])
        m0 = jnp.max(s0, axis=0, keepdims=True)
        m_sc[:, cols] = m0
        acc_sc[:, cols] = _dot(vT_ref[0, 0], jnp.exp2(s0 - m0).astype(BF16))
        qf = qTs[t].astype(F32)
        q2.append(jnp.sum(qf * qf, axis=0, keepdims=True))

    kn2 = jnp.max(kn2_ref[0, :, pl.ds(pl.program_id(1), 1), :])
    bound = jnp.sqrt(jnp.concatenate(q2, axis=1) * kn2)
    fast_ok = jnp.max(bound - m_sc[...]) <= ATTN_FAST_MARGIN

    @pl.when(fast_ok)
    def _fixed_reference():
        ms = [m_sc[:, t * tq:(t + 1) * tq] for t in range(nq)]

        def body(j, carry):
            ks = k_tile(j)
            ps = [jnp.exp2(_dot(ks, qTs[t]) - ms[t]).astype(BF16) for t in range(nq)]
            for t in range(nq):
                cols = slice(t * tq, (t + 1) * tq)
                acc_sc[:, cols] = acc_sc[:, cols] + _dot(vT_ref[0, j], ps[t])
            return carry

        lax.fori_loop(1, nk, body, 0, unroll=2)

    @pl.when(jnp.logical_not(fast_ok))
    def _online():
        qT = jnp.concatenate(qTs, axis=1)

        def body(j, carry):
            sT = _dot(k_tile(j), qT)
            m_prev = m_sc[...]
            m_new = jnp.maximum(m_prev, jnp.max(sT, axis=0, keepdims=True))
            p = jnp.exp2(sT - m_new).astype(BF16)
            acc_sc[...] = acc_sc[...] * jnp.exp2(m_prev - m_new) + _dot(vT_ref[0, j], p)
            m_sc[...] = m_new
            return carry

        lax.fori_loop(1, nk, body, 0)

    for t in range(nq):
        cols = slice(t * tq, (t + 1) * tq)
        o = acc_sc[:V_DIM, cols] / acc_sc[V_DIM:V_DIM + 1, cols] * sgT_ref[0, t].astype(F32)
        o_ref[0, t] = o.astype(BF16)


def _attention(qT, k, vT, sgT, kn2, tm):
    B, nl, _, _ = qT.shape
    L = nl * tm
    nq = ATTN_Q_TILES
    kern = functools.partial(_attn_kernel, tk=tm, nk=nl, nq=nq)
    return pl.pallas_call(
        kern,
        grid=(B, N_HEADS, nl // nq),
        in_specs=[
            pl.BlockSpec((1, nq, HEAD_PAD, tm), lambda b, h, i: (b, i, h, 0)),
            pl.BlockSpec((1, 1, L, HEAD_PAD), lambda b, h, i: (b, h, 0, 0)),
            pl.BlockSpec((1, nl, V_PAD, tm), lambda b, h, i: (b, 0, h, 0)),
            pl.BlockSpec((1, nq, V_DIM, tm), lambda b, h, i: (b, i, h, 0)),
            pl.BlockSpec((1, nl, N_HEADS, LANES), lambda b, h, i: (b, 0, 0, 0)),
        ],
        out_specs=pl.BlockSpec((1, nq, V_DIM, tm), lambda b, h, i: (b, i, h, 0)),
        out_shape=jax.ShapeDtypeStruct((B, nl, N_HEADS * V_DIM, tm), BF16),
        scratch_shapes=[pltpu.VMEM((1, nq * tm), F32), pltpu.VMEM((V_PAD, nq * tm), F32)],
        compiler_params=_params("parallel", "parallel", "arbitrary"),
        name="mla_attn",
    )(qT, k, vT, sgT, kn2)


def _out_kernel(zT_ref, x_ref, wo_ref, g_ref, b_ref, o_ref):
    y = _dot_tn(zT_ref[0, 0], wo_ref[...])
    r = ALPHA * x_ref[0] + y
    o_ref[0] = _layer_norm(r, g_ref[...], b_ref[...])


def _mla_out(ogT, x, woT, ln_g, ln_b, tm):
    B, L, _ = x.shape
    nl = L // tm
    return pl.pallas_call(
        _out_kernel,
        grid=(B, nl),
        in_specs=[
            pl.BlockSpec((1, 1, D_MODEL, tm), lambda b, i: (b, i, 0, 0)),
            pl.BlockSpec((1, tm, D_MODEL), lambda b, i: (b, i, 0)),
            pl.BlockSpec((D_MODEL, D_MODEL), lambda b, i: (0, 0)),
            pl.BlockSpec((1, D_MODEL), lambda b, i: (0, 0)),
            pl.BlockSpec((1, D_MODEL), lambda b, i: (0, 0)),
        ],
        out_specs=pl.BlockSpec((1, tm, D_MODEL), lambda b, i: (b, i, 0)),
        out_shape=jax.ShapeDtypeStruct((B, L, D_MODEL), F32),
        compiler_params=_params("parallel", "parallel"),
        name="mla_out",
    )(ogT, x, woT, ln_g, ln_b)


def _s5_in_kernel(x_ref, wuT_ref, wgT_ref, u_ref, sgT_ref):
    xb = x_ref[0].astype(BF16)
    uT = _dot_nt(wuT_ref[...], xb)
    for c in range(u_ref.shape[0]):
        u_ref[c, 0] = uT[:, c * CHUNK:(c + 1) * CHUNK]
    sgT_ref[...] = _silu(_dot_nt(wgT_ref[...], xb)).astype(BF16)


def _chunk_spec(tm):
    return pl.BlockSpec((tm // CHUNK, 1, D_MODEL, CHUNK), lambda b, i: (i, b, 0, 0))


def _s5_in(x, wuT, wgT, tm):
    B, L, _ = x.shape
    nl = L // tm
    tok = pl.BlockSpec((D_MODEL, tm), lambda b, i: (0, b * nl + i))
    wspec = pl.BlockSpec((D_MODEL, D_MODEL), lambda b, i: (0, 0))
    return pl.pallas_call(
        _s5_in_kernel,
        grid=(B, nl),
        in_specs=[pl.BlockSpec((1, tm, D_MODEL), lambda b, i: (b, i, 0)), wspec, wspec],
        out_specs=[_chunk_spec(tm), tok],
        out_shape=[jax.ShapeDtypeStruct((L // CHUNK, B, D_MODEL, CHUNK), F32),
                   jax.ShapeDtypeStruct((D_MODEL, B * L), BF16)],
        compiler_params=_params("parallel", "parallel"),
        name="s5_in",
    )(x, wuT, wgT)


def _toep_kernel(cbT_ref, pwT_ref, o_ref, kv_sc):
    kv_sc[...] = jnp.dot(cbT_ref[0], pwT_ref[0], preferred_element_type=F32,
                         precision=lax.Precision.HIGHEST)

    def body(pi, carry):
        row0 = pl.multiple_of(pi * CHUNK, CHUNK)
        for po in range(S5_GROUP):
            kv = kv_sc[pl.ds(pi * S5_GROUP + po, 1), :]
            rolled = pltpu.roll(jnp.broadcast_to(kv, (CHUNK, 2 * CHUNK)), 0, 1, stride=1, stride_axis=0)
            o_ref[0, pl.ds(row0, CHUNK), po * CHUNK:(po + 1) * CHUNK] = rolled[:, CHUNK:].astype(BF16)
        return carry

    lax.fori_loop(0, S5_GROUP, body, 0)


def _toeplitz(cbT, pwT):
    G = cbT.shape[0]
    n = S5_GROUP * CHUNK
    blk = pl.BlockSpec((1, 2 * CHUNK, 2 * CHUNK), lambda g: (g, 0, 0))
    return pl.pallas_call(
        _toep_kernel,
        grid=(G,),
        in_specs=[blk, blk],
        out_specs=pl.BlockSpec((1, n, n), lambda g: (g, 0, 0)),
        out_shape=jax.ShapeDtypeStruct((G, n, n), BF16),
        scratch_shapes=[pltpu.VMEM((2 * CHUNK, 2 * CHUNK), F32)],
        compiler_params=_params("parallel"),
        name="s5_toeplitz",
    )(cbT, pwT)


def _scan_kernel(u_ref, tm_ref, rs_ref, ri_ref, ac_ref, y_ref,
                 sre_sc, sim_sc, hfre_sc, hfim_sc, hbre_sc, hbim_sc, *, nseq, nchunk):
    lhs = jnp.concatenate([u_ref[:, p, :] for p in range(S5_GROUP)], axis=1).astype(BF16)
    s = _dot(lhs, rs_ref[0])
    y = _dot(lhs, tm_ref[0])
    tile = 2 * S5_STATE
    sre_sc[...] = s[:, :tile]
    sim_sc[...] = s[:, tile:]
    ac = ac_ref[0]
    are, aim = ac[0:1], ac[1:2]
    is_fwd = lax.broadcasted_iota(jnp.int32, (nseq, tile), 1) < S5_STATE

    ere = jnp.zeros((nseq, tile), F32)
    eim = jnp.zeros((nseq, tile), F32)
    for c in range(nchunk):
        rf = pl.ds(c * nseq, nseq)
        rb = pl.ds((nchunk - 1 - c) * nseq, nseq)
        hfre_sc[rf, :] = ere
        hfim_sc[rf, :] = eim
        hbre_sc[rb, :] = ere
        hbim_sc[rb, :] = eim
        s_re = jnp.where(is_fwd, sre_sc[rf, :], sre_sc[rb, :])
        s_im = jnp.where(is_fwd, sim_sc[rf, :], sim_sc[rb, :])
        ere, eim = are * ere - aim * eim + s_re, are * eim + aim * ere + s_im

    fwd_rows = lax.broadcasted_iota(jnp.int32, hfre_sc.shape, 1) < S5_STATE
    h = jnp.concatenate([jnp.where(fwd_rows, hfre_sc[...], hbre_sc[...]),
                         jnp.where(fwd_rows, hfim_sc[...], hbim_sc[...])], axis=1).astype(BF16)
    y = y + _dot(h, ri_ref[0])
    for p in range(S5_GROUP):
        y_ref[:, p, :] = y[:, p * CHUNK:(p + 1) * CHUNK]


def _s5_scan(u4, toep, rs, ri, ac):
    nchunk, nseq, W, _ = u4.shape
    R = nchunk * nseq
    n = S5_GROUP * CHUNK
    kern = functools.partial(_scan_kernel, nseq=nseq, nchunk=nchunk)
    y3 = pl.pallas_call(
        kern,
        grid=(S5_GROUPS,),
        in_specs=[
            pl.BlockSpec((R, S5_GROUP, CHUNK), lambda g: (0, g, 0)),
            pl.BlockSpec((1, n, n), lambda g: (g, 0, 0)),
            pl.BlockSpec((1, n, 4 * S5_STATE), lambda g: (g, 0, 0)),
            pl.BlockSpec((1, 4 * S5_STATE, n), lambda g: (g, 0, 0)),
            pl.BlockSpec((1, 8, 2 * S5_STATE), lambda g: (g, 0, 0)),
        ],
        out_specs=pl.BlockSpec((R, S5_GROUP, CHUNK), lambda g: (0, g, 0)),
        out_shape=jax.ShapeDtypeStruct((R, W, CHUNK), F32),
        scratch_shapes=[pltpu.VMEM((R, 2 * S5_STATE), F32)] * 6,
        compiler_params=_params("parallel"),
        name="s5_scan",
    )(u4.reshape(R, W, CHUNK), toep, rs, ri, ac)
    return y3.reshape(nchunk, nseq, W, CHUNK)


def _s5_out_kernel(ys_ref, u_ref, sg_ref, x_ref, d_ref, wgluT_ref, bglu_ref, wo_ref, g_ref, b_ref, o_ref):
    rep = ys_ref.shape[0]
    d = jnp.tile(d_ref[...], (1, rep))
    bglu = jnp.tile(bglu_ref[...], (1, rep))
    ys = jnp.concatenate([ys_ref[c, 0] for c in range(rep)], axis=1)
    u = jnp.concatenate([u_ref[c, 0] for c in range(rep)], axis=1)
    y = ys + d * u
    y = jax.nn.gelu(y)
    z = _dot(wgluT_ref[...], y.astype(BF16)) + bglu
    v = (y * _sigmoid(z) * sg_ref[...].astype(F32)).astype(BF16)
    r = ALPHA * x_ref[0] + _dot_tn(v, wo_ref[...])
    o_ref[0] = _layer_norm(r, g_ref[...], b_ref[...])


def _s5_out(ys4, u4, sgT, x, w, ln_g, ln_b, tm):
    B, L, _ = x.shape
    nl = L // tm
    tok = pl.BlockSpec((D_MODEL, tm), lambda b, i: (0, b * nl + i))
    sq = pl.BlockSpec((D_MODEL, D_MODEL), lambda b, i: (0, 0))
    col = pl.BlockSpec((D_MODEL, LANES), lambda b, i: (0, 0))
    row = pl.BlockSpec((1, D_MODEL), lambda b, i: (0, 0))
    return pl.pallas_call(
        _s5_out_kernel,
        grid=(B, nl),
        in_specs=[_chunk_spec(tm), _chunk_spec(tm), tok, pl.BlockSpec((1, tm, D_MODEL), lambda b, i: (b, i, 0)),
                  col, sq, col, sq, row, row],
        out_specs=pl.BlockSpec((1, tm, D_MODEL), lambda b, i: (b, i, 0)),
        out_shape=jax.ShapeDtypeStruct((B, L, D_MODEL), F32),
        compiler_params=_params("parallel", "parallel"),
        name="s5_out",
    )(ys4, u4, sgT, x, w['d'], w['wgluT'], w['bglu'], w['wo'], ln_g, ln_b)


def _prep_mla(w_in, g_q, w_q_up, g_kv, w_kv_up, w_out):
    nkv = Q_LORA + KV_LORA
    wa = jnp.concatenate([w_in[:, :nkv + QK_ROPE_DIM],
                          jnp.zeros((D_MODEL, LANES - QK_ROPE_DIM), F32)], axis=1)
    wq = w_q_up.reshape(Q_LORA, N_HEADS, QK_NOPE_DIM + QK_ROPE_DIM)
    wq = jnp.pad(wq, ((0, 0), (0, 0), (0, HEAD_PAD - QK_NOPE_DIM - QK_ROPE_DIM)))
    wkv = w_kv_up.reshape(KV_LORA, N_HEADS, QK_NOPE_DIM + V_DIM)
    return dict(
        wa=wa.astype(BF16),
        gq=g_q.reshape(1, Q_LORA), gkv=g_kv.reshape(1, KV_LORA),
        wqT=wq.reshape(Q_LORA, N_HEADS * HEAD_PAD).T.astype(BF16),
        wkn=wkv[:, :, :QK_NOPE_DIM].reshape(KV_LORA, N_HEADS * QK_NOPE_DIM).astype(BF16),
        wvT=wkv[:, :, QK_NOPE_DIM:].reshape(KV_LORA, N_HEADS * V_DIM).T.astype(BF16),
        wgT=w_in[:, nkv + QK_ROPE_DIM:].T.astype(BF16),
        wo=w_out.astype(BF16),
    )


def _rope_tables(L):
    inv = ROPE_THETA ** (-jnp.arange(0, QK_ROPE_DIM, 2, dtype=F32) / QK_ROPE_DIM)
    ang = jnp.arange(L, dtype=F32)[:, None] * inv[None, :]
    cos, sin = jnp.cos(ang), jnp.sin(ang)
    z = jnp.zeros_like(cos)
    z2 = jnp.zeros((L, LANES - QK_ROPE_DIM), F32)
    return dict(
        cosT=cos.T, sinT=sin.T,
        cosk=jnp.concatenate([cos, cos, z2], axis=1),
        sina=jnp.concatenate([z, sin, z2], axis=1),
        sinb=jnp.concatenate([-sin, z, z2], axis=1),
    )


def _powers(lam_bar, n):
    pw = jnp.stack([jnp.ones_like(lam_bar), lam_bar], axis=-2)
    top = lam_bar
    while pw.shape[-2] - 1 < n:
        pw = jnp.concatenate([pw, pw[..., 1:, :] * top[..., None, :]], axis=-2)
        top = top * top
    return pw


def _reim(z, axis):
    return jnp.concatenate([jnp.real(z), jnp.imag(z)], axis=axis)


def _prep_s5(w_in, a_re, a_im, log_step, b_re, b_im, c_re, c_im, d, w_glu, b_glu, w_out):
    G, N, P, T = S5_GROUPS, S5_STATE, S5_GROUP, CHUNK
    lam = lax.complex(a_re, a_im)
    step = jnp.exp(log_step)[..., None]
    lam_bar = jnp.exp(lam * step)
    b_bar = ((lam_bar - 1.0) / lam)[..., None] * lax.complex(b_re, b_im)
    c = lax.complex(c_re, c_im)
    pw = _powers(lam_bar, T)

    cb = c[:, :, None, :, :] * jnp.swapaxes(b_bar, -1, -2)[:, :, :, None, :]
    cb = cb.reshape(2, G, P * P, N)
    cbT = jnp.concatenate([_reim(cb[0], -1), _reim(cb[1], -1)], axis=-1)
    pf = jnp.swapaxes(pw[0, :, :T], -1, -2)
    pb = jnp.swapaxes(pw[1, :, :T], -1, -2)[..., ::-1]
    zf = jnp.zeros((G, N, T), pf.dtype)
    pf = jnp.concatenate([zf, pf], axis=-1)
    pb = jnp.concatenate([zf[..., :1], pb, zf[..., :T - 1]], axis=-1)
    pwT = jnp.concatenate([jnp.real(pf), -jnp.imag(pf), jnp.real(pb), -jnp.imag(pb)], axis=1)

    sf = pw[0, :, T - 1::-1]
    sb = pw[1, :, :T]
    rs_f = jnp.swapaxes(b_bar[0], -1, -2)[:, :, None, :] * sf[:, None, :, :]
    rs_b = jnp.swapaxes(b_bar[1], -1, -2)[:, :, None, :] * sb[:, None, :, :]
    rs = jnp.concatenate([jnp.real(rs_f), jnp.real(rs_b), jnp.imag(rs_f), jnp.imag(rs_b)], axis=-1)
    rs = rs.reshape(G, P * T, 4 * N).astype(BF16)

    of = pw[0, :, 1:T + 1]
    ob = pw[1, :, T:0:-1]
    ri_f = c[0][:, :, None, :] * of[:, None, :, :]
    ri_b = c[1][:, :, None, :] * ob[:, None, :, :]
    ri = jnp.concatenate([jnp.real(ri_f), jnp.real(ri_b), -jnp.imag(ri_f), -jnp.imag(ri_b)], axis=-1)
    ri = jnp.swapaxes(ri.reshape(G, P * T, 4 * N), -1, -2).astype(BF16)

    a = pw[:, :, T]
    rows = [jnp.concatenate([jnp.real(a[0]), jnp.real(a[1])], -1),
            jnp.concatenate([jnp.imag(a[0]), jnp.imag(a[1])], -1)]
    ac = jnp.stack(rows + [jnp.zeros_like(rows[0])] * 6, axis=1)

    return dict(
        wuT=w_in[:, :D_MODEL].T.astype(BF16), wgT=w_in[:, D_MODEL:].T.astype(BF16),
        cbT=cbT, pwT=pwT, rs=rs, ri=ri, ac=ac,
        d=jnp.broadcast_to(d[:, None], (D_MODEL, LANES)),
        wgluT=w_glu.T.astype(BF16),
        bglu=jnp.broadcast_to(b_glu[:, None], (D_MODEL, LANES)),
        wo=w_out.astype(BF16),
    )


def _mla_layer(x, w, tabs, ln_g, ln_b, tm):
    qT, k, vT, sgT, kn2 = _mla_proj(x, w, tabs, tm)
    ogT = _attention(qT, k, vT, sgT, kn2, tm)
    return _mla_out(ogT, x, w['wo'], ln_g, ln_b, tm)


def _s5_layer(x, w, toep, ln_g, ln_b, tm):
    u4, sgT = _s5_in(x, w['wuT'], w['wgT'], tm)
    ys4 = _s5_scan(u4, toep, w['rs'], w['ri'], w['ac'])
    return _s5_out(ys4, u4, sgT, x, w, ln_g, ln_b, tm)


def _trunk(x, mla_w, s5_w, toeps, ln_g, ln_b):
    L = x.shape[1]
    tm = min(TOK_TILE, L)
    tabs = _rope_tables(L)
    for i in range(DEPTH):
        g, b = ln_g[i].reshape(1, D_MODEL), ln_b[i].reshape(1, D_MODEL)
        if i % 2 == 0:
            x = _mla_layer(x, mla_w[i // 2], tabs, g, b, tm)
        else:
            x = _s5_layer(x, s5_w[i // 2], toeps[i // 2], g, b, tm)
    return x


def kernel(x_prompt, x_sample, mla_w_in, mla_g_q, mla_w_q_up, mla_g_kv, mla_w_kv_up, mla_w_out,
           s5_w_in, s5_a_re, s5_a_im, s5_log_step, s5_b_re, s5_b_im, s5_c_re, s5_c_im, s5_d,
           s5_w_glu, s5_b_glu, s5_w_out, ln_g, ln_b):
    mla = (mla_w_in, mla_g_q, mla_w_q_up, mla_g_kv, mla_w_kv_up, mla_w_out)
    s5 = (s5_w_in, s5_a_re, s5_a_im, s5_log_step, s5_b_re, s5_b_im, s5_c_re, s5_c_im, s5_d,
          s5_w_glu, s5_b_glu, s5_w_out)
    mla_w = [_prep_mla(*[w[j] for w in mla]) for j in range(mla_w_in.shape[0])]
    s5_w = [_prep_s5(*[w[j] for w in s5]) for j in range(s5_w_in.shape[0])]
    toeps = [_toeplitz(w['cbT'], w['pwT']) for w in s5_w]
    y_prompt = _trunk(x_prompt, mla_w, s5_w, toeps, ln_g, ln_b)
    y_sample = _trunk(x_sample, mla_w, s5_w, toeps, ln_g, ln_b)
    return (y_prompt, y_sample)
```

```python
import functools
import math

import jax
import jax.numpy as jnp
from jax import lax
from jax.experimental import pallas as pl
from jax.experimental.pallas import tpu as pltpu

F32 = jnp.float32
BF16 = jnp.bfloat16

D_MODEL = 1024
DEPTH = 4
N_HEADS = 8
QK_NOPE_DIM = 128
QK_ROPE_DIM = 64
V_DIM = 128
Q_LORA = 384
KV_LORA = 256
ROPE_THETA = 10000.0
ATTN_SCALE = 1.0 / math.sqrt(QK_NOPE_DIM + QK_ROPE_DIM)
S5_GROUP = 16
S5_GROUPS = D_MODEL // S5_GROUP
S5_STATE = 64
ALPHA = (2 * DEPTH) ** 0.25
LN_EPS = 1e-5
RMS_EPS = 1e-6

LANES = 128
HEAD_PAD = 256
ROPE_HALF = QK_ROPE_DIM // 2
LOG2_E = math.log2(math.e)
CHUNK = 128
TOK_TILE = 512
ATTN_Q_TILES = 2
ATTN_REF_KEYS = 128
ATTN_FAST_MARGIN = 60.0
VMEM_LIMIT = 56 * 1024 * 1024

NT_DIMS = (((1,), (1,)), ((), ()))
TN_DIMS = (((0,), (0,)), ((), ()))


def _dot(a, b):
    return jnp.dot(a, b, preferred_element_type=F32)


def _dot_nt(a, b):
    return lax.dot_general(a, b, NT_DIMS, preferred_element_type=F32)


def _dot_tn(a, b):
    return lax.dot_general(a, b, TN_DIMS, preferred_element_type=F32)


def _params(*sem):
    return pltpu.CompilerParams(dimension_semantics=sem, vmem_limit_bytes=VMEM_LIMIT)


def _layer_norm(r, g, b):
    mu = jnp.mean(r, axis=-1, keepdims=True)
    d = r - mu
    var = jnp.mean(d * d, axis=-1, keepdims=True)
    return d * lax.rsqrt(var + LN_EPS) * g + b


def _rms_norm(x, g):
    return x * lax.rsqrt(jnp.mean(x * x, axis=-1, keepdims=True) + RMS_EPS) * g


def _sigmoid(x):
    return 0.5 * jnp.tanh(0.5 * x) + 0.5


def _silu(x):
    h = 0.5 * x
    return h * jnp.tanh(h) + h


def _mla_proj_kernel(x_ref, wa_ref, gq_ref, gkv_ref, wqT_ref, wkn_ref, wvT_ref, wgT_ref,
                     cosT_ref, sinT_ref, cosk_ref, sina_ref, sinb_ref,
                     qT_ref, k_ref, vT_ref, sgT_ref, kn2_ref):
    xb = x_ref[0].astype(BF16)
    ha = _dot(xb, wa_ref[...])
    cq = _rms_norm(ha[:, :Q_LORA], gq_ref[...]).astype(BF16)
    ckv = _rms_norm(ha[:, Q_LORA:Q_LORA + KV_LORA], gkv_ref[...]).astype(BF16)
    kr = ha[:, Q_LORA + KV_LORA:]

    qT = _dot_nt(wqT_ref[...], cq) * (ATTN_SCALE * LOG2_E)
    cosT = cosT_ref[...]
    sinT = sinT_ref[...]
    tm = qT.shape[1]
    zpad = jnp.zeros((HEAD_PAD - QK_NOPE_DIM - QK_ROPE_DIM, tm), F32)
    for h in range(N_HEADS):
        blk = qT[h * HEAD_PAD:(h + 1) * HEAD_PAD]
        t1 = blk[QK_NOPE_DIM:QK_NOPE_DIM + ROPE_HALF]
        t2 = blk[QK_NOPE_DIM + ROPE_HALF:QK_NOPE_DIM + QK_ROPE_DIM]
        out = jnp.concatenate(
            [blk[:QK_NOPE_DIM], t1 * cosT - t2 * sinT, t2 * cosT + t1 * sinT, zpad], axis=0)
        qT_ref[0, 0, h * HEAD_PAD:(h + 1) * HEAD_PAD, :] = out.astype(BF16)

    kn = _dot(ckv, wkn_ref[...])
    krr = (kr * cosk_ref[...]
           + pltpu.roll(kr, ROPE_HALF, 1) * sina_ref[...]
           + pltpu.roll(kr, LANES - ROPE_HALF, 1) * sinb_ref[...]).astype(BF16)
    krf = krr.astype(F32)
    kr2 = jnp.sum(krf * krf, axis=1, keepdims=True)
    kn2_rows = []
    for h in range(N_HEADS):
        kb = kn[:, h * QK_NOPE_DIM:(h + 1) * QK_NOPE_DIM].astype(BF16)
        k_ref[0, h, :, :QK_NOPE_DIM] = kb
        k_ref[0, h, :, QK_NOPE_DIM:] = krr
        kf = kb.astype(F32)
        n2 = jnp.max(jnp.sum(kf * kf, axis=1, keepdims=True) + kr2, axis=0, keepdims=True)
        kn2_rows.append(jnp.broadcast_to(n2, (1, LANES)))
    kn2_ref[0, 0] = jnp.concatenate(kn2_rows, axis=0)

    vT_ref[0, 0] = _dot_nt(wvT_ref[...], ckv).astype(BF16)
    sgT_ref[0, 0] = _silu(_dot_nt(wgT_ref[...], xb)).astype(BF16)


def _mla_proj(x, w, tabs, tm):
    B, L, _ = x.shape
    nl = L // tm
    const = lambda shape: pl.BlockSpec(shape, lambda b, i: (0,) * len(shape))
    return pl.pallas_call(
        _mla_proj_kernel,
        grid=(B, nl),
        in_specs=[
            pl.BlockSpec((1, tm, D_MODEL), lambda b, i: (b, i, 0)),
            const(w['wa'].shape), const(w['gq'].shape), const(w['gkv'].shape),
            const(w['wqT'].shape), const(w['wkn'].shape), const(w['wvT'].shape), const(w['wgT'].shape),
            pl.BlockSpec((ROPE_HALF, tm), lambda b, i: (0, i)),
            pl.BlockSpec((ROPE_HALF, tm), lambda b, i: (0, i)),
            pl.BlockSpec((tm, LANES), lambda b, i: (i, 0)),
            pl.BlockSpec((tm, LANES), lambda b, i: (i, 0)),
            pl.BlockSpec((tm, LANES), lambda b, i: (i, 0)),
        ],
        out_specs=[
            pl.BlockSpec((1, 1, N_HEADS * HEAD_PAD, tm), lambda b, i: (b, i, 0, 0)),
            pl.BlockSpec((1, N_HEADS, tm, HEAD_PAD), lambda b, i: (b, 0, i, 0)),
            pl.BlockSpec((1, 1, N_HEADS * V_DIM, tm), lambda b, i: (b, i, 0, 0)),
            pl.BlockSpec((1, 1, N_HEADS * V_DIM, tm), lambda b, i: (b, i, 0, 0)),
            pl.BlockSpec((1, 1, N_HEADS, LANES), lambda b, i: (b, i, 0, 0)),
        ],
        out_shape=[
            jax.ShapeDtypeStruct((B, nl, N_HEADS * HEAD_PAD, tm), BF16),
            jax.ShapeDtypeStruct((B, N_HEADS, L, HEAD_PAD), BF16),
            jax.ShapeDtypeStruct((B, nl, N_HEADS * V_DIM, tm), BF16),
            jax.ShapeDtypeStruct((B, nl, N_HEADS * V_DIM, tm), BF16),
            jax.ShapeDtypeStruct((B, nl, N_HEADS, LANES), F32),
        ],
        compiler_params=_params("parallel", "parallel"),
        name="mla_proj",
    )(x, w['wa'], w['gq'], w['gkv'], w['wqT'], w['wkn'], w['wvT'], w['wgT'],
      tabs['cosT'], tabs['sinT'], tabs['cosk'], tabs['sina'], tabs['sinb'])


def _attn_kernel(qT_ref, k_ref, vT_ref, sgT_ref, kn2_ref, o_ref, m_sc, l_sc, acc_sc, *, tk, nk, nq):
    tq = qT_ref.shape[3]
    qTs = [qT_ref[0, t] for t in range(nq)]

    def k_tile(j):
        return k_ref[0, 0, pl.ds(pl.multiple_of(j * tk, tk), tk), :]

    k_pre = k_ref[0, 0, 0:ATTN_REF_KEYS, :]
    q2 = []
    for t in range(nq):
        cols = slice(t * tq, (t + 1) * tq)
        m_sc[:, cols] = jnp.max(_dot(k_pre, qTs[t]), axis=0, keepdims=True)
        qf = qTs[t].astype(F32)
        q2.append(jnp.sum(qf * qf, axis=0, keepdims=True))
    acc_sc[...] = jnp.zeros(acc_sc.shape, F32)
    l_sc[...] = jnp.zeros(l_sc.shape, F32)

    kn2 = jnp.max(kn2_ref[0, :, pl.ds(pl.program_id(1), 1), :])
    bound = jnp.sqrt(jnp.concatenate(q2, axis=1) * kn2)
    fast_ok = jnp.max(bound - m_sc[...]) <= ATTN_FAST_MARGIN

    @pl.when(fast_ok)
    def _fixed_reference():
        ms = [m_sc[:, t * tq:(t + 1) * tq] for t in range(nq)]

        def body(j, carry):
            ks = k_tile(j)
            ps = []
            for t in range(nq):
                cols = slice(t * tq, (t + 1) * tq)
                p = jnp.exp2(_dot(ks, qTs[t]) - ms[t])
                l_sc[:, cols] = l_sc[:, cols] + jnp.sum(p, axis=0, keepdims=True)
                ps.append(p.astype(BF16))
            for t in range(nq):
                cols = slice(t * tq, (t + 1) * tq)
                acc_sc[:, cols] = acc_sc[:, cols] + _dot(vT_ref[0, j], ps[t])
            return carry

        lax.fori_loop(0, nk, body, 0, unroll=8)

    @pl.when(jnp.logical_not(fast_ok))
    def _online():
        qT = jnp.concatenate(qTs, axis=1)

        def body(j, carry):
            sT = _dot(k_tile(j), qT)
            m_prev = m_sc[...]
            m_new = jnp.maximum(m_prev, jnp.max(sT, axis=0, keepdims=True))
            alpha = jnp.exp2(m_prev - m_new)
            p = jnp.exp2(sT - m_new)
            l_sc[...] = l_sc[...] * alpha + jnp.sum(p, axis=0, keepdims=True)
            acc_sc[...] = acc_sc[...] * alpha + _dot(vT_ref[0, j], p.astype(BF16))
            m_sc[...] = m_new
            return carry

        lax.fori_loop(0, nk, body, 0)

    for t in range(nq):
        cols = slice(t * tq, (t + 1) * tq)
        o = acc_sc[:, cols] / l_sc[:, cols] * sgT_ref[0, t].astype(F32)
        o_ref[0, t] = o.astype(BF16)


def _attention(qT, k, vT, sgT, kn2, tm):
    B, nl, _, _ = qT.shape
    L = nl * tm
    nq = ATTN_Q_TILES
    kern = functools.partial(_attn_kernel, tk=tm, nk=nl, nq=nq)
    return pl.pallas_call(
        kern,
        grid=(B, N_HEADS, nl // nq),
        in_specs=[
            pl.BlockSpec((1, nq, HEAD_PAD, tm), lambda b, h, i: (b, i, h, 0)),
            pl.BlockSpec((1, 1, L, HEAD_PAD), lambda b, h, i: (b, h, 0, 0)),
            pl.BlockSpec((1, nl, V_DIM, tm), lambda b, h, i: (b, 0, h, 0)),
            pl.BlockSpec((1, nq, V_DIM, tm), lambda b, h, i: (b, i, h, 0)),
            pl.BlockSpec((1, nl, N_HEADS, LANES), lambda b, h, i: (b, 0, 0, 0)),
        ],
        out_specs=pl.BlockSpec((1, nq, V_DIM, tm), lambda b, h, i: (b, i, h, 0)),
        out_shape=jax.ShapeDtypeStruct((B, nl, N_HEADS * V_DIM, tm), BF16),
        scratch_shapes=[pltpu.VMEM((1, nq * tm), F32), pltpu.VMEM((1, nq * tm), F32),
                        pltpu.VMEM((V_DIM, nq * tm), F32)],
        compiler_params=_params("parallel", "parallel", "arbitrary"),
        name="mla_attn",
    )(qT, k, vT, sgT, kn2)


def _out_kernel(zT_ref, x_ref, wo_ref, g_ref, b_ref, o_ref):
    y = _dot_tn(zT_ref[0, 0], wo_ref[...])
    r = ALPHA * x_ref[0] + y
    o_ref[0] = _layer_norm(r, g_ref[...], b_ref[...])


def _mla_out(ogT, x, woT, ln_g, ln_b, tm):
    B, L, _ = x.shape
    nl = L // tm
    return pl.pallas_call(
        _out_kernel,
        grid=(B, nl),
        in_specs=[
            pl.BlockSpec((1, 1, D_MODEL, tm), lambda b, i: (b, i, 0, 0)),
            pl.BlockSpec((1, tm, D_MODEL), lambda b, i: (b, i, 0)),
            pl.BlockSpec((D_MODEL, D_MODEL), lambda b, i: (0, 0)),
            pl.BlockSpec((1, D_MODEL), lambda b, i: (0, 0)),
            pl.BlockSpec((1, D_MODEL), lambda b, i: (0, 0)),
        ],
        out_specs=pl.BlockSpec((1, tm, D_MODEL), lambda b, i: (b, i, 0)),
        out_shape=jax.ShapeDtypeStruct((B, L, D_MODEL), F32),
        compiler_params=_params("parallel", "parallel"),
        name="mla_out",
    )(ogT, x, woT, ln_g, ln_b)


def _s5_in_kernel(x_ref, wuT_ref, wgT_ref, u_ref, sgT_ref):
    xb = x_ref[0].astype(BF16)
    uT = _dot_nt(wuT_ref[...], xb)
    for c in range(u_ref.shape[0]):
        u_ref[c, 0] = uT[:, c * CHUNK:(c + 1) * CHUNK]
    sgT_ref[...] = _silu(_dot_nt(wgT_ref[...], xb)).astype(BF16)


def _chunk_spec(tm):
    return pl.BlockSpec((tm // CHUNK, 1, D_MODEL, CHUNK), lambda b, i: (i, b, 0, 0))


def _s5_in(x, wuT, wgT, tm):
    B, L, _ = x.shape
    nl = L // tm
    tok = pl.BlockSpec((D_MODEL, tm), lambda b, i: (0, b * nl + i))
    wspec = pl.BlockSpec((D_MODEL, D_MODEL), lambda b, i: (0, 0))
    return pl.pallas_call(
        _s5_in_kernel,
        grid=(B, nl),
        in_specs=[pl.BlockSpec((1, tm, D_MODEL), lambda b, i: (b, i, 0)), wspec, wspec],
        out_specs=[_chunk_spec(tm), tok],
        out_shape=[jax.ShapeDtypeStruct((L // CHUNK, B, D_MODEL, CHUNK), F32),
                   jax.ShapeDtypeStruct((D_MODEL, B * L), BF16)],
        compiler_params=_params("parallel", "parallel"),
        name="s5_in",
    )(x, wuT, wgT)


def _toep_kernel(cbT_ref, pwT_ref, o_ref, kv_sc):
    kv_sc[...] = jnp.dot(cbT_ref[0], pwT_ref[0], preferred_element_type=F32,
                         precision=lax.Precision.HIGHEST)

    def body(pi, carry):
        row0 = pl.multiple_of(pi * CHUNK, CHUNK)
        for po in range(S5_GROUP):
            kv = kv_sc[pl.ds(pi * S5_GROUP + po, 1), :]
            rolled = pltpu.roll(jnp.broadcast_to(kv, (CHUNK, 2 * CHUNK)), 0, 1, stride=1, stride_axis=0)
            o_ref[0, pl.ds(row0, CHUNK), po * CHUNK:(po + 1) * CHUNK] = rolled[:, CHUNK:].astype(BF16)
        return carry

    lax.fori_loop(0, S5_GROUP, body, 0)


def _toeplitz(cbT, pwT):
    G = cbT.shape[0]
    n = S5_GROUP * CHUNK
    blk = pl.BlockSpec((1, 2 * CHUNK, 2 * CHUNK), lambda g: (g, 0, 0))
    return pl.pallas_call(
        _toep_kernel,
        grid=(G,),
        in_specs=[blk, blk],
        out_specs=pl.BlockSpec((1, n, n), lambda g: (g, 0, 0)),
        out_shape=jax.ShapeDtypeStruct((G, n, n), BF16),
        scratch_shapes=[pltpu.VMEM((2 * CHUNK, 2 * CHUNK), F32)],
        compiler_params=_params("parallel"),
        name="s5_toeplitz",
    )(cbT, pwT)


def _scan_kernel(u_ref, tm_ref, rs_ref, ri_ref, ac_ref, y_ref,
                 sre_sc, sim_sc, hfre_sc, hfim_sc, hbre_sc, hbim_sc, *, nseq, nchunk):
    lhs = jnp.concatenate([u_ref[:, p, :] for p in range(S5_GROUP)], axis=1).astype(BF16)
    s = _dot(lhs, rs_ref[0])
    y = _dot(lhs, tm_ref[0])
    tile = 2 * S5_STATE
    sre_sc[...] = s[:, :tile]
    sim_sc[...] = s[:, tile:]
    ac = ac_ref[0]
    are, aim = ac[0:1], ac[1:2]
    is_fwd = lax.broadcasted_iota(jnp.int32, (nseq, tile), 1) < S5_STATE

    ere = jnp.zeros((nseq, tile), F32)
    eim = jnp.zeros((nseq, tile), F32)
    for c in range(nchunk):
        rf = pl.ds(c * nseq, nseq)
        rb = pl.ds((nchunk - 1 - c) * nseq, nseq)
        hfre_sc[rf, :] = ere
        hfim_sc[rf, :] = eim
        hbre_sc[rb, :] = ere
        hbim_sc[rb, :] = eim
        s_re = jnp.where(is_fwd, sre_sc[rf, :], sre_sc[rb, :])
        s_im = jnp.where(is_fwd, sim_sc[rf, :], sim_sc[rb, :])
        ere, eim = are * ere - aim * eim + s_re, are * eim + aim * ere + s_im

    fwd_rows = lax.broadcasted_iota(jnp.int32, hfre_sc.shape, 1) < S5_STATE
    h = jnp.concatenate([jnp.where(fwd_rows, hfre_sc[...], hbre_sc[...]),
                         jnp.where(fwd_rows, hfim_sc[...], hbim_sc[...])], axis=1).astype(BF16)
    y = y + _dot(h, ri_ref[0])
    for p in range(S5_GROUP):
        y_ref[:, p, :] = y[:, p * CHUNK:(p + 1) * CHUNK]


def _s5_scan(u4, toep, rs, ri, ac):
    nchunk, nseq, W, _ = u4.shape
    R = nchunk * nseq
    n = S5_GROUP * CHUNK
    kern = functools.partial(_scan_kernel, nseq=nseq, nchunk=nchunk)
    y3 = pl.pallas_call(
        kern,
        grid=(S5_GROUPS,),
        in_specs=[
            pl.BlockSpec((R, S5_GROUP, CHUNK), lambda g: (0, g, 0)),
            pl.BlockSpec((1, n, n), lambda g: (g, 0, 0)),
            pl.BlockSpec((1, n, 4 * S5_STATE), lambda g: (g, 0, 0)),
            pl.BlockSpec((1, 4 * S5_STATE, n), lambda g: (g, 0, 0)),
            pl.BlockSpec((1, 8, 2 * S5_STATE), lambda g: (g, 0, 0)),
        ],
        out_specs=pl.BlockSpec((R, S5_GROUP, CHUNK), lambda g: (0, g, 0)),
        out_shape=jax.ShapeDtypeStruct((R, W, CHUNK), F32),
        scratch_shapes=[pltpu.VMEM((R, 2 * S5_STATE), F32)] * 6,
        compiler_params=_params("parallel"),
        name="s5_scan",
    )(u4.reshape(R, W, CHUNK), toep, rs, ri, ac)
    return y3.reshape(nchunk, nseq, W, CHUNK)


def _s5_out_kernel(ys_ref, u_ref, sg_ref, x_ref, d_ref, wgluT_ref, bglu_ref, wo_ref, g_ref, b_ref, o_ref):
    rep = ys_ref.shape[0]
    d = jnp.tile(d_ref[...], (1, rep))
    bglu = jnp.tile(bglu_ref[...], (1, rep))
    ys = jnp.concatenate([ys_ref[c, 0] for c in range(rep)], axis=1)
    u = jnp.concatenate([u_ref[c, 0] for c in range(rep)], axis=1)
    y = ys + d * u
    y = jax.nn.gelu(y)
    z = _dot(wgluT_ref[...], y.astype(BF16)) + bglu
    v = (y * _sigmoid(z) * sg_ref[...].astype(F32)).astype(BF16)
    r = ALPHA * x_ref[0] + _dot_tn(v, wo_ref[...])
    o_ref[0] = _layer_norm(r, g_ref[...], b_ref[...])


def _s5_out(ys4, u4, sgT, x, w, ln_g, ln_b, tm):
    B, L, _ = x.shape
    nl = L // tm
    tok = pl.BlockSpec((D_MODEL, tm), lambda b, i: (0, b * nl + i))
    sq = pl.BlockSpec((D_MODEL, D_MODEL), lambda b, i: (0, 0))
    col = pl.BlockSpec((D_MODEL, LANES), lambda b, i: (0, 0))
    row = pl.BlockSpec((1, D_MODEL), lambda b, i: (0, 0))
    return pl.pallas_call(
        _s5_out_kernel,
        grid=(B, nl),
        in_specs=[_chunk_spec(tm), _chunk_spec(tm), tok, pl.BlockSpec((1, tm, D_MODEL), lambda b, i: (b, i, 0)),
                  col, sq, col, sq, row, row],
        out_specs=pl.BlockSpec((1, tm, D_MODEL), lambda b, i: (b, i, 0)),
        out_shape=jax.ShapeDtypeStruct((B, L, D_MODEL), F32),
        compiler_params=_params("parallel", "parallel"),
        name="s5_out",
    )(ys4, u4, sgT, x, w['d'], w['wgluT'], w['bglu'], w['wo'], ln_g, ln_b)


def _prep_mla(w_in, g_q, w_q_up, g_kv, w_kv_up, w_out):
    nkv = Q_LORA + KV_LORA
    wa = jnp.concatenate([w_in[:, :nkv + QK_ROPE_DIM],
                          jnp.zeros((D_MODEL, LANES - QK_ROPE_DIM), F32)], axis=1)
    wq = w_q_up.reshape(Q_LORA, N_HEADS, QK_NOPE_DIM + QK_ROPE_DIM)
    wq = jnp.pad(wq, ((0, 0), (0, 0), (0, HEAD_PAD - QK_NOPE_DIM - QK_ROPE_DIM)))
    wkv = w_kv_up.reshape(KV_LORA, N_HEADS, QK_NOPE_DIM + V_DIM)
    return dict(
        wa=wa.astype(BF16),
        gq=g_q.reshape(1, Q_LORA), gkv=g_kv.reshape(1, KV_LORA),
        wqT=wq.reshape(Q_LORA, N_HEADS * HEAD_PAD).T.astype(BF16),
        wkn=wkv[:, :, :QK_NOPE_DIM].reshape(KV_LORA, N_HEADS * QK_NOPE_DIM).astype(BF16),
        wvT=wkv[:, :, QK_NOPE_DIM:].reshape(KV_LORA, N_HEADS * V_DIM).T.astype(BF16),
        wgT=w_in[:, nkv + QK_ROPE_DIM:].T.astype(BF16),
        wo=w_out.astype(BF16),
    )


def _rope_tables(L):
    inv = ROPE_THETA ** (-jnp.arange(0, QK_ROPE_DIM, 2, dtype=F32) / QK_ROPE_DIM)
    ang = jnp.arange(L, dtype=F32)[:, None] * inv[None, :]
    cos, sin = jnp.cos(ang), jnp.sin(ang)
    z = jnp.zeros_like(cos)
    z2 = jnp.zeros((L, LANES - QK_ROPE_DIM), F32)
    return dict(
        cosT=cos.T, sinT=sin.T,
        cosk=jnp.concatenate([cos, cos, z2], axis=1),
        sina=jnp.concatenate([z, sin, z2], axis=1),
        sinb=jnp.concatenate([-sin, z, z2], axis=1),
    )


def _powers(lam_bar, n):
    pw = jnp.stack([jnp.ones_like(lam_bar), lam_bar], axis=-2)
    top = lam_bar
    while pw.shape[-2] - 1 < n:
        pw = jnp.concatenate([pw, pw[..., 1:, :] * top[..., None, :]], axis=-2)
        top = top * top
    return pw


def _reim(z, axis):
    return jnp.concatenate([jnp.real(z), jnp.imag(z)], axis=axis)


def _prep_s5(w_in, a_re, a_im, log_step, b_re, b_im, c_re, c_im, d, w_glu, b_glu, w_out):
    G, N, P, T = S5_GROUPS, S5_STATE, S5_GROUP, CHUNK
    lam = lax.complex(a_re, a_im)
    step = jnp.exp(log_step)[..., None]
    lam_bar = jnp.exp(lam * step)
    b_bar = ((lam_bar - 1.0) / lam)[..., None] * lax.complex(b_re, b_im)
    c = lax.complex(c_re, c_im)
    pw = _powers(lam_bar, T)

    cb = c[:, :, None, :, :] * jnp.swapaxes(b_bar, -1, -2)[:, :, :, None, :]
    cb = cb.reshape(2, G, P * P, N)
    cbT = jnp.concatenate([_reim(cb[0], -1), _reim(cb[1], -1)], axis=-1)
    pf = jnp.swapaxes(pw[0, :, :T], -1, -2)
    pb = jnp.swapaxes(pw[1, :, :T], -1, -2)[..., ::-1]
    zf = jnp.zeros((G, N, T), pf.dtype)
    pf = jnp.concatenate([zf, pf], axis=-1)
    pb = jnp.concatenate([zf[..., :1], pb, zf[..., :T - 1]], axis=-1)
    pwT = jnp.concatenate([jnp.real(pf), -jnp.imag(pf), jnp.real(pb), -jnp.imag(pb)], axis=1)

    sf = pw[0, :, T - 1::-1]
    sb = pw[1, :, :T]
    rs_f = jnp.swapaxes(b_bar[0], -1, -2)[:, :, None, :] * sf[:, None, :, :]
    rs_b = jnp.swapaxes(b_bar[1], -1, -2)[:, :, None, :] * sb[:, None, :, :]
    rs = jnp.concatenate([jnp.real(rs_f), jnp.real(rs_b), jnp.imag(rs_f), jnp.imag(rs_b)], axis=-1)
    rs = rs.reshape(G, P * T, 4 * N).astype(BF16)

    of = pw[0, :, 1:T + 1]
    ob = pw[1, :, T:0:-1]
    ri_f = c[0][:, :, None, :] * of[:, None, :, :]
    ri_b = c[1][:, :, None, :] * ob[:, None, :, :]
    ri = jnp.concatenate([jnp.real(ri_f), jnp.real(ri_b), -jnp.imag(ri_f), -jnp.imag(ri_b)], axis=-1)
    ri = jnp.swapaxes(ri.reshape(G, P * T, 4 * N), -1, -2).astype(BF16)

    a = pw[:, :, T]
    rows = [jnp.concatenate([jnp.real(a[0]), jnp.real(a[1])], -1),
            jnp.concatenate([jnp.imag(a[0]), jnp.imag(a[1])], -1)]
    ac = jnp.stack(rows + [jnp.zeros_like(rows[0])] * 6, axis=1)

    return dict(
        wuT=w_in[:, :D_MODEL].T.astype(BF16), wgT=w_in[:, D_MODEL:].T.astype(BF16),
        cbT=cbT, pwT=pwT, rs=rs, ri=ri, ac=ac,
        d=jnp.broadcast_to(d[:, None], (D_MODEL, LANES)),
        wgluT=w_glu.T.astype(BF16),
        bglu=jnp.broadcast_to(b_glu[:, None], (D_MODEL, LANES)),
        wo=w_out.astype(BF16),
    )


def _mla_layer(x, w, tabs, ln_g, ln_b, tm):
    qT, k, vT, sgT, kn2 = _mla_proj(x, w, tabs, tm)
    ogT = _attention(qT, k, vT, sgT, kn2, tm)
    return _mla_out(ogT, x, w['wo'], ln_g, ln_b, tm)


def _s5_layer(x, w, toep, ln_g, ln_b, tm):
    u4, sgT = _s5_in(x, w['wuT'], w['wgT'], tm)
    ys4 = _s5_scan(u4, toep, w['rs'], w['ri'], w['ac'])
    return _s5_out(ys4, u4, sgT, x, w, ln_g, ln_b, tm)


def _trunk(x, mla_w, s5_w, toeps, ln_g, ln_b):
    L = x.shape[1]
    tm = min(TOK_TILE, L)
    tabs = _rope_tables(L)
    for i in range(DEPTH):
        g, b = ln_g[i].reshape(1, D_MODEL), ln_b[i].reshape(1, D_MODEL)
        if i % 2 == 0:
            x = _mla_layer(x, mla_w[i // 2], tabs, g, b, tm)
        else:
            x = _s5_layer(x, s5_w[i // 2], toeps[i // 2], g, b, tm)
    return x


def kernel(x_prompt, x_sample, mla_w_in, mla_g_q, mla_w_q_up, mla_g_kv, mla_w_kv_up, mla_w_out,
           s5_w_in, s5_a_re, s5_a_im, s5_log_step, s5_b_re, s5_b_im, s5_c_re, s5_c_im, s5_d,
           s5_w_glu, s5_b_glu, s5_w_out, ln_g, ln_b):
    mla = (mla_w_in, mla_g_q, mla_w_q_up, mla_g_kv, mla_w_kv_up, mla_w_out)
    s5 = (s5_w_in, s5_a_re, s5_a_im, s5_log_step, s5_b_re, s5_b_im, s5_c_re, s5_c_im, s5_d,
          s5_w_glu, s5_b_glu, s5_w_out)
    mla_w = [_prep_mla(*[w[j] for w in mla]) for j in range(mla_w_in.shape[0])]
    s5_w = [_prep_s5(*[w[j] for w in s5]) for j in range(s5_w_in.shape[0])]
    toeps = [_toeplitz(w['cbT'], w['pwT']) for w in s5_w]
    y_prompt = _trunk(x_prompt, mla_w, s5_w, toeps, ln_g, ln_b)
    y_sample = _trunk(x_sample, mla_w, s5_w, toeps, ln_g, ln_b)
    return (y_prompt, y_sample)
```

```python
import functools
import math

import jax
import jax.numpy as jnp
from jax import lax
from jax.experimental import pallas as pl
from jax.experimental.pallas import tpu as pltpu

F32 = jnp.float32
BF16 = jnp.bfloat16

D_MODEL = 1024
DEPTH = 4
N_HEADS = 8
QK_NOPE_DIM = 128
QK_ROPE_DIM = 64
V_DIM = 128
Q_LORA = 384
KV_LORA = 256
ROPE_THETA = 10000.0
ATTN_SCALE = 1.0 / math.sqrt(QK_NOPE_DIM + QK_ROPE_DIM)
S5_GROUP = 16
S5_GROUPS = D_MODEL // S5_GROUP
S5_STATE = 64
ALPHA = (2 * DEPTH) ** 0.25
LN_EPS = 1e-5
RMS_EPS = 1e-6

LANES = 128
HEAD_PAD = 256
ROPE_HALF = QK_ROPE_DIM // 2
LOG2_E = math.log2(math.e)
CHUNK = 128
TOK_TILE = 512
ATTN_Q_TILES = 4
ATTN_REF_KEYS = 128
ATTN_FAST_MARGIN = 60.0
VMEM_LIMIT = 56 * 1024 * 1024

NT_DIMS = (((1,), (1,)), ((), ()))
TN_DIMS = (((0,), (0,)), ((), ()))


def _dot(a, b):
    return jnp.dot(a, b, preferred_element_type=F32)


def _dot_nt(a, b):
    return lax.dot_general(a, b, NT_DIMS, preferred_element_type=F32)


def _dot_tn(a, b):
    return lax.dot_general(a, b, TN_DIMS, preferred_element_type=F32)


def _params(*sem):
    return pltpu.CompilerParams(dimension_semantics=sem, vmem_limit_bytes=VMEM_LIMIT)


def _layer_norm(r, g, b):
    mu = jnp.mean(r, axis=-1, keepdims=True)
    d = r - mu
    var = jnp.mean(d * d, axis=-1, keepdims=True)
    return d * lax.rsqrt(var + LN_EPS) * g + b


def _rms_norm(x, g):
    return x * lax.rsqrt(jnp.mean(x * x, axis=-1, keepdims=True) + RMS_EPS) * g


def _sigmoid(x):
    return 0.5 * jnp.tanh(0.5 * x) + 0.5


def _silu(x):
    h = 0.5 * x
    return h * jnp.tanh(h) + h


def _mla_proj_kernel(x_ref, wa_ref, gq_ref, gkv_ref, wqT_ref, wkn_ref, wvT_ref, wgT_ref,
                     cosT_ref, sinT_ref, cosk_ref, sina_ref, sinb_ref,
                     qT_ref, k_ref, vT_ref, sgT_ref, kn2_ref):
    xb = x_ref[0].astype(BF16)
    ha = _dot(xb, wa_ref[...])
    cq = _rms_norm(ha[:, :Q_LORA], gq_ref[...]).astype(BF16)
    ckv = _rms_norm(ha[:, Q_LORA:Q_LORA + KV_LORA], gkv_ref[...]).astype(BF16)
    kr = ha[:, Q_LORA + KV_LORA:]

    qT = _dot_nt(wqT_ref[...], cq) * (ATTN_SCALE * LOG2_E)
    cosT = cosT_ref[...]
    sinT = sinT_ref[...]
    tm = qT.shape[1]
    zpad = jnp.zeros((HEAD_PAD - QK_NOPE_DIM - QK_ROPE_DIM, tm), F32)
    for h in range(N_HEADS):
        blk = qT[h * HEAD_PAD:(h + 1) * HEAD_PAD]
        t1 = blk[QK_NOPE_DIM:QK_NOPE_DIM + ROPE_HALF]
        t2 = blk[QK_NOPE_DIM + ROPE_HALF:QK_NOPE_DIM + QK_ROPE_DIM]
        out = jnp.concatenate(
            [blk[:QK_NOPE_DIM], t1 * cosT - t2 * sinT, t2 * cosT + t1 * sinT, zpad], axis=0)
        qT_ref[0, 0, h * HEAD_PAD:(h + 1) * HEAD_PAD, :] = out.astype(BF16)

    kn = _dot(ckv, wkn_ref[...])
    krr = (kr * cosk_ref[...]
           + pltpu.roll(kr, ROPE_HALF, 1) * sina_ref[...]
           + pltpu.roll(kr, LANES - ROPE_HALF, 1) * sinb_ref[...]).astype(BF16)
    krf = krr.astype(F32)
    kr2 = jnp.sum(krf * krf, axis=1, keepdims=True)
    kn2_rows = []
    for h in range(N_HEADS):
        kb = kn[:, h * QK_NOPE_DIM:(h + 1) * QK_NOPE_DIM].astype(BF16)
        k_ref[0, h, :, :QK_NOPE_DIM] = kb
        k_ref[0, h, :, QK_NOPE_DIM:] = krr
        kf = kb.astype(F32)
        n2 = jnp.max(jnp.sum(kf * kf, axis=1, keepdims=True) + kr2, axis=0, keepdims=True)
        kn2_rows.append(jnp.broadcast_to(n2, (1, LANES)))
    kn2_ref[0, 0] = jnp.concatenate(kn2_rows, axis=0)

    vT_ref[0, 0] = _dot_nt(wvT_ref[...], ckv).astype(BF16)
    sgT_ref[0, 0] = _silu(_dot_nt(wgT_ref[...], xb)).astype(BF16)


def _mla_proj(x, w, tabs, tm):
    B, L, _ = x.shape
    nl = L // tm
    const = lambda shape: pl.BlockSpec(shape, lambda b, i: (0,) * len(shape))
    return pl.pallas_call(
        _mla_proj_kernel,
        grid=(B, nl),
        in_specs=[
            pl.BlockSpec((1, tm, D_MODEL), lambda b, i: (b, i, 0)),
            const(w['wa'].shape), const(w['gq'].shape), const(w['gkv'].shape),
            const(w['wqT'].shape), const(w['wkn'].shape), const(w['wvT'].shape), const(w['wgT'].shape),
            pl.BlockSpec((ROPE_HALF, tm), lambda b, i: (0, i)),
            pl.BlockSpec((ROPE_HALF, tm), lambda b, i: (0, i)),
            pl.BlockSpec((tm, LANES), lambda b, i: (i, 0)),
            pl.BlockSpec((tm, LANES), lambda b, i: (i, 0)),
            pl.BlockSpec((tm, LANES), lambda b, i: (i, 0)),
        ],
        out_specs=[
            pl.BlockSpec((1, 1, N_HEADS * HEAD_PAD, tm), lambda b, i: (b, i, 0, 0)),
            pl.BlockSpec((1, N_HEADS, tm, HEAD_PAD), lambda b, i: (b, 0, i, 0)),
            pl.BlockSpec((1, 1, N_HEADS * V_DIM, tm), lambda b, i: (b, i, 0, 0)),
            pl.BlockSpec((1, 1, N_HEADS * V_DIM, tm), lambda b, i: (b, i, 0, 0)),
            pl.BlockSpec((1, 1, N_HEADS, LANES), lambda b, i: (b, i, 0, 0)),
        ],
        out_shape=[
            jax.ShapeDtypeStruct((B, nl, N_HEADS * HEAD_PAD, tm), BF16),
            jax.ShapeDtypeStruct((B, N_HEADS, L, HEAD_PAD), BF16),
            jax.ShapeDtypeStruct((B, nl, N_HEADS * V_DIM, tm), BF16),
            jax.ShapeDtypeStruct((B, nl, N_HEADS * V_DIM, tm), BF16),
            jax.ShapeDtypeStruct((B, nl, N_HEADS, LANES), F32),
        ],
        compiler_params=_params("parallel", "parallel"),
        name="mla_proj",
    )(x, w['wa'], w['gq'], w['gkv'], w['wqT'], w['wkn'], w['wvT'], w['wgT'],
      tabs['cosT'], tabs['sinT'], tabs['cosk'], tabs['sina'], tabs['sinb'])


def _attn_kernel(qT_ref, k_ref, vT_ref, sgT_ref, kn2_ref, o_ref, m_sc, l_sc, acc_sc, *, tk, nk, nq):
    tq = qT_ref.shape[3]
    qTs = [qT_ref[0, t] for t in range(nq)]

    def k_tile(j):
        return k_ref[0, 0, pl.ds(pl.multiple_of(j * tk, tk), tk), :]

    k_pre = k_ref[0, 0, 0:ATTN_REF_KEYS, :]
    q2 = []
    for t in range(nq):
        cols = slice(t * tq, (t + 1) * tq)
        m_sc[:, cols] = jnp.max(_dot(k_pre, qTs[t]), axis=0, keepdims=True)
        qf = qTs[t].astype(F32)
        q2.append(jnp.sum(qf * qf, axis=0, keepdims=True))
    acc_sc[...] = jnp.zeros(acc_sc.shape, F32)
    l_sc[...] = jnp.zeros(l_sc.shape, F32)

    kn2 = jnp.max(kn2_ref[0, :, pl.ds(pl.program_id(1), 1), :])
    bound = jnp.sqrt(jnp.concatenate(q2, axis=1) * kn2)
    fast_ok = jnp.max(bound - m_sc[...]) <= ATTN_FAST_MARGIN

    @pl.when(fast_ok)
    def _fixed_reference():
        ms = [m_sc[:, t * tq:(t + 1) * tq] for t in range(nq)]

        def body(j, carry):
            ks = k_tile(j)
            ps = []
            for t in range(nq):
                cols = slice(t * tq, (t + 1) * tq)
                p = jnp.exp2(_dot(ks, qTs[t]) - ms[t])
                l_sc[:, cols] = l_sc[:, cols] + jnp.sum(p, axis=0, keepdims=True)
                ps.append(p.astype(BF16))
            for t in range(nq):
                cols = slice(t * tq, (t + 1) * tq)
                acc_sc[:, cols] = acc_sc[:, cols] + _dot(vT_ref[0, j], ps[t])
            return carry

        lax.fori_loop(0, nk, body, 0, unroll=8)

    @pl.when(jnp.logical_not(fast_ok))
    def _online():
        qT = jnp.concatenate(qTs, axis=1)

        def body(j, carry):
            sT = _dot(k_tile(j), qT)
            m_prev = m_sc[...]
            m_new = jnp.maximum(m_prev, jnp.max(sT, axis=0, keepdims=True))
            alpha = jnp.exp2(m_prev - m_new)
            p = jnp.exp2(sT - m_new)
            l_sc[...] = l_sc[...] * alpha + jnp.sum(p, axis=0, keepdims=True)
            acc_sc[...] = acc_sc[...] * alpha + _dot(vT_ref[0, j], p.astype(BF16))
            m_sc[...] = m_new
            return carry

        lax.fori_loop(0, nk, body, 0)

    for t in range(nq):
        cols = slice(t * tq, (t + 1) * tq)
        o = acc_sc[:, cols] / l_sc[:, cols] * sgT_ref[0, t].astype(F32)
        o_ref[0, t] = o.astype(BF16)


def _attention(qT, k, vT, sgT, kn2, tm):
    B, nl, _, _ = qT.shape
    L = nl * tm
    nq = ATTN_Q_TILES
    kern = functools.partial(_attn_kernel, tk=tm, nk=nl, nq=nq)
    return pl.pallas_call(
        kern,
        grid=(B, N_HEADS, nl // nq),
        in_specs=[
            pl.BlockSpec((1, nq, HEAD_PAD, tm), lambda b, h, i: (b, i, h, 0)),
            pl.BlockSpec((1, 1, L, HEAD_PAD), lambda b, h, i: (b, h, 0, 0)),
            pl.BlockSpec((1, nl, V_DIM, tm), lambda b, h, i: (b, 0, h, 0)),
            pl.BlockSpec((1, nq, V_DIM, tm), lambda b, h, i: (b, i, h, 0)),
            pl.BlockSpec((1, nl, N_HEADS, LANES), lambda b, h, i: (b, 0, 0, 0)),
        ],
        out_specs=pl.BlockSpec((1, nq, V_DIM, tm), lambda b, h, i: (b, i, h, 0)),
        out_shape=jax.ShapeDtypeStruct((B, nl, N_HEADS * V_DIM, tm), BF16),
        scratch_shapes=[pltpu.VMEM((1, nq * tm), F32), pltpu.VMEM((1, nq * tm), F32),
                        pltpu.VMEM((V_DIM, nq * tm), F32)],
        compiler_params=_params("parallel", "parallel", "arbitrary"),
        name="mla_attn",
    )(qT, k, vT, sgT, kn2)


def _out_kernel(zT_ref, x_ref, wo_ref, g_ref, b_ref, o_ref):
    half = zT_ref.shape[3] // 2
    ys = [_dot_tn(zT_ref[0, 0, :, i * half:(i + 1) * half], wo_ref[...]) for i in range(2)]
    for i in range(2):
        rows = slice(i * half, (i + 1) * half)
        o_ref[0, rows] = _layer_norm(ALPHA * x_ref[0, rows] + ys[i], g_ref[...], b_ref[...])


def _mla_out(ogT, x, woT, ln_g, ln_b, tm):
    B, L, _ = x.shape
    nl = L // tm
    return pl.pallas_call(
        _out_kernel,
        grid=(B, nl),
        in_specs=[
            pl.BlockSpec((1, 1, D_MODEL, tm), lambda b, i: (b, i, 0, 0)),
            pl.BlockSpec((1, tm, D_MODEL), lambda b, i: (b, i, 0)),
            pl.BlockSpec((D_MODEL, D_MODEL), lambda b, i: (0, 0)),
            pl.BlockSpec((1, D_MODEL), lambda b, i: (0, 0)),
            pl.BlockSpec((1, D_MODEL), lambda b, i: (0, 0)),
        ],
        out_specs=pl.BlockSpec((1, tm, D_MODEL), lambda b, i: (b, i, 0)),
        out_shape=jax.ShapeDtypeStruct((B, L, D_MODEL), F32),
        compiler_params=_params("parallel", "parallel"),
        name="mla_out",
    )(ogT, x, woT, ln_g, ln_b)


def _s5_in_kernel(x_ref, wuT_ref, wgT_ref, u_ref, sgT_ref):
    xb = x_ref[0].astype(BF16)
    uT = _dot_nt(wuT_ref[...], xb)
    for c in range(u_ref.shape[0]):
        u_ref[c, 0] = uT[:, c * CHUNK:(c + 1) * CHUNK]
    sgT_ref[...] = _silu(_dot_nt(wgT_ref[...], xb)).astype(BF16)


def _chunk_spec(tm):
    return pl.BlockSpec((tm // CHUNK, 1, D_MODEL, CHUNK), lambda b, i: (i, b, 0, 0))


def _s5_in(x, wuT, wgT, tm):
    B, L, _ = x.shape
    nl = L // tm
    tok = pl.BlockSpec((D_MODEL, tm), lambda b, i: (0, b * nl + i))
    wspec = pl.BlockSpec((D_MODEL, D_MODEL), lambda b, i: (0, 0))
    return pl.pallas_call(
        _s5_in_kernel,
        grid=(B, nl),
        in_specs=[pl.BlockSpec((1, tm, D_MODEL), lambda b, i: (b, i, 0)), wspec, wspec],
        out_specs=[_chunk_spec(tm), tok],
        out_shape=[jax.ShapeDtypeStruct((L // CHUNK, B, D_MODEL, CHUNK), F32),
                   jax.ShapeDtypeStruct((D_MODEL, B * L), BF16)],
        compiler_params=_params("parallel", "parallel"),
        name="s5_in",
    )(x, wuT, wgT)


def _toep_kernel(cbT_ref, pwT_ref, o_ref, kv_sc):
    kv_sc[...] = jnp.dot(cbT_ref[0], pwT_ref[0], preferred_element_type=F32,
                         precision=lax.Precision.HIGHEST)

    def body(pi, carry):
        row0 = pl.multiple_of(pi * CHUNK, CHUNK)
        for po in range(S5_GROUP):
            kv = kv_sc[pl.ds(pi * S5_GROUP + po, 1), :]
            rolled = pltpu.roll(jnp.broadcast_to(kv, (CHUNK, 2 * CHUNK)), 0, 1, stride=1, stride_axis=0)
            o_ref[0, pl.ds(row0, CHUNK), po * CHUNK:(po + 1) * CHUNK] = rolled[:, CHUNK:].astype(BF16)
        return carry

    lax.fori_loop(0, S5_GROUP, body, 0)


def _toeplitz(cbT, pwT):
    G = cbT.shape[0]
    n = S5_GROUP * CHUNK
    blk = pl.BlockSpec((1, 2 * CHUNK, 2 * CHUNK), lambda g: (g, 0, 0))
    return pl.pallas_call(
        _toep_kernel,
        grid=(G,),
        in_specs=[blk, blk],
        out_specs=pl.BlockSpec((1, n, n), lambda g: (g, 0, 0)),
        out_shape=jax.ShapeDtypeStruct((G, n, n), BF16),
        scratch_shapes=[pltpu.VMEM((2 * CHUNK, 2 * CHUNK), F32)],
        compiler_params=_params("parallel"),
        name="s5_toeplitz",
    )(cbT, pwT)


def _scan_kernel(u_ref, tm_ref, rs_ref, ri_ref, ac_ref, y_ref,
                 sre_sc, sim_sc, hfre_sc, hfim_sc, hbre_sc, hbim_sc, *, nseq, nchunk):
    lhs = jnp.concatenate([u_ref[:, p, :] for p in range(S5_GROUP)], axis=1).astype(BF16)
    s = _dot(lhs, rs_ref[0])
    y = _dot(lhs, tm_ref[0])
    tile = 2 * S5_STATE
    sre_sc[...] = s[:, :tile]
    sim_sc[...] = s[:, tile:]
    ac = ac_ref[0]
    are, aim = ac[0:1], ac[1:2]
    is_fwd = lax.broadcasted_iota(jnp.int32, (nseq, tile), 1) < S5_STATE

    ere = jnp.zeros((nseq, tile), F32)
    eim = jnp.zeros((nseq, tile), F32)
    for c in range(nchunk):
        rf = pl.ds(c * nseq, nseq)
        rb = pl.ds((nchunk - 1 - c) * nseq, nseq)
        hfre_sc[rf, :] = ere
        hfim_sc[rf, :] = eim
        hbre_sc[rb, :] = ere
        hbim_sc[rb, :] = eim
        s_re = jnp.where(is_fwd, sre_sc[rf, :], sre_sc[rb, :])
        s_im = jnp.where(is_fwd, sim_sc[rf, :], sim_sc[rb, :])
        ere, eim = are * ere - aim * eim + s_re, are * eim + aim * ere + s_im

    fwd_rows = lax.broadcasted_iota(jnp.int32, hfre_sc.shape, 1) < S5_STATE
    h = jnp.concatenate([jnp.where(fwd_rows, hfre_sc[...], hbre_sc[...]),
                         jnp.where(fwd_rows, hfim_sc[...], hbim_sc[...])], axis=1).astype(BF16)
    y = y + _dot(h, ri_ref[0])
    for p in range(S5_GROUP):
        y_ref[:, p, :] = y[:, p * CHUNK:(p + 1) * CHUNK]


def _s5_scan(u4, toep, rs, ri, ac):
    nchunk, nseq, W, _ = u4.shape
    R = nchunk * nseq
    n = S5_GROUP * CHUNK
    kern = functools.partial(_scan_kernel, nseq=nseq, nchunk=nchunk)
    y3 = pl.pallas_call(
        kern,
        grid=(S5_GROUPS,),
        in_specs=[
            pl.BlockSpec((R, S5_GROUP, CHUNK), lambda g: (0, g, 0)),
            pl.BlockSpec((1, n, n), lambda g: (g, 0, 0)),
            pl.BlockSpec((1, n, 4 * S5_STATE), lambda g: (g, 0, 0)),
            pl.BlockSpec((1, 4 * S5_STATE, n), lambda g: (g, 0, 0)),
            pl.BlockSpec((1, 8, 2 * S5_STATE), lambda g: (g, 0, 0)),
        ],
        out_specs=pl.BlockSpec((R, S5_GROUP, CHUNK), lambda g: (0, g, 0)),
        out_shape=jax.ShapeDtypeStruct((R, W, CHUNK), F32),
        scratch_shapes=[pltpu.VMEM((R, 2 * S5_STATE), F32)] * 6,
        compiler_params=_params("parallel"),
        name="s5_scan",
    )(u4.reshape(R, W, CHUNK), toep, rs, ri, ac)
    return y3.reshape(nchunk, nseq, W, CHUNK)


def _s5_out_kernel(ys_ref, u_ref, sg_ref, x_ref, d_ref, wgluT_ref, bglu_ref, wo_ref, g_ref, b_ref, o_ref):
    rep = ys_ref.shape[0] // 2
    half = rep * CHUNK
    d = jnp.tile(d_ref[...], (1, rep))
    bglu = jnp.tile(bglu_ref[...], (1, rep))

    def gather(ref, i):
        return jnp.concatenate([ref[i * rep + c, 0] for c in range(rep)], axis=1)

    y = [jax.nn.gelu(gather(ys_ref, i) + d * gather(u_ref, i)) for i in range(2)]
    z = [_dot(wgluT_ref[...], y[i].astype(BF16)) + bglu for i in range(2)]
    v = [(y[i] * _sigmoid(z[i]) * sg_ref[:, i * half:(i + 1) * half].astype(F32)).astype(BF16) for i in range(2)]
    o = [_dot_tn(v[i], wo_ref[...]) for i in range(2)]
    for i in range(2):
        rows = slice(i * half, (i + 1) * half)
        o_ref[0, rows] = _layer_norm(ALPHA * x_ref[0, rows] + o[i], g_ref[...], b_ref[...])


def _s5_out(ys4, u4, sgT, x, w, ln_g, ln_b, tm):
    B, L, _ = x.shape
    nl = L // tm
    tok = pl.BlockSpec((D_MODEL, tm), lambda b, i: (0, b * nl + i))
    sq = pl.BlockSpec((D_MODEL, D_MODEL), lambda b, i: (0, 0))
    col = pl.BlockSpec((D_MODEL, LANES), lambda b, i: (0, 0))
    row = pl.BlockSpec((1, D_MODEL), lambda b, i: (0, 0))
    return pl.pallas_call(
        _s5_out_kernel,
        grid=(B, nl),
        in_specs=[_chunk_spec(tm), _chunk_spec(tm), tok, pl.BlockSpec((1, tm, D_MODEL), lambda b, i: (b, i, 0)),
                  col, sq, col, sq, row, row],
        out_specs=pl.BlockSpec((1, tm, D_MODEL), lambda b, i: (b, i, 0)),
        out_shape=jax.ShapeDtypeStruct((B, L, D_MODEL), F32),
        compiler_params=_params("parallel", "parallel"),
        name="s5_out",
    )(ys4, u4, sgT, x, w['d'], w['wgluT'], w['bglu'], w['wo'], ln_g, ln_b)


def _prep_mla(w_in, g_q, w_q_up, g_kv, w_kv_up, w_out):
    nkv = Q_LORA + KV_LORA
    wa = jnp.concatenate([w_in[:, :nkv + QK_ROPE_DIM],
                          jnp.zeros((D_MODEL, LANES - QK_ROPE_DIM), F32)], axis=1)
    wq = w_q_up.reshape(Q_LORA, N_HEADS, QK_NOPE_DIM + QK_ROPE_DIM)
    wq = jnp.pad(wq, ((0, 0), (0, 0), (0, HEAD_PAD - QK_NOPE_DIM - QK_ROPE_DIM)))
    wkv = w_kv_up.reshape(KV_LORA, N_HEADS, QK_NOPE_DIM + V_DIM)
    return dict(
        wa=wa.astype(BF16),
        gq=g_q.reshape(1, Q_LORA), gkv=g_kv.reshape(1, KV_LORA),
        wqT=wq.reshape(Q_LORA, N_HEADS * HEAD_PAD).T.astype(BF16),
        wkn=wkv[:, :, :QK_NOPE_DIM].reshape(KV_LORA, N_HEADS * QK_NOPE_DIM).astype(BF16),
        wvT=wkv[:, :, QK_NOPE_DIM:].reshape(KV_LORA, N_HEADS * V_DIM).T.astype(BF16),
        wgT=w_in[:, nkv + QK_ROPE_DIM:].T.astype(BF16),
        wo=w_out.astype(BF16),
    )


def _rope_tables(L):
    inv = ROPE_THETA ** (-jnp.arange(0, QK_ROPE_DIM, 2, dtype=F32) / QK_ROPE_DIM)
    ang = jnp.arange(L, dtype=F32)[:, None] * inv[None, :]
    cos, sin = jnp.cos(ang), jnp.sin(ang)
    z = jnp.zeros_like(cos)
    z2 = jnp.zeros((L, LANES - QK_ROPE_DIM), F32)
    return dict(
        cosT=cos.T, sinT=sin.T,
        cosk=jnp.concatenate([cos, cos, z2], axis=1),
        sina=jnp.concatenate([z, sin, z2], axis=1),
        sinb=jnp.concatenate([-sin, z, z2], axis=1),
    )


def _powers(lam_bar, n):
    pw = jnp.stack([jnp.ones_like(lam_bar), lam_bar], axis=-2)
    top = lam_bar
    while pw.shape[-2] - 1 < n:
        pw = jnp.concatenate([pw, pw[..., 1:, :] * top[..., None, :]], axis=-2)
        top = top * top
    return pw


def _reim(z, axis):
    return jnp.concatenate([jnp.real(z), jnp.imag(z)], axis=axis)


def _prep_s5(w_in, a_re, a_im, log_step, b_re, b_im, c_re, c_im, d, w_glu, b_glu, w_out):
    G, N, P, T = S5_GROUPS, S5_STATE, S5_GROUP, CHUNK
    lam = lax.complex(a_re, a_im)
    step = jnp.exp(log_step)[..., None]
    lam_bar = jnp.exp(lam * step)
    b_bar = ((lam_bar - 1.0) / lam)[..., None] * lax.complex(b_re, b_im)
    c = lax.complex(c_re, c_im)
    pw = _powers(lam_bar, T)

    cb = c[:, :, None, :, :] * jnp.swapaxes(b_bar, -1, -2)[:, :, :, None, :]
    cb = cb.reshape(2, G, P * P, N)
    cbT = jnp.concatenate([_reim(cb[0], -1), _reim(cb[1], -1)], axis=-1)
    pf = jnp.swapaxes(pw[0, :, :T], -1, -2)
    pb = jnp.swapaxes(pw[1, :, :T], -1, -2)[..., ::-1]
    zf = jnp.zeros((G, N, T), pf.dtype)
    pf = jnp.concatenate([zf, pf], axis=-1)
    pb = jnp.concatenate([zf[..., :1], pb, zf[..., :T - 1]], axis=-1)
    pwT = jnp.concatenate([jnp.real(pf), -jnp.imag(pf), jnp.real(pb), -jnp.imag(pb)], axis=1)

    sf = pw[0, :, T - 1::-1]
    sb = pw[1, :, :T]
    rs_f = jnp.swapaxes(b_bar[0], -1, -2)[:, :, None, :] * sf[:, None, :, :]
    rs_b = jnp.swapaxes(b_bar[1], -1, -2)[:, :, None, :] * sb[:, None, :, :]
    rs = jnp.concatenate([jnp.real(rs_f), jnp.real(rs_b), jnp.imag(rs_f), jnp.imag(rs_b)], axis=-1)
    rs = rs.reshape(G, P * T, 4 * N).astype(BF16)

    of = pw[0, :, 1:T + 1]
    ob = pw[1, :, T:0:-1]
    ri_f = c[0][:, :, None, :] * of[:, None, :, :]
    ri_b = c[1][:, :, None, :] * ob[:, None, :, :]
    ri = jnp.concatenate([jnp.real(ri_f), jnp.real(ri_b), -jnp.imag(ri_f), -jnp.imag(ri_b)], axis=-1)
    ri = jnp.swapaxes(ri.reshape(G, P * T, 4 * N), -1, -2).astype(BF16)

    a = pw[:, :, T]
    rows = [jnp.concatenate([jnp.real(a[0]), jnp.real(a[1])], -1),
            jnp.concatenate([jnp.imag(a[0]), jnp.imag(a[1])], -1)]
    ac = jnp.stack(rows + [jnp.zeros_like(rows[0])] * 6, axis=1)

    return dict(
        wuT=w_in[:, :D_MODEL].T.astype(BF16), wgT=w_in[:, D_MODEL:].T.astype(BF16),
        cbT=cbT, pwT=pwT, rs=rs, ri=ri, ac=ac,
        d=jnp.broadcast_to(d[:, None], (D_MODEL, LANES)),
        wgluT=w_glu.T.astype(BF16),
        bglu=jnp.broadcast_to(b_glu[:, None], (D_MODEL, LANES)),
        wo=w_out.astype(BF16),
    )


def _mla_layer(x, w, tabs, ln_g, ln_b, tm):
    qT, k, vT, sgT, kn2 = _mla_proj(x, w, tabs, tm)
    ogT = _attention(qT, k, vT, sgT, kn2, tm)
    return _mla_out(ogT, x, w['wo'], ln_g, ln_b, tm)


def _s5_layer(x, w, toep, ln_g, ln_b, tm):
    u4, sgT = _s5_in(x, w['wuT'], w['wgT'], tm)
    ys4 = _s5_scan(u4, toep, w['rs'], w['ri'], w['ac'])
    return _s5_out(ys4, u4, sgT, x, w, ln_g, ln_b, tm)


def _trunk(x, mla_w, s5_w, toeps, ln_g, ln_b):
    L = x.shape[1]
    tm = min(TOK_TILE, L)
    tabs = _rope_tables(L)
    for i in range(DEPTH):
        g, b = ln_g[i].reshape(1, D_MODEL), ln_b[i].reshape(1, D_MODEL)
        if i % 2 == 0:
            x = _mla_layer(x, mla_w[i // 2], tabs, g, b, tm)
        else:
            x = _s5_layer(x, s5_w[i // 2], toeps[i // 2], g, b, tm)
    return x


def kernel(x_prompt, x_sample, mla_w_in, mla_g_q, mla_w_q_up, mla_g_kv, mla_w_kv_up, mla_w_out,
           s5_w_in, s5_a_re, s5_a_im, s5_log_step, s5_b_re, s5_b_im, s5_c_re, s5_c_im, s5_d,
           s5_w_glu, s5_b_glu, s5_w_out, ln_g, ln_b):
    mla = (mla_w_in, mla_g_q, mla_w_q_up, mla_g_kv, mla_w_kv_up, mla_w_out)
    s5 = (s5_w_in, s5_a_re, s5_a_im, s5_log_step, s5_b_re, s5_b_im, s5_c_re, s5_c_im, s5_d,
          s5_w_glu, s5_b_glu, s5_w_out)
    mla_w = [_prep_mla(*[w[j] for w in mla]) for j in range(mla_w_in.shape[0])]
    s5_w = [_prep_s5(*[w[j] for w in s5]) for j in range(s5_w_in.shape[0])]
    toeps = [_toeplitz(w['cbT'], w['pwT']) for w in s5_w]
    y_prompt = _trunk(x_prompt, mla_w, s5_w, toeps, ln_g, ln_b)
    y_sample = _trunk(x_sample, mla_w, s5_w, toeps, ln_g, ln_b)
    return (y_prompt, y_sample)
```

```python
import functools
import math

import jax
import jax.numpy as jnp
from jax import lax
from jax.experimental import pallas as pl
from jax.experimental.pallas import tpu as pltpu

F32 = jnp.float32
BF16 = jnp.bfloat16

D_MODEL = 1024
DEPTH = 4
N_HEADS = 8
QK_NOPE_DIM = 128
QK_ROPE_DIM = 64
V_DIM = 128
Q_LORA = 384
KV_LORA = 256
ROPE_THETA = 10000.0
ATTN_SCALE = 1.0 / math.sqrt(QK_NOPE_DIM + QK_ROPE_DIM)
S5_GROUP = 16
S5_GROUPS = D_MODEL // S5_GROUP
S5_STATE = 64
ALPHA = (2 * DEPTH) ** 0.25
LN_EPS = 1e-5
RMS_EPS = 1e-6

LANES = 128
HEAD_PAD = 256
ROPE_HALF = QK_ROPE_DIM // 2
LOG2_E = math.log2(math.e)
CHUNK = 128
TOK_TILE = 512
ATTN_Q_TILES = 4
ATTN_REF_KEYS = 128
ATTN_FAST_MARGIN = 60.0
VMEM_LIMIT = 56 * 1024 * 1024

NT_DIMS = (((1,), (1,)), ((), ()))
TN_DIMS = (((0,), (0,)), ((), ()))


def _dot(a, b):
    return jnp.dot(a, b, preferred_element_type=F32)


def _dot_nt(a, b):
    return lax.dot_general(a, b, NT_DIMS, preferred_element_type=F32)


def _dot_tn(a, b):
    return lax.dot_general(a, b, TN_DIMS, preferred_element_type=F32)


def _params(*sem):
    return pltpu.CompilerParams(dimension_semantics=sem, vmem_limit_bytes=VMEM_LIMIT)


def _layer_norm(r, g, b):
    mu = jnp.mean(r, axis=-1, keepdims=True)
    d = r - mu
    var = jnp.mean(d * d, axis=-1, keepdims=True)
    return d * lax.rsqrt(var + LN_EPS) * g + b


def _rms_norm(x, g):
    return x * lax.rsqrt(jnp.mean(x * x, axis=-1, keepdims=True) + RMS_EPS) * g


def _sigmoid(x):
    return 0.5 * jnp.tanh(0.5 * x) + 0.5


def _silu(x):
    h = 0.5 * x
    return h * jnp.tanh(h) + h


def _mla_proj_kernel(x_ref, wa_ref, gq_ref, gkv_ref, wqT_ref, wkn_ref, wvT_ref, wgT_ref,
                     cosT_ref, sinT_ref, cosk_ref, sina_ref, sinb_ref,
                     qT_ref, k_ref, vT_ref, sgT_ref, kn2_ref):
    xb = x_ref[0].astype(BF16)
    ha = _dot(xb, wa_ref[...])
    cq = _rms_norm(ha[:, :Q_LORA], gq_ref[...]).astype(BF16)
    ckv = _rms_norm(ha[:, Q_LORA:Q_LORA + KV_LORA], gkv_ref[...]).astype(BF16)
    kr = ha[:, Q_LORA + KV_LORA:]

    qT = _dot_nt(wqT_ref[...], cq) * (ATTN_SCALE * LOG2_E)
    cosT = cosT_ref[...]
    sinT = sinT_ref[...]
    tm = qT.shape[1]
    zpad = jnp.zeros((HEAD_PAD - QK_NOPE_DIM - QK_ROPE_DIM, tm), F32)
    for h in range(N_HEADS):
        blk = qT[h * HEAD_PAD:(h + 1) * HEAD_PAD]
        t1 = blk[QK_NOPE_DIM:QK_NOPE_DIM + ROPE_HALF]
        t2 = blk[QK_NOPE_DIM + ROPE_HALF:QK_NOPE_DIM + QK_ROPE_DIM]
        out = jnp.concatenate(
            [blk[:QK_NOPE_DIM], t1 * cosT - t2 * sinT, t2 * cosT + t1 * sinT, zpad], axis=0)
        qT_ref[0, 0, h * HEAD_PAD:(h + 1) * HEAD_PAD, :] = out.astype(BF16)

    kn = _dot(ckv, wkn_ref[...])
    krr = (kr * cosk_ref[...]
           + pltpu.roll(kr, ROPE_HALF, 1) * sina_ref[...]
           + pltpu.roll(kr, LANES - ROPE_HALF, 1) * sinb_ref[...]).astype(BF16)
    krf = krr.astype(F32)
    kr2 = jnp.sum(krf * krf, axis=1, keepdims=True)
    kn2_rows = []
    for h in range(N_HEADS):
        kb = kn[:, h * QK_NOPE_DIM:(h + 1) * QK_NOPE_DIM].astype(BF16)
        k_ref[0, h, :, :QK_NOPE_DIM] = kb
        k_ref[0, h, :, QK_NOPE_DIM:] = krr
        kf = kb.astype(F32)
        n2 = jnp.max(jnp.sum(kf * kf, axis=1, keepdims=True) + kr2, axis=0, keepdims=True)
        kn2_rows.append(jnp.broadcast_to(n2, (1, LANES)))
    kn2_ref[0, 0] = jnp.concatenate(kn2_rows, axis=0)

    vT_ref[0, 0] = _dot_nt(wvT_ref[...], ckv).astype(BF16)
    sgT_ref[0, 0] = _silu(_dot_nt(wgT_ref[...], xb)).astype(BF16)


def _mla_proj(x, w, tabs, tm):
    B, L, _ = x.shape
    nl = L // tm
    const = lambda shape: pl.BlockSpec(shape, lambda b, i: (0,) * len(shape))
    return pl.pallas_call(
        _mla_proj_kernel,
        grid=(B, nl),
        in_specs=[
            pl.BlockSpec((1, tm, D_MODEL), lambda b, i: (b, i, 0)),
            const(w['wa'].shape), const(w['gq'].shape), const(w['gkv'].shape),
            const(w['wqT'].shape), const(w['wkn'].shape), const(w['wvT'].shape), const(w['wgT'].shape),
            pl.BlockSpec((ROPE_HALF, tm), lambda b, i: (0, i)),
            pl.BlockSpec((ROPE_HALF, tm), lambda b, i: (0, i)),
            pl.BlockSpec((tm, LANES), lambda b, i: (i, 0)),
            pl.BlockSpec((tm, LANES), lambda b, i: (i, 0)),
            pl.BlockSpec((tm, LANES), lambda b, i: (i, 0)),
        ],
        out_specs=[
            pl.BlockSpec((1, 1, N_HEADS * HEAD_PAD, tm), lambda b, i: (b, i, 0, 0)),
            pl.BlockSpec((1, N_HEADS, tm, HEAD_PAD), lambda b, i: (b, 0, i, 0)),
            pl.BlockSpec((1, 1, N_HEADS * V_DIM, tm), lambda b, i: (b, i, 0, 0)),
            pl.BlockSpec((1, 1, N_HEADS * V_DIM, tm), lambda b, i: (b, i, 0, 0)),
            pl.BlockSpec((1, 1, N_HEADS, LANES), lambda b, i: (b, i, 0, 0)),
        ],
        out_shape=[
            jax.ShapeDtypeStruct((B, nl, N_HEADS * HEAD_PAD, tm), BF16),
            jax.ShapeDtypeStruct((B, N_HEADS, L, HEAD_PAD), BF16),
            jax.ShapeDtypeStruct((B, nl, N_HEADS * V_DIM, tm), BF16),
            jax.ShapeDtypeStruct((B, nl, N_HEADS * V_DIM, tm), BF16),
            jax.ShapeDtypeStruct((B, nl, N_HEADS, LANES), F32),
        ],
        compiler_params=_params("parallel", "parallel"),
        name="mla_proj",
    )(x, w['wa'], w['gq'], w['gkv'], w['wqT'], w['wkn'], w['wvT'], w['wgT'],
      tabs['cosT'], tabs['sinT'], tabs['cosk'], tabs['sina'], tabs['sinb'])


def _attn_kernel(qT_ref, k_ref, vT_ref, sgT_ref, kn2_ref, o_ref, m_sc, l_sc, acc_sc, *, tk, nk, nq):
    tq = qT_ref.shape[3]
    qTs = [qT_ref[0, t] for t in range(nq)]

    def k_tile(j):
        return k_ref[0, 0, pl.ds(pl.multiple_of(j * tk, tk), tk), :]

    k_pre = k_ref[0, 0, 0:ATTN_REF_KEYS, :]
    q2 = []
    for t in range(nq):
        cols = slice(t * tq, (t + 1) * tq)
        m_sc[:, cols] = jnp.max(_dot(k_pre, qTs[t]), axis=0, keepdims=True)
        qf = qTs[t].astype(F32)
        q2.append(jnp.sum(qf * qf, axis=0, keepdims=True))
    acc_sc[...] = jnp.zeros(acc_sc.shape, F32)
    l_sc[...] = jnp.zeros(l_sc.shape, F32)

    kn2 = jnp.max(kn2_ref[0, :, pl.ds(pl.program_id(1), 1), :])
    bound = jnp.sqrt(jnp.concatenate(q2, axis=1) * kn2)
    fast_ok = jnp.max(bound - m_sc[...]) <= ATTN_FAST_MARGIN

    @pl.when(fast_ok)
    def _fixed_reference():
        ms = [m_sc[:, t * tq:(t + 1) * tq] for t in range(nq)]

        def body(j, carry):
            ks = k_tile(j)
            ps = []
            for t in range(nq):
                cols = slice(t * tq, (t + 1) * tq)
                p = jnp.exp2(_dot(ks, qTs[t]) - ms[t])
                l_sc[:, cols] = l_sc[:, cols] + jnp.sum(p, axis=0, keepdims=True)
                ps.append(p.astype(BF16))
            for t in range(nq):
                cols = slice(t * tq, (t + 1) * tq)
                acc_sc[:, cols] = acc_sc[:, cols] + _dot(vT_ref[0, j], ps[t])
            return carry

        lax.fori_loop(0, nk, body, 0, unroll=8)

    @pl.when(jnp.logical_not(fast_ok))
    def _online():
        qT = jnp.concatenate(qTs, axis=1)

        def body(j, carry):
            sT = _dot(k_tile(j), qT)
            m_prev = m_sc[...]
            m_new = jnp.maximum(m_prev, jnp.max(sT, axis=0, keepdims=True))
            alpha = jnp.exp2(m_prev - m_new)
            p = jnp.exp2(sT - m_new)
            l_sc[...] = l_sc[...] * alpha + jnp.sum(p, axis=0, keepdims=True)
            acc_sc[...] = acc_sc[...] * alpha + _dot(vT_ref[0, j], p.astype(BF16))
            m_sc[...] = m_new
            return carry

        lax.fori_loop(0, nk, body, 0)

    for t in range(nq):
        cols = slice(t * tq, (t + 1) * tq)
        o = acc_sc[:, cols] / l_sc[:, cols] * sgT_ref[0, t].astype(F32)
        o_ref[0, t] = o.astype(BF16)


def _attention(qT, k, vT, sgT, kn2, tm):
    B, nl, _, _ = qT.shape
    L = nl * tm
    nq = ATTN_Q_TILES
    kern = functools.partial(_attn_kernel, tk=tm, nk=nl, nq=nq)
    return pl.pallas_call(
        kern,
        grid=(B, N_HEADS, nl // nq),
        in_specs=[
            pl.BlockSpec((1, nq, HEAD_PAD, tm), lambda b, h, i: (b, i, h, 0)),
            pl.BlockSpec((1, 1, L, HEAD_PAD), lambda b, h, i: (b, h, 0, 0)),
            pl.BlockSpec((1, nl, V_DIM, tm), lambda b, h, i: (b, 0, h, 0)),
            pl.BlockSpec((1, nq, V_DIM, tm), lambda b, h, i: (b, i, h, 0)),
            pl.BlockSpec((1, nl, N_HEADS, LANES), lambda b, h, i: (b, 0, 0, 0)),
        ],
        out_specs=pl.BlockSpec((1, nq, V_DIM, tm), lambda b, h, i: (b, i, h, 0)),
        out_shape=jax.ShapeDtypeStruct((B, nl, N_HEADS * V_DIM, tm), BF16),
        scratch_shapes=[pltpu.VMEM((1, nq * tm), F32), pltpu.VMEM((1, nq * tm), F32),
                        pltpu.VMEM((V_DIM, nq * tm), F32)],
        compiler_params=_params("parallel", "parallel", "arbitrary"),
        name="mla_attn",
    )(qT, k, vT, sgT, kn2)


def _out_kernel(zT_ref, x_ref, wo_ref, g_ref, b_ref, o_ref):
    half = zT_ref.shape[3] // 2
    ys = [_dot_tn(zT_ref[0, 0, :, i * half:(i + 1) * half], wo_ref[...]) for i in range(2)]
    for i in range(2):
        rows = slice(i * half, (i + 1) * half)
        o_ref[0, rows] = _layer_norm(ALPHA * x_ref[0, rows] + ys[i], g_ref[...], b_ref[...])


def _mla_out(ogT, x, woT, ln_g, ln_b, tm):
    B, L, _ = x.shape
    nl = L // tm
    return pl.pallas_call(
        _out_kernel,
        grid=(B, nl),
        in_specs=[
            pl.BlockSpec((1, 1, D_MODEL, tm), lambda b, i: (b, i, 0, 0)),
            pl.BlockSpec((1, tm, D_MODEL), lambda b, i: (b, i, 0)),
            pl.BlockSpec((D_MODEL, D_MODEL), lambda b, i: (0, 0)),
            pl.BlockSpec((1, D_MODEL), lambda b, i: (0, 0)),
            pl.BlockSpec((1, D_MODEL), lambda b, i: (0, 0)),
        ],
        out_specs=pl.BlockSpec((1, tm, D_MODEL), lambda b, i: (b, i, 0)),
        out_shape=jax.ShapeDtypeStruct((B, L, D_MODEL), F32),
        compiler_params=_params("parallel", "parallel"),
        name="mla_out",
    )(ogT, x, woT, ln_g, ln_b)


def _s5_in_kernel(x_ref, wuT_ref, wgT_ref, u_ref, sgT_ref):
    xb = x_ref[0].astype(BF16)
    uT = _dot_nt(wuT_ref[...], xb)
    for c in range(u_ref.shape[0]):
        u_ref[c, 0] = uT[:, c * CHUNK:(c + 1) * CHUNK]
    sgT_ref[...] = _silu(_dot_nt(wgT_ref[...], xb)).astype(BF16)


def _chunk_spec(tm):
    return pl.BlockSpec((tm // CHUNK, 1, D_MODEL, CHUNK), lambda b, i: (i, b, 0, 0))


def _s5_in(x, wuT, wgT, tm):
    B, L, _ = x.shape
    nl = L // tm
    tok = pl.BlockSpec((D_MODEL, tm), lambda b, i: (0, b * nl + i))
    wspec = pl.BlockSpec((D_MODEL, D_MODEL), lambda b, i: (0, 0))
    return pl.pallas_call(
        _s5_in_kernel,
        grid=(B, nl),
        in_specs=[pl.BlockSpec((1, tm, D_MODEL), lambda b, i: (b, i, 0)), wspec, wspec],
        out_specs=[_chunk_spec(tm), tok],
        out_shape=[jax.ShapeDtypeStruct((L // CHUNK, B, D_MODEL, CHUNK), F32),
                   jax.ShapeDtypeStruct((D_MODEL, B * L), BF16)],
        compiler_params=_params("parallel", "parallel"),
        name="s5_in",
    )(x, wuT, wgT)


def _toep_kernel(cbT_ref, pwT_ref, o_ref, kv_sc):
    kv_sc[...] = jnp.dot(cbT_ref[0], pwT_ref[0], preferred_element_type=F32,
                         precision=lax.Precision.HIGHEST)

    def body(pi, carry):
        row0 = pl.multiple_of(pi * CHUNK, CHUNK)
        for po in range(S5_GROUP):
            kv = kv_sc[pl.ds(pi * S5_GROUP + po, 1), :]
            rolled = pltpu.roll(jnp.broadcast_to(kv, (CHUNK, 2 * CHUNK)), 0, 1, stride=1, stride_axis=0)
            o_ref[0, pl.ds(row0, CHUNK), po * CHUNK:(po + 1) * CHUNK] = rolled[:, CHUNK:].astype(BF16)
        return carry

    lax.fori_loop(0, S5_GROUP, body, 0)


def _toeplitz(cbT, pwT):
    G = cbT.shape[0]
    n = S5_GROUP * CHUNK
    blk = pl.BlockSpec((1, 2 * CHUNK, 2 * CHUNK), lambda g: (g, 0, 0))
    return pl.pallas_call(
        _toep_kernel,
        grid=(G,),
        in_specs=[blk, blk],
        out_specs=pl.BlockSpec((1, n, n), lambda g: (g, 0, 0)),
        out_shape=jax.ShapeDtypeStruct((G, n, n), BF16),
        scratch_shapes=[pltpu.VMEM((2 * CHUNK, 2 * CHUNK), F32)],
        compiler_params=_params("parallel"),
        name="s5_toeplitz",
    )(cbT, pwT)


def _sublane_transpose8(xs):
    xs = list(xs)
    sub = lax.broadcasted_iota(jnp.int32, xs[0].shape, 1)
    for k in (4, 2, 1):
        keep = (sub & k) == 0
        for i in range(8):
            if i & k:
                continue
            lo, hi = xs[i], xs[i + k]
            xs[i] = jnp.where(keep, lo, pltpu.roll(hi, k, 1))
            xs[i + k] = jnp.where(keep, pltpu.roll(lo, 8 - k, 1), hi)
    return xs


def _scan_kernel(u_ref, tm_ref, rs_ref, ri_ref, ac_ref, dg_ref, y_ref,
                 sre_sc, sim_sc, hfre_sc, hfim_sc, hbre_sc, hbim_sc, *, nseq, nchunk):
    R = u_ref.shape[0]
    cols = []
    for half in range(S5_GROUP // 8):
        ch = slice(8 * half, 8 * half + 8)
        xs = [u_ref[pl.ds(i, R // 8, stride=8), ch, :] for i in range(8)]
        cols += [x.reshape(R, CHUNK) for x in _sublane_transpose8(xs)]
    u = jnp.concatenate(cols, axis=1)
    lhs = u.astype(BF16)
    s = _dot(lhs, rs_ref[0])
    y = _dot(lhs, tm_ref[0])
    tile = 2 * S5_STATE
    sre_sc[...] = s[:, :tile]
    sim_sc[...] = s[:, tile:]
    ac = ac_ref[0]
    are, aim = ac[0:1], ac[1:2]
    is_fwd = lax.broadcasted_iota(jnp.int32, (nseq, tile), 1) < S5_STATE

    ere = jnp.zeros((nseq, tile), F32)
    eim = jnp.zeros((nseq, tile), F32)
    for c in range(nchunk):
        rf = pl.ds(c * nseq, nseq)
        rb = pl.ds((nchunk - 1 - c) * nseq, nseq)
        hfre_sc[rf, :] = ere
        hfim_sc[rf, :] = eim
        hbre_sc[rb, :] = ere
        hbim_sc[rb, :] = eim
        s_re = jnp.where(is_fwd, sre_sc[rf, :], sre_sc[rb, :])
        s_im = jnp.where(is_fwd, sim_sc[rf, :], sim_sc[rb, :])
        ere, eim = are * ere - aim * eim + s_re, are * eim + aim * ere + s_im

    fwd_rows = lax.broadcasted_iota(jnp.int32, hfre_sc.shape, 1) < S5_STATE
    h = jnp.concatenate([jnp.where(fwd_rows, hfre_sc[...], hbre_sc[...]),
                         jnp.where(fwd_rows, hfim_sc[...], hbim_sc[...])], axis=1).astype(BF16)
    y = y + _dot(h, ri_ref[0]) + dg_ref[0] * u
    for half in range(S5_GROUP // 8):
        ys = [y[:, (8 * half + p) * CHUNK:(8 * half + p + 1) * CHUNK].reshape(R // 8, 8, CHUNK) for p in range(8)]
        for i, x in enumerate(_sublane_transpose8(ys)):
            y_ref[pl.ds(i, R // 8, stride=8), 8 * half:8 * half + 8, :] = x


def _s5_scan(u4, toep, rs, ri, ac, dg):
    nchunk, nseq, W, _ = u4.shape
    R = nchunk * nseq
    n = S5_GROUP * CHUNK
    kern = functools.partial(_scan_kernel, nseq=nseq, nchunk=nchunk)
    y3 = pl.pallas_call(
        kern,
        grid=(S5_GROUPS,),
        in_specs=[
            pl.BlockSpec((R, S5_GROUP, CHUNK), lambda g: (0, g, 0)),
            pl.BlockSpec((1, n, n), lambda g: (g, 0, 0)),
            pl.BlockSpec((1, n, 4 * S5_STATE), lambda g: (g, 0, 0)),
            pl.BlockSpec((1, 4 * S5_STATE, n), lambda g: (g, 0, 0)),
            pl.BlockSpec((1, 8, 2 * S5_STATE), lambda g: (g, 0, 0)),
            pl.BlockSpec((1, 1, n), lambda g: (g, 0, 0)),
        ],
        out_specs=pl.BlockSpec((R, S5_GROUP, CHUNK), lambda g: (0, g, 0)),
        out_shape=jax.ShapeDtypeStruct((R, W, CHUNK), F32),
        scratch_shapes=[pltpu.VMEM((R, 2 * S5_STATE), F32)] * 6,
        compiler_params=_params("parallel"),
        name="s5_scan",
    )(u4.reshape(R, W, CHUNK), toep, rs, ri, ac, dg)
    return y3.reshape(nchunk, nseq, W, CHUNK)


def _s5_out_kernel(ys_ref, sg_ref, x_ref, wgluT_ref, bglu_ref, wo_ref, g_ref, b_ref, o_ref):
    rep = ys_ref.shape[0] // 2
    half = rep * CHUNK
    bglu = jnp.tile(bglu_ref[...], (1, rep))
    y = [jax.nn.gelu(jnp.concatenate([ys_ref[i * rep + c, 0] for c in range(rep)], axis=1))
         for i in range(2)]
    z = [_dot(wgluT_ref[...], y[i].astype(BF16)) + bglu for i in range(2)]
    v = [(y[i] * _sigmoid(z[i]) * sg_ref[:, i * half:(i + 1) * half].astype(F32)).astype(BF16) for i in range(2)]
    o = [_dot_tn(v[i], wo_ref[...]) for i in range(2)]
    for i in range(2):
        rows = slice(i * half, (i + 1) * half)
        o_ref[0, rows] = _layer_norm(ALPHA * x_ref[0, rows] + o[i], g_ref[...], b_ref[...])


def _s5_out(ys4, sgT, x, w, ln_g, ln_b, tm):
    B, L, _ = x.shape
    nl = L // tm
    tok = pl.BlockSpec((D_MODEL, tm), lambda b, i: (0, b * nl + i))
    sq = pl.BlockSpec((D_MODEL, D_MODEL), lambda b, i: (0, 0))
    col = pl.BlockSpec((D_MODEL, LANES), lambda b, i: (0, 0))
    row = pl.BlockSpec((1, D_MODEL), lambda b, i: (0, 0))
    return pl.pallas_call(
        _s5_out_kernel,
        grid=(B, nl),
        in_specs=[_chunk_spec(tm), tok, pl.BlockSpec((1, tm, D_MODEL), lambda b, i: (b, i, 0)),
                  sq, col, sq, row, row],
        out_specs=pl.BlockSpec((1, tm, D_MODEL), lambda b, i: (b, i, 0)),
        out_shape=jax.ShapeDtypeStruct((B, L, D_MODEL), F32),
        compiler_params=_params("parallel", "parallel"),
        name="s5_out",
    )(ys4, sgT, x, w['wgluT'], w['bglu'], w['wo'], ln_g, ln_b)


def _prep_mla(w_in, g_q, w_q_up, g_kv, w_kv_up, w_out):
    nkv = Q_LORA + KV_LORA
    wa = jnp.concatenate([w_in[:, :nkv + QK_ROPE_DIM],
                          jnp.zeros((D_MODEL, LANES - QK_ROPE_DIM), F32)], axis=1)
    wq = w_q_up.reshape(Q_LORA, N_HEADS, QK_NOPE_DIM + QK_ROPE_DIM)
    wq = jnp.pad(wq, ((0, 0), (0, 0), (0, HEAD_PAD - QK_NOPE_DIM - QK_ROPE_DIM)))
    wkv = w_kv_up.reshape(KV_LORA, N_HEADS, QK_NOPE_DIM + V_DIM)
    return dict(
        wa=wa.astype(BF16),
        gq=g_q.reshape(1, Q_LORA), gkv=g_kv.reshape(1, KV_LORA),
        wqT=wq.reshape(Q_LORA, N_HEADS * HEAD_PAD).T.astype(BF16),
        wkn=wkv[:, :, :QK_NOPE_DIM].reshape(KV_LORA, N_HEADS * QK_NOPE_DIM).astype(BF16),
        wvT=wkv[:, :, QK_NOPE_DIM:].reshape(KV_LORA, N_HEADS * V_DIM).T.astype(BF16),
        wgT=w_in[:, nkv + QK_ROPE_DIM:].T.astype(BF16),
        wo=w_out.astype(BF16),
    )


def _rope_tables(L):
    inv = ROPE_THETA ** (-jnp.arange(0, QK_ROPE_DIM, 2, dtype=F32) / QK_ROPE_DIM)
    ang = jnp.arange(L, dtype=F32)[:, None] * inv[None, :]
    cos, sin = jnp.cos(ang), jnp.sin(ang)
    z = jnp.zeros_like(cos)
    z2 = jnp.zeros((L, LANES - QK_ROPE_DIM), F32)
    return dict(
        cosT=cos.T, sinT=sin.T,
        cosk=jnp.concatenate([cos, cos, z2], axis=1),
        sina=jnp.concatenate([z, sin, z2], axis=1),
        sinb=jnp.concatenate([-sin, z, z2], axis=1),
    )


def _powers(lam_bar, n):
    pw = jnp.stack([jnp.ones_like(lam_bar), lam_bar], axis=-2)
    top = lam_bar
    while pw.shape[-2] - 1 < n:
        pw = jnp.concatenate([pw, pw[..., 1:, :] * top[..., None, :]], axis=-2)
        top = top * top
    return pw


def _reim(z, axis):
    return jnp.concatenate([jnp.real(z), jnp.imag(z)], axis=axis)


def _prep_s5(w_in, a_re, a_im, log_step, b_re, b_im, c_re, c_im, d, w_glu, b_glu, w_out):
    G, N, P, T = S5_GROUPS, S5_STATE, S5_GROUP, CHUNK
    lam = lax.complex(a_re, a_im)
    step = jnp.exp(log_step)[..., None]
    lam_bar = jnp.exp(lam * step)
    b_bar = ((lam_bar - 1.0) / lam)[..., None] * lax.complex(b_re, b_im)
    c = lax.complex(c_re, c_im)
    pw = _powers(lam_bar, T)

    cb = c[:, :, None, :, :] * jnp.swapaxes(b_bar, -1, -2)[:, :, :, None, :]
    cb = cb.reshape(2, G, P * P, N)
    cbT = jnp.concatenate([_reim(cb[0], -1), _reim(cb[1], -1)], axis=-1)
    pf = jnp.swapaxes(pw[0, :, :T], -1, -2)
    pb = jnp.swapaxes(pw[1, :, :T], -1, -2)[..., ::-1]
    zf = jnp.zeros((G, N, T), pf.dtype)
    pf = jnp.concatenate([zf, pf], axis=-1)
    pb = jnp.concatenate([zf[..., :1], pb, zf[..., :T - 1]], axis=-1)
    pwT = jnp.concatenate([jnp.real(pf), -jnp.imag(pf), jnp.real(pb), -jnp.imag(pb)], axis=1)

    sf = pw[0, :, T - 1::-1]
    sb = pw[1, :, :T]
    rs_f = jnp.swapaxes(b_bar[0], -1, -2)[:, :, None, :] * sf[:, None, :, :]
    rs_b = jnp.swapaxes(b_bar[1], -1, -2)[:, :, None, :] * sb[:, None, :, :]
    rs = jnp.concatenate([jnp.real(rs_f), jnp.real(rs_b), jnp.imag(rs_f), jnp.imag(rs_b)], axis=-1)
    rs = rs.reshape(G, P * T, 4 * N).astype(BF16)

    of = pw[0, :, 1:T + 1]
    ob = pw[1, :, T:0:-1]
    ri_f = c[0][:, :, None, :] * of[:, None, :, :]
    ri_b = c[1][:, :, None, :] * ob[:, None, :, :]
    ri = jnp.concatenate([jnp.real(ri_f), jnp.real(ri_b), -jnp.imag(ri_f), -jnp.imag(ri_b)], axis=-1)
    ri = jnp.swapaxes(ri.reshape(G, P * T, 4 * N), -1, -2).astype(BF16)

    a = pw[:, :, T]
    rows = [jnp.concatenate([jnp.real(a[0]), jnp.real(a[1])], -1),
            jnp.concatenate([jnp.imag(a[0]), jnp.imag(a[1])], -1)]
    ac = jnp.stack(rows + [jnp.zeros_like(rows[0])] * 6, axis=1)

    return dict(
        wuT=w_in[:, :D_MODEL].T.astype(BF16), wgT=w_in[:, D_MODEL:].T.astype(BF16),
        cbT=cbT, pwT=pwT, rs=rs, ri=ri, ac=ac,
        dg=jnp.repeat(d.reshape(G, 1, P), T, axis=-1),
        wgluT=w_glu.T.astype(BF16),
        bglu=jnp.broadcast_to(b_glu[:, None], (D_MODEL, LANES)),
        wo=w_out.astype(BF16),
    )


def _mla_layer(x, w, tabs, ln_g, ln_b, tm):
    qT, k, vT, sgT, kn2 = _mla_proj(x, w, tabs, tm)
    ogT = _attention(qT, k, vT, sgT, kn2, tm)
    return _mla_out(ogT, x, w['wo'], ln_g, ln_b, tm)


def _s5_layer(x, w, toep, ln_g, ln_b, tm):
    u4, sgT = _s5_in(x, w['wuT'], w['wgT'], tm)
    ys4 = _s5_scan(u4, toep, w['rs'], w['ri'], w['ac'], w['dg'])
    return _s5_out(ys4, sgT, x, w, ln_g, ln_b, tm)


def _trunk(x, mla_w, s5_w, toeps, ln_g, ln_b):
    L = x.shape[1]
    tm = min(TOK_TILE, L)
    tabs = _rope_tables(L)
    for i in range(DEPTH):
        g, b = ln_g[i].reshape(1, D_MODEL), ln_b[i].reshape(1, D_MODEL)
        if i % 2 == 0:
            x = _mla_layer(x, mla_w[i // 2], tabs, g, b, tm)
        else:
            x = _s5_layer(x, s5_w[i // 2], toeps[i // 2], g, b, tm)
    return x


def kernel(x_prompt, x_sample, mla_w_in, mla_g_q, mla_w_q_up, mla_g_kv, mla_w_kv_up, mla_w_out,
           s5_w_in, s5_a_re, s5_a_im, s5_log_step, s5_b_re, s5_b_im, s5_c_re, s5_c_im, s5_d,
           s5_w_glu, s5_b_glu, s5_w_out, ln_g, ln_b):
    mla = (mla_w_in, mla_g_q, mla_w_q_up, mla_g_kv, mla_w_kv_up, mla_w_out)
    s5 = (s5_w_in, s5_a_re, s5_a_im, s5_log_step, s5_b_re, s5_b_im, s5_c_re, s5_c_im, s5_d,
          s5_w_glu, s5_b_glu, s5_w_out)
    mla_w = [_prep_mla(*[w[j] for w in mla]) for j in range(mla_w_in.shape[0])]
    s5_w = [_prep_s5(*[w[j] for w in s5]) for j in range(s5_w_in.shape[0])]
    toeps = [_toeplitz(w['cbT'], w['pwT']) for w in s5_w]
    y_prompt = _trunk(x_prompt, mla_w, s5_w, toeps, ln_g, ln_b)
    y_sample = _trunk(x_sample, mla_w, s5_w, toeps, ln_g, ln_b)
    return (y_prompt, y_sample)
```

```python
import functools
import math

import jax
import jax.numpy as jnp
from jax import lax
from jax.experimental import pallas as pl
from jax.experimental.pallas import tpu as pltpu

F32 = jnp.float32
BF16 = jnp.bfloat16

D_MODEL = 1024
DEPTH = 4
N_HEADS = 8
QK_NOPE_DIM = 128
QK_ROPE_DIM = 64
V_DIM = 128
Q_LORA = 384
KV_LORA = 256
ROPE_THETA = 10000.0
ATTN_SCALE = 1.0 / math.sqrt(QK_NOPE_DIM + QK_ROPE_DIM)
S5_GROUP = 16
S5_GROUPS = D_MODEL // S5_GROUP
S5_STATE = 64
ALPHA = (2 * DEPTH) ** 0.25
LN_EPS = 1e-5
RMS_EPS = 1e-6

LANES = 128
HEAD_PAD = 256
ROPE_HALF = QK_ROPE_DIM // 2
LOG2_E = math.log2(math.e)
CHUNK = 128
TOK_TILE = 512
ATTN_Q_TILES = 8
ATTN_REF_KEYS = 128
ATTN_FAST_MARGIN = 60.0
VMEM_LIMIT = 56 * 1024 * 1024

NT_DIMS = (((1,), (1,)), ((), ()))
TN_DIMS = (((0,), (0,)), ((), ()))


def _dot(a, b):
    return jnp.dot(a, b, preferred_element_type=F32)


def _dot_nt(a, b):
    return lax.dot_general(a, b, NT_DIMS, preferred_element_type=F32)


def _dot_tn(a, b):
    return lax.dot_general(a, b, TN_DIMS, preferred_element_type=F32)


def _params(*sem):
    return pltpu.CompilerParams(dimension_semantics=sem, vmem_limit_bytes=VMEM_LIMIT)


def _layer_norm(r, g, b):
    mu = jnp.mean(r, axis=-1, keepdims=True)
    d = r - mu
    var = jnp.mean(d * d, axis=-1, keepdims=True)
    return d * lax.rsqrt(var + LN_EPS) * g + b


def _rms_norm(x, g):
    return x * lax.rsqrt(jnp.mean(x * x, axis=-1, keepdims=True) + RMS_EPS) * g


def _sigmoid(x):
    return 0.5 * jnp.tanh(0.5 * x) + 0.5


def _silu(x):
    h = 0.5 * x
    return h * jnp.tanh(h) + h


def _mla_proj_kernel(x_ref, wa_ref, gq_ref, gkv_ref, wqT_ref, wkn_ref, wvT_ref, wgT_ref,
                     cosT_ref, sinT_ref, cosk_ref, sina_ref, sinb_ref,
                     qT_ref, k_ref, vT_ref, sgT_ref, kn2_ref):
    xb = x_ref[0].astype(BF16)
    ha = _dot(xb, wa_ref[...])
    cq = _rms_norm(ha[:, :Q_LORA], gq_ref[...]).astype(BF16)
    ckv = _rms_norm(ha[:, Q_LORA:Q_LORA + KV_LORA], gkv_ref[...]).astype(BF16)
    kr = ha[:, Q_LORA + KV_LORA:]

    qT = _dot_nt(wqT_ref[...], cq) * (ATTN_SCALE * LOG2_E)
    cosT = cosT_ref[...]
    sinT = sinT_ref[...]
    tm = qT.shape[1]
    zpad = jnp.zeros((HEAD_PAD - QK_NOPE_DIM - QK_ROPE_DIM, tm), F32)
    for h in range(N_HEADS):
        blk = qT[h * HEAD_PAD:(h + 1) * HEAD_PAD]
        t1 = blk[QK_NOPE_DIM:QK_NOPE_DIM + ROPE_HALF]
        t2 = blk[QK_NOPE_DIM + ROPE_HALF:QK_NOPE_DIM + QK_ROPE_DIM]
        out = jnp.concatenate(
            [blk[:QK_NOPE_DIM], t1 * cosT - t2 * sinT, t2 * cosT + t1 * sinT, zpad], axis=0)
        qT_ref[0, 0, h * HEAD_PAD:(h + 1) * HEAD_PAD, :] = out.astype(BF16)

    kn = _dot(ckv, wkn_ref[...])
    krr = (kr * cosk_ref[...]
           + pltpu.roll(kr, ROPE_HALF, 1) * sina_ref[...]
           + pltpu.roll(kr, LANES - ROPE_HALF, 1) * sinb_ref[...]).astype(BF16)
    krf = krr.astype(F32)
    kr2 = jnp.sum(krf * krf, axis=1, keepdims=True)
    kn2_rows = []
    for h in range(N_HEADS):
        kb = kn[:, h * QK_NOPE_DIM:(h + 1) * QK_NOPE_DIM].astype(BF16)
        k_ref[0, h, :, :QK_NOPE_DIM] = kb
        k_ref[0, h, :, QK_NOPE_DIM:] = krr
        kf = kb.astype(F32)
        n2 = jnp.max(jnp.sum(kf * kf, axis=1, keepdims=True) + kr2, axis=0, keepdims=True)
        kn2_rows.append(jnp.broadcast_to(n2, (1, LANES)))
    kn2_ref[0, 0] = jnp.concatenate(kn2_rows, axis=0)

    vT_ref[0, 0] = _dot_nt(wvT_ref[...], ckv).astype(BF16)
    sgT_ref[0, 0] = _silu(_dot_nt(wgT_ref[...], xb)).astype(BF16)


def _mla_proj(x, w, tabs, tm):
    B, L, _ = x.shape
    nl = L // tm
    const = lambda shape: pl.BlockSpec(shape, lambda b, i: (0,) * len(shape))
    return pl.pallas_call(
        _mla_proj_kernel,
        grid=(B, nl),
        in_specs=[
            pl.BlockSpec((1, tm, D_MODEL), lambda b, i: (b, i, 0)),
            const(w['wa'].shape), const(w['gq'].shape), const(w['gkv'].shape),
            const(w['wqT'].shape), const(w['wkn'].shape), const(w['wvT'].shape), const(w['wgT'].shape),
            pl.BlockSpec((ROPE_HALF, tm), lambda b, i: (0, i)),
            pl.BlockSpec((ROPE_HALF, tm), lambda b, i: (0, i)),
            pl.BlockSpec((tm, LANES), lambda b, i: (i, 0)),
            pl.BlockSpec((tm, LANES), lambda b, i: (i, 0)),
            pl.BlockSpec((tm, LANES), lambda b, i: (i, 0)),
        ],
        out_specs=[
            pl.BlockSpec((1, 1, N_HEADS * HEAD_PAD, tm), lambda b, i: (b, i, 0, 0)),
            pl.BlockSpec((1, N_HEADS, tm, HEAD_PAD), lambda b, i: (b, 0, i, 0)),
            pl.BlockSpec((1, 1, N_HEADS * V_DIM, tm), lambda b, i: (b, i, 0, 0)),
            pl.BlockSpec((1, 1, N_HEADS * V_DIM, tm), lambda b, i: (b, i, 0, 0)),
            pl.BlockSpec((1, 1, N_HEADS, LANES), lambda b, i: (b, i, 0, 0)),
        ],
        out_shape=[
            jax.ShapeDtypeStruct((B, nl, N_HEADS * HEAD_PAD, tm), BF16),
            jax.ShapeDtypeStruct((B, N_HEADS, L, HEAD_PAD), BF16),
            jax.ShapeDtypeStruct((B, nl, N_HEADS * V_DIM, tm), BF16),
            jax.ShapeDtypeStruct((B, nl, N_HEADS * V_DIM, tm), BF16),
            jax.ShapeDtypeStruct((B, nl, N_HEADS, LANES), F32),
        ],
        compiler_params=_params("parallel", "parallel"),
        name="mla_proj",
    )(x, w['wa'], w['gq'], w['gkv'], w['wqT'], w['wkn'], w['wvT'], w['wgT'],
      tabs['cosT'], tabs['sinT'], tabs['cosk'], tabs['sina'], tabs['sinb'])


def _attn_kernel(qT_ref, k_ref, vT_ref, sgT_ref, kn2_ref, o_ref, m_sc, l_sc, acc_sc, *, tk, nk, nq):
    tq = qT_ref.shape[3]
    qTs = [qT_ref[0, t] for t in range(nq)]

    def k_tile(j):
        return k_ref[0, 0, pl.ds(pl.multiple_of(j * tk, tk), tk), :]

    k_pre = k_ref[0, 0, 0:ATTN_REF_KEYS, :]
    q2 = []
    for t in range(nq):
        cols = slice(t * tq, (t + 1) * tq)
        m_sc[:, cols] = jnp.max(_dot(k_pre, qTs[t]), axis=0, keepdims=True)
        qf = qTs[t].astype(F32)
        q2.append(jnp.sum(qf * qf, axis=0, keepdims=True))
    acc_sc[...] = jnp.zeros(acc_sc.shape, F32)
    l_sc[...] = jnp.zeros(l_sc.shape, F32)

    kn2 = jnp.max(kn2_ref[0, :, pl.ds(pl.program_id(1), 1), :])
    bound = jnp.sqrt(jnp.concatenate(q2, axis=1) * kn2)
    fast_ok = jnp.max(bound - m_sc[...]) <= ATTN_FAST_MARGIN

    @pl.when(fast_ok)
    def _fixed_reference():
        ms = [m_sc[:, t * tq:(t + 1) * tq] for t in range(nq)]

        def body(j, carry):
            ks = k_tile(j)
            ps = []
            for t in range(nq):
                cols = slice(t * tq, (t + 1) * tq)
                p = jnp.exp2(_dot(ks, qTs[t]) - ms[t])
                l_sc[:, cols] = l_sc[:, cols] + jnp.sum(p, axis=0, keepdims=True)
                ps.append(p.astype(BF16))
            for t in range(nq):
                cols = slice(t * tq, (t + 1) * tq)
                acc_sc[:, cols] = acc_sc[:, cols] + _dot(vT_ref[0, j], ps[t])
            return carry

        lax.fori_loop(0, nk, body, 0, unroll=8)

    @pl.when(jnp.logical_not(fast_ok))
    def _online():
        qT = jnp.concatenate(qTs, axis=1)

        def body(j, carry):
            sT = _dot(k_tile(j), qT)
            m_prev = m_sc[...]
            m_new = jnp.maximum(m_prev, jnp.max(sT, axis=0, keepdims=True))
            alpha = jnp.exp2(m_prev - m_new)
            p = jnp.exp2(sT - m_new)
            l_sc[...] = l_sc[...] * alpha + jnp.sum(p, axis=0, keepdims=True)
            acc_sc[...] = acc_sc[...] * alpha + _dot(vT_ref[0, j], p.astype(BF16))
            m_sc[...] = m_new
            return carry

        lax.fori_loop(0, nk, body, 0)

    for t in range(nq):
        cols = slice(t * tq, (t + 1) * tq)
        o = acc_sc[:, cols] / l_sc[:, cols] * sgT_ref[0, t].astype(F32)
        o_ref[0, t] = o.astype(BF16)


def _attention(qT, k, vT, sgT, kn2, tm):
    B, nl, _, _ = qT.shape
    L = nl * tm
    nq = ATTN_Q_TILES
    kern = functools.partial(_attn_kernel, tk=tm, nk=nl, nq=nq)
    return pl.pallas_call(
        kern,
        grid=(B, N_HEADS, nl // nq),
        in_specs=[
            pl.BlockSpec((1, nq, HEAD_PAD, tm), lambda b, h, i: (b, i, h, 0)),
            pl.BlockSpec((1, 1, L, HEAD_PAD), lambda b, h, i: (b, h, 0, 0)),
            pl.BlockSpec((1, nl, V_DIM, tm), lambda b, h, i: (b, 0, h, 0)),
            pl.BlockSpec((1, nq, V_DIM, tm), lambda b, h, i: (b, i, h, 0)),
            pl.BlockSpec((1, nl, N_HEADS, LANES), lambda b, h, i: (b, 0, 0, 0)),
        ],
        out_specs=pl.BlockSpec((1, nq, V_DIM, tm), lambda b, h, i: (b, i, h, 0)),
        out_shape=jax.ShapeDtypeStruct((B, nl, N_HEADS * V_DIM, tm), BF16),
        scratch_shapes=[pltpu.VMEM((1, nq * tm), F32), pltpu.VMEM((1, nq * tm), F32),
                        pltpu.VMEM((V_DIM, nq * tm), F32)],
        compiler_params=_params("parallel", "parallel", "arbitrary"),
        name="mla_attn",
    )(qT, k, vT, sgT, kn2)


def _out_kernel(zT_ref, x_ref, wo_ref, g_ref, b_ref, o_ref):
    half = zT_ref.shape[3] // 2
    ys = [_dot_tn(zT_ref[0, 0, :, i * half:(i + 1) * half], wo_ref[...]) for i in range(2)]
    for i in range(2):
        rows = slice(i * half, (i + 1) * half)
        o_ref[0, rows] = _layer_norm(ALPHA * x_ref[0, rows] + ys[i], g_ref[...], b_ref[...])


def _mla_out(ogT, x, woT, ln_g, ln_b, tm):
    B, L, _ = x.shape
    nl = L // tm
    return pl.pallas_call(
        _out_kernel,
        grid=(B, nl),
        in_specs=[
            pl.BlockSpec((1, 1, D_MODEL, tm), lambda b, i: (b, i, 0, 0)),
            pl.BlockSpec((1, tm, D_MODEL), lambda b, i: (b, i, 0)),
            pl.BlockSpec((D_MODEL, D_MODEL), lambda b, i: (0, 0)),
            pl.BlockSpec((1, D_MODEL), lambda b, i: (0, 0)),
            pl.BlockSpec((1, D_MODEL), lambda b, i: (0, 0)),
        ],
        out_specs=pl.BlockSpec((1, tm, D_MODEL), lambda b, i: (b, i, 0)),
        out_shape=jax.ShapeDtypeStruct((B, L, D_MODEL), F32),
        compiler_params=_params("parallel", "parallel"),
        name="mla_out",
    )(ogT, x, woT, ln_g, ln_b)


def _s5_in_kernel(x_ref, wuT_ref, wgT_ref, u_ref, sgT_ref):
    xb = x_ref[0].astype(BF16)
    uT = _dot_nt(wuT_ref[...], xb)
    for c in range(u_ref.shape[0]):
        u_ref[c, 0] = uT[:, c * CHUNK:(c + 1) * CHUNK].astype(BF16)
    sgT_ref[...] = _silu(_dot_nt(wgT_ref[...], xb)).astype(BF16)


def _chunk_spec(tm):
    return pl.BlockSpec((tm // CHUNK, 1, D_MODEL, CHUNK), lambda b, i: (i, b, 0, 0))


def _s5_in(x, wuT, wgT, tm):
    B, L, _ = x.shape
    nl = L // tm
    tok = pl.BlockSpec((D_MODEL, tm), lambda b, i: (0, b * nl + i))
    wspec = pl.BlockSpec((D_MODEL, D_MODEL), lambda b, i: (0, 0))
    return pl.pallas_call(
        _s5_in_kernel,
        grid=(B, nl),
        in_specs=[pl.BlockSpec((1, tm, D_MODEL), lambda b, i: (b, i, 0)), wspec, wspec],
        out_specs=[_chunk_spec(tm), tok],
        out_shape=[jax.ShapeDtypeStruct((L // CHUNK, B, D_MODEL, CHUNK), BF16),
                   jax.ShapeDtypeStruct((D_MODEL, B * L), BF16)],
        compiler_params=_params("parallel", "parallel"),
        name="s5_in",
    )(x, wuT, wgT)


def _toep_kernel(cbT_ref, pwT_ref, o_ref, kv_sc):
    kv_sc[...] = jnp.dot(cbT_ref[0], pwT_ref[0], preferred_element_type=F32,
                         precision=lax.Precision.HIGHEST)

    def body(pi, carry):
        row0 = pl.multiple_of(pi * CHUNK, CHUNK)
        for po in range(S5_GROUP):
            kv = kv_sc[pl.ds(pi * S5_GROUP + po, 1), :]
            rolled = pltpu.roll(jnp.broadcast_to(kv, (CHUNK, 2 * CHUNK)), 0, 1, stride=1, stride_axis=0)
            o_ref[0, pl.ds(row0, CHUNK), po * CHUNK:(po + 1) * CHUNK] = rolled[:, CHUNK:].astype(BF16)
        return carry

    lax.fori_loop(0, S5_GROUP, body, 0)


def _toeplitz(cbT, pwT):
    G = cbT.shape[0]
    n = S5_GROUP * CHUNK
    blk = pl.BlockSpec((1, 2 * CHUNK, 2 * CHUNK), lambda g: (g, 0, 0))
    return pl.pallas_call(
        _toep_kernel,
        grid=(G,),
        in_specs=[blk, blk],
        out_specs=pl.BlockSpec((1, n, n), lambda g: (g, 0, 0)),
        out_shape=jax.ShapeDtypeStruct((G, n, n), BF16),
        scratch_shapes=[pltpu.VMEM((2 * CHUNK, 2 * CHUNK), F32)],
        compiler_params=_params("parallel"),
        name="s5_toeplitz",
    )(cbT, pwT)


def _sublane_transpose8(xs):
    xs = list(xs)
    sub = lax.broadcasted_iota(jnp.int32, xs[0].shape, 1)
    for k in (4, 2, 1):
        keep = (sub & k) == 0
        for i in range(8):
            if i & k:
                continue
            lo, hi = xs[i], xs[i + k]
            xs[i] = jnp.where(keep, lo, pltpu.roll(hi, k, 1))
            xs[i + k] = jnp.where(keep, pltpu.roll(lo, 8 - k, 1), hi)
    return xs


def _scan_kernel(u_ref, tm_ref, rs_ref, ri_ref, ac_ref, dg_ref, y_ref,
                 sre_sc, sim_sc, hfre_sc, hfim_sc, hbre_sc, hbim_sc, *, nseq, nchunk):
    R = 8 * u_ref.shape[0]
    rows8 = [u_ref[:, i].astype(F32) for i in range(8)]
    cols = []
    for half in range(S5_GROUP // 8):
        xs = [x[:, 8 * half:8 * half + 8, :] for x in rows8]
        cols += [x.reshape(R, CHUNK) for x in _sublane_transpose8(xs)]
    u = jnp.concatenate(cols, axis=1)
    lhs = u.astype(BF16)
    s = _dot(lhs, rs_ref[0])
    y = _dot(lhs, tm_ref[0])
    tile = 2 * S5_STATE
    sre_sc[...] = s[:, :tile]
    sim_sc[...] = s[:, tile:]
    ac = ac_ref[0]
    are, aim = ac[0:1], ac[1:2]
    is_fwd = lax.broadcasted_iota(jnp.int32, (nseq, tile), 1) < S5_STATE

    ere = jnp.zeros((nseq, tile), F32)
    eim = jnp.zeros((nseq, tile), F32)
    for c in range(nchunk):
        rf = pl.ds(c * nseq, nseq)
        rb = pl.ds((nchunk - 1 - c) * nseq, nseq)
        hfre_sc[rf, :] = ere
        hfim_sc[rf, :] = eim
        hbre_sc[rb, :] = ere
        hbim_sc[rb, :] = eim
        s_re = jnp.where(is_fwd, sre_sc[rf, :], sre_sc[rb, :])
        s_im = jnp.where(is_fwd, sim_sc[rf, :], sim_sc[rb, :])
        ere, eim = are * ere - aim * eim + s_re, are * eim + aim * ere + s_im

    fwd_rows = lax.broadcasted_iota(jnp.int32, hfre_sc.shape, 1) < S5_STATE
    h = jnp.concatenate([jnp.where(fwd_rows, hfre_sc[...], hbre_sc[...]),
                         jnp.where(fwd_rows, hfim_sc[...], hbim_sc[...])], axis=1).astype(BF16)
    y = y + _dot(h, ri_ref[0]) + dg_ref[0] * u
    halves = []
    for half in range(S5_GROUP // 8):
        ys = [y[:, (8 * half + p) * CHUNK:(8 * half + p + 1) * CHUNK].reshape(R // 8, 8, CHUNK) for p in range(8)]
        halves.append(_sublane_transpose8(ys))
    for i in range(8):
        y_ref[:, i] = jnp.concatenate([h[i] for h in halves], axis=1).astype(BF16)


def _s5_scan(u4, toep, rs, ri, ac, dg):
    nchunk, nseq, W, _ = u4.shape
    R = nchunk * nseq
    n = S5_GROUP * CHUNK
    kern = functools.partial(_scan_kernel, nseq=nseq, nchunk=nchunk)
    y3 = pl.pallas_call(
        kern,
        grid=(S5_GROUPS,),
        in_specs=[
            pl.BlockSpec((R // 8, 8, S5_GROUP, CHUNK), lambda g: (0, 0, g, 0)),
            pl.BlockSpec((1, n, n), lambda g: (g, 0, 0)),
            pl.BlockSpec((1, n, 4 * S5_STATE), lambda g: (g, 0, 0)),
            pl.BlockSpec((1, 4 * S5_STATE, n), lambda g: (g, 0, 0)),
            pl.BlockSpec((1, 8, 2 * S5_STATE), lambda g: (g, 0, 0)),
            pl.BlockSpec((1, 1, n), lambda g: (g, 0, 0)),
        ],
        out_specs=pl.BlockSpec((R // 8, 8, S5_GROUP, CHUNK), lambda g: (0, 0, g, 0)),
        out_shape=jax.ShapeDtypeStruct((R // 8, 8, W, CHUNK), BF16),
        scratch_shapes=[pltpu.VMEM((R, 2 * S5_STATE), F32)] * 6,
        compiler_params=_params("parallel"),
        name="s5_scan",
    )(u4.reshape(R // 8, 8, W, CHUNK), toep, rs, ri, ac, dg)
    return y3.reshape(nchunk, nseq, W, CHUNK)


def _s5_out_kernel(ys_ref, sg_ref, x_ref, wgluT_ref, bglu_ref, wo_ref, g_ref, b_ref, o_ref):
    rep = ys_ref.shape[0] // 2
    half = rep * CHUNK
    bglu = jnp.tile(bglu_ref[...], (1, rep))
    y = [jax.nn.gelu(jnp.concatenate([ys_ref[i * rep + c, 0] for c in range(rep)], axis=1).astype(F32))
         for i in range(2)]
    z = [_dot(wgluT_ref[...], y[i].astype(BF16)) + bglu for i in range(2)]
    v = [(y[i] * _sigmoid(z[i]) * sg_ref[:, i * half:(i + 1) * half].astype(F32)).astype(BF16) for i in range(2)]
    o = [_dot_tn(v[i], wo_ref[...]) for i in range(2)]
    for i in range(2):
        rows = slice(i * half, (i + 1) * half)
        o_ref[0, rows] = _layer_norm(ALPHA * x_ref[0, rows] + o[i], g_ref[...], b_ref[...])


def _s5_out(ys4, sgT, x, w, ln_g, ln_b, tm):
    B, L, _ = x.shape
    nl = L // tm
    tok = pl.BlockSpec((D_MODEL, tm), lambda b, i: (0, b * nl + i))
    sq = pl.BlockSpec((D_MODEL, D_MODEL), lambda b, i: (0, 0))
    col = pl.BlockSpec((D_MODEL, LANES), lambda b, i: (0, 0))
    row = pl.BlockSpec((1, D_MODEL), lambda b, i: (0, 0))
    return pl.pallas_call(
        _s5_out_kernel,
        grid=(B, nl),
        in_specs=[_chunk_spec(tm), tok, pl.BlockSpec((1, tm, D_MODEL), lambda b, i: (b, i, 0)),
                  sq, col, sq, row, row],
        out_specs=pl.BlockSpec((1, tm, D_MODEL), lambda b, i: (b, i, 0)),
        out_shape=jax.ShapeDtypeStruct((B, L, D_MODEL), F32),
        compiler_params=_params("parallel", "parallel"),
        name="s5_out",
    )(ys4, sgT, x, w['wgluT'], w['bglu'], w['wo'], ln_g, ln_b)


def _prep_mla(w_in, g_q, w_q_up, g_kv, w_kv_up, w_out):
    nkv = Q_LORA + KV_LORA
    wa = jnp.concatenate([w_in[:, :nkv + QK_ROPE_DIM],
                          jnp.zeros((D_MODEL, LANES - QK_ROPE_DIM), F32)], axis=1)
    wq = w_q_up.reshape(Q_LORA, N_HEADS, QK_NOPE_DIM + QK_ROPE_DIM)
    wq = jnp.pad(wq, ((0, 0), (0, 0), (0, HEAD_PAD - QK_NOPE_DIM - QK_ROPE_DIM)))
    wkv = w_kv_up.reshape(KV_LORA, N_HEADS, QK_NOPE_DIM + V_DIM)
    return dict(
        wa=wa.astype(BF16),
        gq=g_q.reshape(1, Q_LORA), gkv=g_kv.reshape(1, KV_LORA),
        wqT=wq.reshape(Q_LORA, N_HEADS * HEAD_PAD).T.astype(BF16),
        wkn=wkv[:, :, :QK_NOPE_DIM].reshape(KV_LORA, N_HEADS * QK_NOPE_DIM).astype(BF16),
        wvT=wkv[:, :, QK_NOPE_DIM:].reshape(KV_LORA, N_HEADS * V_DIM).T.astype(BF16),
        wgT=w_in[:, nkv + QK_ROPE_DIM:].T.astype(BF16),
        wo=w_out.astype(BF16),
    )


def _rope_tables(L):
    inv = ROPE_THETA ** (-jnp.arange(0, QK_ROPE_DIM, 2, dtype=F32) / QK_ROPE_DIM)
    ang = jnp.arange(L, dtype=F32)[:, None] * inv[None, :]
    cos, sin = jnp.cos(ang), jnp.sin(ang)
    z = jnp.zeros_like(cos)
    z2 = jnp.zeros((L, LANES - QK_ROPE_DIM), F32)
    return dict(
        cosT=cos.T, sinT=sin.T,
        cosk=jnp.concatenate([cos, cos, z2], axis=1),
        sina=jnp.concatenate([z, sin, z2], axis=1),
        sinb=jnp.concatenate([-sin, z, z2], axis=1),
    )


def _powers(lam_bar, n):
    pw = jnp.stack([jnp.ones_like(lam_bar), lam_bar], axis=-2)
    top = lam_bar
    while pw.shape[-2] - 1 < n:
        pw = jnp.concatenate([pw, pw[..., 1:, :] * top[..., None, :]], axis=-2)
        top = top * top
    return pw


def _reim(z, axis):
    return jnp.concatenate([jnp.real(z), jnp.imag(z)], axis=axis)


def _prep_s5(w_in, a_re, a_im, log_step, b_re, b_im, c_re, c_im, d, w_glu, b_glu, w_out):
    G, N, P, T = S5_GROUPS, S5_STATE, S5_GROUP, CHUNK
    lam = lax.complex(a_re, a_im)
    step = jnp.exp(log_step)[..., None]
    lam_bar = jnp.exp(lam * step)
    b_bar = ((lam_bar - 1.0) / lam)[..., None] * lax.complex(b_re, b_im)
    c = lax.complex(c_re, c_im)
    pw = _powers(lam_bar, T)

    cb = c[:, :, None, :, :] * jnp.swapaxes(b_bar, -1, -2)[:, :, :, None, :]
    cb = cb.reshape(2, G, P * P, N)
    cbT = jnp.concatenate([_reim(cb[0], -1), _reim(cb[1], -1)], axis=-1)
    pf = jnp.swapaxes(pw[0, :, :T], -1, -2)
    pb = jnp.swapaxes(pw[1, :, :T], -1, -2)[..., ::-1]
    zf = jnp.zeros((G, N, T), pf.dtype)
    pf = jnp.concatenate([zf, pf], axis=-1)
    pb = jnp.concatenate([zf[..., :1], pb, zf[..., :T - 1]], axis=-1)
    pwT = jnp.concatenate([jnp.real(pf), -jnp.imag(pf), jnp.real(pb), -jnp.imag(pb)], axis=1)

    sf = pw[0, :, T - 1::-1]
    sb = pw[1, :, :T]
    rs_f = jnp.swapaxes(b_bar[0], -1, -2)[:, :, None, :] * sf[:, None, :, :]
    rs_b = jnp.swapaxes(b_bar[1], -1, -2)[:, :, None, :] * sb[:, None, :, :]
    rs = jnp.concatenate([jnp.real(rs_f), jnp.real(rs_b), jnp.imag(rs_f), jnp.imag(rs_b)], axis=-1)
    rs = rs.reshape(G, P * T, 4 * N).astype(BF16)

    of = pw[0, :, 1:T + 1]
    ob = pw[1, :, T:0:-1]
    ri_f = c[0][:, :, None, :] * of[:, None, :, :]
    ri_b = c[1][:, :, None, :] * ob[:, None, :, :]
    ri = jnp.concatenate([jnp.real(ri_f), jnp.real(ri_b), -jnp.imag(ri_f), -jnp.imag(ri_b)], axis=-1)
    ri = jnp.swapaxes(ri.reshape(G, P * T, 4 * N), -1, -2).astype(BF16)

    a = pw[:, :, T]
    rows = [jnp.concatenate([jnp.real(a[0]), jnp.real(a[1])], -1),
            jnp.concatenate([jnp.imag(a[0]), jnp.imag(a[1])], -1)]
    ac = jnp.stack(rows + [jnp.zeros_like(rows[0])] * 6, axis=1)

    return dict(
        wuT=w_in[:, :D_MODEL].T.astype(BF16), wgT=w_in[:, D_MODEL:].T.astype(BF16),
        cbT=cbT, pwT=pwT, rs=rs, ri=ri, ac=ac,
        dg=jnp.repeat(d.reshape(G, 1, P), T, axis=-1),
        wgluT=w_glu.T.astype(BF16),
        bglu=jnp.broadcast_to(b_glu[:, None], (D_MODEL, LANES)),
        wo=w_out.astype(BF16),
    )


def _mla_layer(x, w, tabs, ln_g, ln_b, tm):
    qT, k, vT, sgT, kn2 = _mla_proj(x, w, tabs, tm)
    ogT = _attention(qT, k, vT, sgT, kn2, tm)
    return _mla_out(ogT, x, w['wo'], ln_g, ln_b, tm)


def _s5_layer(x, w, toep, ln_g, ln_b, tm):
    u4, sgT = _s5_in(x, w['wuT'], w['wgT'], tm)
    ys4 = _s5_scan(u4, toep, w['rs'], w['ri'], w['ac'], w['dg'])
    return _s5_out(ys4, sgT, x, w, ln_g, ln_b, tm)


def _trunk(x, mla_w, s5_w, toeps, ln_g, ln_b):
    L = x.shape[1]
    tm = min(TOK_TILE, L)
    tabs = _rope_tables(L)
    for i in range(DEPTH):
        g, b = ln_g[i].reshape(1, D_MODEL), ln_b[i].reshape(1, D_MODEL)
        if i % 2 == 0:
            x = _mla_layer(x, mla_w[i // 2], tabs, g, b, tm)
        else:
            x = _s5_layer(x, s5_w[i // 2], toeps[i // 2], g, b, tm)
    return x


def kernel(x_prompt, x_sample, mla_w_in, mla_g_q, mla_w_q_up, mla_g_kv, mla_w_kv_up, mla_w_out,
           s5_w_in, s5_a_re, s5_a_im, s5_log_step, s5_b_re, s5_b_im, s5_c_re, s5_c_im, s5_d,
           s5_w_glu, s5_b_glu, s5_w_out, ln_g, ln_b):
    mla = (mla_w_in, mla_g_q, mla_w_q_up, mla_g_kv, mla_w_kv_up, mla_w_out)
    s5 = (s5_w_in, s5_a_re, s5_a_im, s5_log_step, s5_b_re, s5_b_im, s5_c_re, s5_c_im, s5_d,
          s5_w_glu, s5_b_glu, s5_w_out)
    mla_w = [_prep_mla(*[w[j] for w in mla]) for j in range(mla_w_in.shape[0])]
    s5_w = [_prep_s5(*[w[j] for w in s5]) for j in range(s5_w_in.shape[0])]
    toeps = [_toeplitz(w['cbT'], w['pwT']) for w in s5_w]
    y_prompt = _trunk(x_prompt, mla_w, s5_w, toeps, ln_g, ln_b)
    y_sample = _trunk(x_sample, mla_w, s5_w, toeps, ln_g, ln_b)
    return (y_prompt, y_sample)
```

```python
import functools
import math

import jax
import jax.numpy as jnp
from jax import lax
from jax.experimental import pallas as pl
from jax.experimental.pallas import tpu as pltpu

F32 = jnp.float32
BF16 = jnp.bfloat16

D_MODEL = 1024
DEPTH = 4
N_HEADS = 8
QK_NOPE_DIM = 128
QK_ROPE_DIM = 64
V_DIM = 128
Q_LORA = 384
KV_LORA = 256
ROPE_THETA = 10000.0
ATTN_SCALE = 1.0 / math.sqrt(QK_NOPE_DIM + QK_ROPE_DIM)
S5_GROUP = 16
S5_GROUPS = D_MODEL // S5_GROUP
S5_STATE = 64
ALPHA = (2 * DEPTH) ** 0.25
LN_EPS = 1e-5
RMS_EPS = 1e-6

LANES = 128
HEAD_PAD = 256
ROPE_HALF = QK_ROPE_DIM // 2
LOG2_E = math.log2(math.e)
CHUNK = 128
TOK_TILE = 512
ATTN_Q_TILES = 8
ATTN_REF_KEYS = 128
ATTN_FAST_MARGIN = 60.0
VMEM_LIMIT = 56 * 1024 * 1024

NT_DIMS = (((1,), (1,)), ((), ()))
TN_DIMS = (((0,), (0,)), ((), ()))


def _dot(a, b):
    return jnp.dot(a, b, preferred_element_type=F32)


def _dot_nt(a, b):
    return lax.dot_general(a, b, NT_DIMS, preferred_element_type=F32)


def _dot_tn(a, b):
    return lax.dot_general(a, b, TN_DIMS, preferred_element_type=F32)


def _params(*sem):
    return pltpu.CompilerParams(dimension_semantics=sem, vmem_limit_bytes=VMEM_LIMIT)


def _layer_norm(r, g, b):
    mu = jnp.mean(r, axis=-1, keepdims=True)
    d = r - mu
    var = jnp.mean(d * d, axis=-1, keepdims=True)
    return d * lax.rsqrt(var + LN_EPS) * g + b


def _rms_norm(x, g):
    return x * lax.rsqrt(jnp.mean(x * x, axis=-1, keepdims=True) + RMS_EPS) * g


def _sigmoid(x):
    return 0.5 * jnp.tanh(0.5 * x) + 0.5


def _silu(x):
    h = 0.5 * x
    return h * jnp.tanh(h) + h


def _mla_proj_kernel(x_ref, wa_ref, gq_ref, gkv_ref, wqT_ref, wkn_ref, wvT_ref, wgT_ref,
                     cosT_ref, sinT_ref, cosk_ref, sina_ref, sinb_ref,
                     qT_ref, k_ref, vT_ref, sgT_ref, kn2_ref):
    xb = x_ref[0].astype(BF16)
    ha = _dot(xb, wa_ref[...])
    cq = _rms_norm(ha[:, :Q_LORA], gq_ref[...]).astype(BF16)
    ckv = _rms_norm(ha[:, Q_LORA:Q_LORA + KV_LORA], gkv_ref[...]).astype(BF16)
    kr = ha[:, Q_LORA + KV_LORA:]

    qT = _dot_nt(wqT_ref[...], cq) * (ATTN_SCALE * LOG2_E)
    cosT = cosT_ref[...]
    sinT = sinT_ref[...]
    tm = qT.shape[1]
    head = QK_NOPE_DIM + QK_ROPE_DIM
    zpad = jnp.zeros((HEAD_PAD - head, tm), F32)
    for h in range(N_HEADS):
        blk = qT[h * head:(h + 1) * head]
        t1 = blk[QK_NOPE_DIM:QK_NOPE_DIM + ROPE_HALF]
        t2 = blk[QK_NOPE_DIM + ROPE_HALF:QK_NOPE_DIM + QK_ROPE_DIM]
        out = jnp.concatenate(
            [blk[:QK_NOPE_DIM], t1 * cosT - t2 * sinT, t2 * cosT + t1 * sinT, zpad], axis=0)
        qT_ref[0, 0, h * HEAD_PAD:(h + 1) * HEAD_PAD, :] = out.astype(BF16)

    kn = _dot(ckv, wkn_ref[...])
    krr = (kr * cosk_ref[...]
           + pltpu.roll(kr, ROPE_HALF, 1) * sina_ref[...]
           + pltpu.roll(kr, LANES - ROPE_HALF, 1) * sinb_ref[...]).astype(BF16)
    krf = krr.astype(F32)
    kr2 = jnp.sum(krf * krf, axis=1, keepdims=True)
    kn2_rows = []
    for h in range(N_HEADS):
        kb = kn[:, h * QK_NOPE_DIM:(h + 1) * QK_NOPE_DIM].astype(BF16)
        k_ref[0, h, :, :QK_NOPE_DIM] = kb
        k_ref[0, h, :, QK_NOPE_DIM:] = krr
        kf = kb.astype(F32)
        n2 = jnp.max(jnp.sum(kf * kf, axis=1, keepdims=True) + kr2, axis=0, keepdims=True)
        kn2_rows.append(jnp.broadcast_to(n2, (1, LANES)))
    kn2_ref[0, 0] = jnp.concatenate(kn2_rows, axis=0)

    vT_ref[0, 0] = _dot_nt(wvT_ref[...], ckv).astype(BF16)
    sgT_ref[0, 0] = _silu(_dot_nt(wgT_ref[...], xb)).astype(BF16)


def _mla_proj(x, w, tabs, tm):
    B, L, _ = x.shape
    nl = L // tm
    const = lambda shape: pl.BlockSpec(shape, lambda b, i: (0,) * len(shape))
    return pl.pallas_call(
        _mla_proj_kernel,
        grid=(B, nl),
        in_specs=[
            pl.BlockSpec((1, tm, D_MODEL), lambda b, i: (b, i, 0)),
            const(w['wa'].shape), const(w['gq'].shape), const(w['gkv'].shape),
            const(w['wqT'].shape), const(w['wkn'].shape), const(w['wvT'].shape), const(w['wgT'].shape),
            pl.BlockSpec((ROPE_HALF, tm), lambda b, i: (0, i)),
            pl.BlockSpec((ROPE_HALF, tm), lambda b, i: (0, i)),
            pl.BlockSpec((tm, LANES), lambda b, i: (i, 0)),
            pl.BlockSpec((tm, LANES), lambda b, i: (i, 0)),
            pl.BlockSpec((tm, LANES), lambda b, i: (i, 0)),
        ],
        out_specs=[
            pl.BlockSpec((1, 1, N_HEADS * HEAD_PAD, tm), lambda b, i: (b, i, 0, 0)),
            pl.BlockSpec((1, N_HEADS, tm, HEAD_PAD), lambda b, i: (b, 0, i, 0)),
            pl.BlockSpec((1, 1, N_HEADS * V_DIM, tm), lambda b, i: (b, i, 0, 0)),
            pl.BlockSpec((1, 1, N_HEADS * V_DIM, tm), lambda b, i: (b, i, 0, 0)),
            pl.BlockSpec((1, 1, N_HEADS, LANES), lambda b, i: (b, i, 0, 0)),
        ],
        out_shape=[
            jax.ShapeDtypeStruct((B, nl, N_HEADS * HEAD_PAD, tm), BF16),
            jax.ShapeDtypeStruct((B, N_HEADS, L, HEAD_PAD), BF16),
            jax.ShapeDtypeStruct((B, nl, N_HEADS * V_DIM, tm), BF16),
            jax.ShapeDtypeStruct((B, nl, N_HEADS * V_DIM, tm), BF16),
            jax.ShapeDtypeStruct((B, nl, N_HEADS, LANES), F32),
        ],
        compiler_params=_params("parallel", "parallel"),
        name="mla_proj",
    )(x, w['wa'], w['gq'], w['gkv'], w['wqT'], w['wkn'], w['wvT'], w['wgT'],
      tabs['cosT'], tabs['sinT'], tabs['cosk'], tabs['sina'], tabs['sinb'])


def _attn_kernel(qT_ref, k_ref, vT_ref, sgT_ref, kn2_ref, o_ref, m_sc, l_sc, acc_sc, *, tk, nk, nq):
    tq = qT_ref.shape[3]
    qTs = [qT_ref[0, t] for t in range(nq)]

    def k_tile(j):
        return k_ref[0, 0, pl.ds(pl.multiple_of(j * tk, tk), tk), :]

    k_pre = k_ref[0, 0, 0:ATTN_REF_KEYS, :]
    q2 = []
    for t in range(nq):
        cols = slice(t * tq, (t + 1) * tq)
        m_sc[:, cols] = jnp.max(_dot(k_pre, qTs[t]), axis=0, keepdims=True)
        qf = qTs[t].astype(F32)
        q2.append(jnp.sum(qf * qf, axis=0, keepdims=True))
    acc_sc[...] = jnp.zeros(acc_sc.shape, F32)
    l_sc[...] = jnp.zeros(l_sc.shape, F32)

    kn2 = jnp.max(kn2_ref[0, :, pl.ds(pl.program_id(1), 1), :])
    bound = jnp.sqrt(jnp.concatenate(q2, axis=1) * kn2)
    fast_ok = jnp.max(bound - m_sc[...]) <= ATTN_FAST_MARGIN

    @pl.when(fast_ok)
    def _fixed_reference():
        ms = [m_sc[:, t * tq:(t + 1) * tq] for t in range(nq)]

        def body(j, carry):
            ks = k_tile(j)
            ps = []
            for t in range(nq):
                cols = slice(t * tq, (t + 1) * tq)
                p = jnp.exp2(_dot(ks, qTs[t]) - ms[t])
                l_sc[:, cols] = l_sc[:, cols] + jnp.sum(p, axis=0, keepdims=True)
                ps.append(p.astype(BF16))
            for t in range(nq):
                cols = slice(t * tq, (t + 1) * tq)
                acc_sc[:, cols] = acc_sc[:, cols] + _dot(vT_ref[0, j], ps[t])
            return carry

        lax.fori_loop(0, nk, body, 0, unroll=8)

    @pl.when(jnp.logical_not(fast_ok))
    def _online():
        qT = jnp.concatenate(qTs, axis=1)

        def body(j, carry):
            sT = _dot(k_tile(j), qT)
            m_prev = m_sc[...]
            m_new = jnp.maximum(m_prev, jnp.max(sT, axis=0, keepdims=True))
            alpha = jnp.exp2(m_prev - m_new)
            p = jnp.exp2(sT - m_new)
            l_sc[...] = l_sc[...] * alpha + jnp.sum(p, axis=0, keepdims=True)
            acc_sc[...] = acc_sc[...] * alpha + _dot(vT_ref[0, j], p.astype(BF16))
            m_sc[...] = m_new
            return carry

        lax.fori_loop(0, nk, body, 0)

    for t in range(nq):
        cols = slice(t * tq, (t + 1) * tq)
        o = acc_sc[:, cols] / l_sc[:, cols] * sgT_ref[0, t].astype(F32)
        o_ref[0, t] = o.astype(BF16)


def _attention(qT, k, vT, sgT, kn2, tm):
    B, nl, _, _ = qT.shape
    L = nl * tm
    nq = min(ATTN_Q_TILES, nl)
    assert nl % nq == 0 and L >= ATTN_REF_KEYS
    kern = functools.partial(_attn_kernel, tk=tm, nk=nl, nq=nq)
    return pl.pallas_call(
        kern,
        grid=(B, N_HEADS, nl // nq),
        in_specs=[
            pl.BlockSpec((1, nq, HEAD_PAD, tm), lambda b, h, i: (b, i, h, 0)),
            pl.BlockSpec((1, 1, L, HEAD_PAD), lambda b, h, i: (b, h, 0, 0)),
            pl.BlockSpec((1, nl, V_DIM, tm), lambda b, h, i: (b, 0, h, 0)),
            pl.BlockSpec((1, nq, V_DIM, tm), lambda b, h, i: (b, i, h, 0)),
            pl.BlockSpec((1, nl, N_HEADS, LANES), lambda b, h, i: (b, 0, 0, 0)),
        ],
        out_specs=pl.BlockSpec((1, nq, V_DIM, tm), lambda b, h, i: (b, i, h, 0)),
        out_shape=jax.ShapeDtypeStruct((B, nl, N_HEADS * V_DIM, tm), BF16),
        scratch_shapes=[pltpu.VMEM((1, nq * tm), F32), pltpu.VMEM((1, nq * tm), F32),
                        pltpu.VMEM((V_DIM, nq * tm), F32)],
        compiler_params=_params("parallel", "parallel", "arbitrary"),
        name="mla_attn",
    )(qT, k, vT, sgT, kn2)


def _out_kernel(zT_ref, x_ref, wo_ref, g_ref, b_ref, o_ref):
    half = zT_ref.shape[3] // 2
    ys = [_dot_tn(zT_ref[0, 0, :, i * half:(i + 1) * half], wo_ref[...]) for i in range(2)]
    for i in range(2):
        rows = slice(i * half, (i + 1) * half)
        o_ref[0, rows] = _layer_norm(ALPHA * x_ref[0, rows] + ys[i], g_ref[...], b_ref[...])


def _mla_out(ogT, x, woT, ln_g, ln_b, tm):
    B, L, _ = x.shape
    nl = L // tm
    return pl.pallas_call(
        _out_kernel,
        grid=(B, nl),
        in_specs=[
            pl.BlockSpec((1, 1, D_MODEL, tm), lambda b, i: (b, i, 0, 0)),
            pl.BlockSpec((1, tm, D_MODEL), lambda b, i: (b, i, 0)),
            pl.BlockSpec((D_MODEL, D_MODEL), lambda b, i: (0, 0)),
            pl.BlockSpec((1, D_MODEL), lambda b, i: (0, 0)),
            pl.BlockSpec((1, D_MODEL), lambda b, i: (0, 0)),
        ],
        out_specs=pl.BlockSpec((1, tm, D_MODEL), lambda b, i: (b, i, 0)),
        out_shape=jax.ShapeDtypeStruct((B, L, D_MODEL), F32),
        compiler_params=_params("parallel", "parallel"),
        name="mla_out",
    )(ogT, x, woT, ln_g, ln_b)


def _s5_in_kernel(x_ref, wuT_ref, wgT_ref, u_ref, sgT_ref):
    xb = x_ref[0].astype(BF16)
    uT = _dot_nt(wuT_ref[...], xb)
    for c in range(u_ref.shape[0]):
        u_ref[c, 0] = uT[:, c * CHUNK:(c + 1) * CHUNK].astype(BF16)
    sgT_ref[...] = _silu(_dot_nt(wgT_ref[...], xb)).astype(BF16)


def _chunk_spec(tm):
    return pl.BlockSpec((tm // CHUNK, 1, D_MODEL, CHUNK), lambda b, i: (i, b, 0, 0))


def _s5_in(x, wuT, wgT, tm):
    B, L, _ = x.shape
    nl = L // tm
    tok = pl.BlockSpec((D_MODEL, tm), lambda b, i: (0, b * nl + i))
    wspec = pl.BlockSpec((D_MODEL, D_MODEL), lambda b, i: (0, 0))
    return pl.pallas_call(
        _s5_in_kernel,
        grid=(B, nl),
        in_specs=[pl.BlockSpec((1, tm, D_MODEL), lambda b, i: (b, i, 0)), wspec, wspec],
        out_specs=[_chunk_spec(tm), tok],
        out_shape=[jax.ShapeDtypeStruct((L // CHUNK, B, D_MODEL, CHUNK), BF16),
                   jax.ShapeDtypeStruct((D_MODEL, B * L), BF16)],
        compiler_params=_params("parallel", "parallel"),
        name="s5_in",
    )(x, wuT, wgT)


def _toep_kernel(cbT_ref, pwT_ref, o_ref, kv_sc):
    kv_sc[...] = jnp.dot(cbT_ref[0], pwT_ref[0], preferred_element_type=F32,
                         precision=lax.Precision.HIGHEST)

    def body(pi, carry):
        row0 = pl.multiple_of(pi * CHUNK, CHUNK)
        for po in range(S5_GROUP):
            kv = kv_sc[pl.ds(pi * S5_GROUP + po, 1), :]
            rolled = pltpu.roll(jnp.broadcast_to(kv, (CHUNK, 2 * CHUNK)), 0, 1, stride=1, stride_axis=0)
            o_ref[0, pl.ds(row0, CHUNK), po * CHUNK:(po + 1) * CHUNK] = rolled[:, CHUNK:].astype(BF16)
        return carry

    lax.fori_loop(0, S5_GROUP, body, 0)


def _toeplitz(cbT, pwT):
    G = cbT.shape[0]
    n = S5_GROUP * CHUNK
    blk = pl.BlockSpec((1, 2 * CHUNK, 2 * CHUNK), lambda g: (g, 0, 0))
    return pl.pallas_call(
        _toep_kernel,
        grid=(G,),
        in_specs=[blk, blk],
        out_specs=pl.BlockSpec((1, n, n), lambda g: (g, 0, 0)),
        out_shape=jax.ShapeDtypeStruct((G, n, n), BF16),
        scratch_shapes=[pltpu.VMEM((2 * CHUNK, 2 * CHUNK), F32)],
        compiler_params=_params("parallel"),
        name="s5_toeplitz",
    )(cbT, pwT)


def _sublane_transpose8(xs):
    xs = list(xs)
    sub = lax.broadcasted_iota(jnp.int32, xs[0].shape, 1)
    for k in (4, 2, 1):
        keep = (sub & k) == 0
        for i in range(8):
            if i & k:
                continue
            lo, hi = xs[i], xs[i + k]
            xs[i] = jnp.where(keep, lo, pltpu.roll(hi, k, 1))
            xs[i + k] = jnp.where(keep, pltpu.roll(lo, 8 - k, 1), hi)
    return xs


def _scan_kernel(u_ref, tm_ref, rs_ref, ri_ref, ac_ref, dg_ref, y_ref,
                 sre_sc, sim_sc, hfre_sc, hfim_sc, hbre_sc, hbim_sc, *, nseq, nchunk):
    R = 8 * u_ref.shape[0]
    rows8 = [u_ref[:, i].astype(F32) for i in range(8)]
    cols = []
    for half in range(S5_GROUP // 8):
        xs = [x[:, 8 * half:8 * half + 8, :] for x in rows8]
        cols += [x.reshape(R, CHUNK) for x in _sublane_transpose8(xs)]
    u = jnp.concatenate(cols, axis=1)
    lhs = u.astype(BF16)
    s = _dot(lhs, rs_ref[0])
    y = _dot(lhs, tm_ref[0])
    tile = 2 * S5_STATE
    sre_sc[...] = s[:, :tile]
    sim_sc[...] = s[:, tile:]
    ac = ac_ref[0]
    are, aim = ac[0:1], ac[1:2]
    is_fwd = lax.broadcasted_iota(jnp.int32, (nseq, tile), 1) < S5_STATE

    ere = jnp.zeros((nseq, tile), F32)
    eim = jnp.zeros((nseq, tile), F32)
    for c in range(nchunk):
        rf = pl.ds(c * nseq, nseq)
        rb = pl.ds((nchunk - 1 - c) * nseq, nseq)
        hfre_sc[rf, :] = ere
        hfim_sc[rf, :] = eim
        hbre_sc[rb, :] = ere
        hbim_sc[rb, :] = eim
        s_re = jnp.where(is_fwd, sre_sc[rf, :], sre_sc[rb, :])
        s_im = jnp.where(is_fwd, sim_sc[rf, :], sim_sc[rb, :])
        ere, eim = are * ere - aim * eim + s_re, are * eim + aim * ere + s_im

    fwd_rows = lax.broadcasted_iota(jnp.int32, hfre_sc.shape, 1) < S5_STATE
    h = jnp.concatenate([jnp.where(fwd_rows, hfre_sc[...], hbre_sc[...]),
                         jnp.where(fwd_rows, hfim_sc[...], hbim_sc[...])], axis=1).astype(BF16)
    y = y + _dot(h, ri_ref[0]) + dg_ref[0] * u
    halves = []
    for half in range(S5_GROUP // 8):
        ys = [y[:, (8 * half + p) * CHUNK:(8 * half + p + 1) * CHUNK].reshape(R // 8, 8, CHUNK) for p in range(8)]
        halves.append(_sublane_transpose8(ys))
    for i in range(8):
        y_ref[:, i] = jnp.concatenate([h[i] for h in halves], axis=1).astype(BF16)


def _s5_scan(u4, toep, rs, ri, ac, dg):
    nchunk, nseq, W, _ = u4.shape
    R = nchunk * nseq
    n = S5_GROUP * CHUNK
    kern = functools.partial(_scan_kernel, nseq=nseq, nchunk=nchunk)
    y3 = pl.pallas_call(
        kern,
        grid=(S5_GROUPS,),
        in_specs=[
            pl.BlockSpec((R // 8, 8, S5_GROUP, CHUNK), lambda g: (0, 0, g, 0)),
            pl.BlockSpec((1, n, n), lambda g: (g, 0, 0)),
            pl.BlockSpec((1, n, 4 * S5_STATE), lambda g: (g, 0, 0)),
            pl.BlockSpec((1, 4 * S5_STATE, n), lambda g: (g, 0, 0)),
            pl.BlockSpec((1, 8, 2 * S5_STATE), lambda g: (g, 0, 0)),
            pl.BlockSpec((1, 1, n), lambda g: (g, 0, 0)),
        ],
        out_specs=pl.BlockSpec((R // 8, 8, S5_GROUP, CHUNK), lambda g: (0, 0, g, 0)),
        out_shape=jax.ShapeDtypeStruct((R // 8, 8, W, CHUNK), BF16),
        scratch_shapes=[pltpu.VMEM((R, 2 * S5_STATE), F32)] * 6,
        compiler_params=_params("parallel"),
        name="s5_scan",
    )(u4.reshape(R // 8, 8, W, CHUNK), toep, rs, ri, ac, dg)
    return y3.reshape(nchunk, nseq, W, CHUNK)


def _s5_out_kernel(ys_ref, sg_ref, x_ref, wgluT_ref, bglu_ref, wo_ref, g_ref, b_ref, o_ref):
    rep = ys_ref.shape[0] // 2
    half = rep * CHUNK
    bglu = jnp.tile(bglu_ref[...], (1, rep))
    y = [jax.nn.gelu(jnp.concatenate([ys_ref[i * rep + c, 0] for c in range(rep)], axis=1).astype(F32))
         for i in range(2)]
    z = [_dot(wgluT_ref[...], y[i].astype(BF16)) + bglu for i in range(2)]
    v = [(y[i] * _sigmoid(z[i]) * sg_ref[:, i * half:(i + 1) * half].astype(F32)).astype(BF16) for i in range(2)]
    o = [_dot_tn(v[i], wo_ref[...]) for i in range(2)]
    for i in range(2):
        rows = slice(i * half, (i + 1) * half)
        o_ref[0, rows] = _layer_norm(ALPHA * x_ref[0, rows] + o[i], g_ref[...], b_ref[...])


def _s5_out(ys4, sgT, x, w, ln_g, ln_b, tm):
    B, L, _ = x.shape
    nl = L // tm
    tok = pl.BlockSpec((D_MODEL, tm), lambda b, i: (0, b * nl + i))
    sq = pl.BlockSpec((D_MODEL, D_MODEL), lambda b, i: (0, 0))
    col = pl.BlockSpec((D_MODEL, LANES), lambda b, i: (0, 0))
    row = pl.BlockSpec((1, D_MODEL), lambda b, i: (0, 0))
    return pl.pallas_call(
        _s5_out_kernel,
        grid=(B, nl),
        in_specs=[_chunk_spec(tm), tok, pl.BlockSpec((1, tm, D_MODEL), lambda b, i: (b, i, 0)),
                  sq, col, sq, row, row],
        out_specs=pl.BlockSpec((1, tm, D_MODEL), lambda b, i: (b, i, 0)),
        out_shape=jax.ShapeDtypeStruct((B, L, D_MODEL), F32),
        compiler_params=_params("parallel", "parallel"),
        name="s5_out",
    )(ys4, sgT, x, w['wgluT'], w['bglu'], w['wo'], ln_g, ln_b)


def _prep_mla(w_in, g_q, w_q_up, g_kv, w_kv_up, w_out):
    nkv = Q_LORA + KV_LORA
    wa = jnp.concatenate([w_in[:, :nkv + QK_ROPE_DIM],
                          jnp.zeros((D_MODEL, LANES - QK_ROPE_DIM), F32)], axis=1)
    wkv = w_kv_up.reshape(KV_LORA, N_HEADS, QK_NOPE_DIM + V_DIM)
    return dict(
        wa=wa.astype(BF16),
        gq=g_q.reshape(1, Q_LORA), gkv=g_kv.reshape(1, KV_LORA),
        wqT=w_q_up.T.astype(BF16),
        wkn=wkv[:, :, :QK_NOPE_DIM].reshape(KV_LORA, N_HEADS * QK_NOPE_DIM).astype(BF16),
        wvT=wkv[:, :, QK_NOPE_DIM:].reshape(KV_LORA, N_HEADS * V_DIM).T.astype(BF16),
        wgT=w_in[:, nkv + QK_ROPE_DIM:].T.astype(BF16),
        wo=w_out.astype(BF16),
    )


def _rope_tables(L):
    inv = ROPE_THETA ** (-jnp.arange(0, QK_ROPE_DIM, 2, dtype=F32) / QK_ROPE_DIM)
    ang = jnp.arange(L, dtype=F32)[:, None] * inv[None, :]
    cos, sin = jnp.cos(ang), jnp.sin(ang)
    z = jnp.zeros_like(cos)
    z2 = jnp.zeros((L, LANES - QK_ROPE_DIM), F32)
    return dict(
        cosT=cos.T, sinT=sin.T,
        cosk=jnp.concatenate([cos, cos, z2], axis=1),
        sina=jnp.concatenate([z, sin, z2], axis=1),
        sinb=jnp.concatenate([-sin, z, z2], axis=1),
    )


def _powers(lam_bar, n):
    pw = jnp.stack([jnp.ones_like(lam_bar), lam_bar], axis=-2)
    top = lam_bar
    while pw.shape[-2] - 1 < n:
        pw = jnp.concatenate([pw, pw[..., 1:, :] * top[..., None, :]], axis=-2)
        top = top * top
    return pw


def _reim(z, axis):
    return jnp.concatenate([jnp.real(z), jnp.imag(z)], axis=axis)


def _prep_s5(w_in, a_re, a_im, log_step, b_re, b_im, c_re, c_im, d, w_glu, b_glu, w_out):
    G, N, P, T = S5_GROUPS, S5_STATE, S5_GROUP, CHUNK
    lam = lax.complex(a_re, a_im)
    step = jnp.exp(log_step)[..., None]
    lam_bar = jnp.exp(lam * step)
    b_bar = ((lam_bar - 1.0) / lam)[..., None] * lax.complex(b_re, b_im)
    c = lax.complex(c_re, c_im)
    pw = _powers(lam_bar, T)

    cb = c[:, :, None, :, :] * jnp.swapaxes(b_bar, -1, -2)[:, :, :, None, :]
    cb = cb.reshape(2, G, P * P, N)
    cbT = jnp.concatenate([_reim(cb[0], -1), _reim(cb[1], -1)], axis=-1)
    pf = jnp.swapaxes(pw[0, :, :T], -1, -2)
    pb = jnp.swapaxes(pw[1, :, :T], -1, -2)[..., ::-1]
    zf = jnp.zeros((G, N, T), pf.dtype)
    pf = jnp.concatenate([zf, pf], axis=-1)
    pb = jnp.concatenate([zf[..., :1], pb, zf[..., :T - 1]], axis=-1)
    pwT = jnp.concatenate([jnp.real(pf), -jnp.imag(pf), jnp.real(pb), -jnp.imag(pb)], axis=1)

    sf = pw[0, :, T - 1::-1]
    sb = pw[1, :, :T]
    rs_f = jnp.swapaxes(b_bar[0], -1, -2)[:, :, None, :] * sf[:, None, :, :]
    rs_b = jnp.swapaxes(b_bar[1], -1, -2)[:, :, None, :] * sb[:, None, :, :]
    rs = jnp.concatenate([jnp.real(rs_f), jnp.real(rs_b), jnp.imag(rs_f), jnp.imag(rs_b)], axis=-1)
    rs = rs.reshape(G, P * T, 4 * N).astype(BF16)

    of = pw[0, :, 1:T + 1]
    ob = pw[1, :, T:0:-1]
    ri_f = c[0][:, :, None, :] * of[:, None, :, :]
    ri_b = c[1][:, :, None, :] * ob[:, None, :, :]
    ri = jnp.concatenate([jnp.real(ri_f), jnp.real(ri_b), -jnp.imag(ri_f), -jnp.imag(ri_b)], axis=-1)
    ri = jnp.swapaxes(ri.reshape(G, P * T, 4 * N), -1, -2).astype(BF16)

    a = pw[:, :, T]
    rows = [jnp.concatenate([jnp.real(a[0]), jnp.real(a[1])], -1),
            jnp.concatenate([jnp.imag(a[0]), jnp.imag(a[1])], -1)]
    ac = jnp.stack(rows + [jnp.zeros_like(rows[0])] * 6, axis=1)

    return dict(
        wuT=w_in[:, :D_MODEL].T.astype(BF16), wgT=w_in[:, D_MODEL:].T.astype(BF16),
        cbT=cbT, pwT=pwT, rs=rs, ri=ri, ac=ac,
        dg=jnp.repeat(d.reshape(G, 1, P), T, axis=-1),
        wgluT=w_glu.T.astype(BF16),
        bglu=jnp.broadcast_to(b_glu[:, None], (D_MODEL, LANES)),
        wo=w_out.astype(BF16),
    )


def _mla_layer(x, w, tabs, ln_g, ln_b, tm):
    qT, k, vT, sgT, kn2 = _mla_proj(x, w, tabs, tm)
    ogT = _attention(qT, k, vT, sgT, kn2, tm)
    return _mla_out(ogT, x, w['wo'], ln_g, ln_b, tm)


def _s5_layer(x, w, toep, ln_g, ln_b, tm):
    u4, sgT = _s5_in(x, w['wuT'], w['wgT'], tm)
    ys4 = _s5_scan(u4, toep, w['rs'], w['ri'], w['ac'], w['dg'])
    return _s5_out(ys4, sgT, x, w, ln_g, ln_b, tm)


def _trunk(x, mla_w, s5_w, toeps, ln_g, ln_b):
    L = x.shape[1]
    tm = min(TOK_TILE, L)
    tabs = _rope_tables(L)
    for i in range(DEPTH):
        g, b = ln_g[i].reshape(1, D_MODEL), ln_b[i].reshape(1, D_MODEL)
        if i % 2 == 0:
            x = _mla_layer(x, mla_w[i // 2], tabs, g, b, tm)
        else:
            x = _s5_layer(x, s5_w[i // 2], toeps[i // 2], g, b, tm)
    return x


def kernel(x_prompt, x_sample, mla_w_in, mla_g_q, mla_w_q_up, mla_g_kv, mla_w_kv_up, mla_w_out,
           s5_w_in, s5_a_re, s5_a_im, s5_log_step, s5_b_re, s5_b_im, s5_c_re, s5_c_im, s5_d,
           s5_w_glu, s5_b_glu, s5_w_out, ln_g, ln_b):
    mla = (mla_w_in, mla_g_q, mla_w_q_up, mla_g_kv, mla_w_kv_up, mla_w_out)
    s5 = (s5_w_in, s5_a_re, s5_a_im, s5_log_step, s5_b_re, s5_b_im, s5_c_re, s5_c_im, s5_d,
          s5_w_glu, s5_b_glu, s5_w_out)
    mla_w = [_prep_mla(*[w[j] for w in mla]) for j in range(mla_w_in.shape[0])]
    s5_w = [_prep_s5(*[w[j] for w in s5]) for j in range(s5_w_in.shape[0])]
    toeps = [_toeplitz(w['cbT'], w['pwT']) for w in s5_w]
    y_prompt = _trunk(x_prompt, mla_w, s5_w, toeps, ln_g, ln_b)
    y_sample = _trunk(x_sample, mla_w, s5_w, toeps, ln_g, ln_b)
    return (y_prompt, y_sample)
```

```python
import functools
import math

import jax
import jax.numpy as jnp
from jax import lax
from jax.experimental import pallas as pl
from jax.experimental.pallas import tpu as pltpu

F32 = jnp.float32
BF16 = jnp.bfloat16

D_MODEL = 1024
DEPTH = 4
N_HEADS = 8
QK_NOPE_DIM = 128
QK_ROPE_DIM = 64
V_DIM = 128
Q_LORA = 384
KV_LORA = 256
ROPE_THETA = 10000.0
ATTN_SCALE = 1.0 / math.sqrt(QK_NOPE_DIM + QK_ROPE_DIM)
S5_GROUP = 16
S5_GROUPS = D_MODEL // S5_GROUP
S5_STATE = 64
ALPHA = (2 * DEPTH) ** 0.25
LN_EPS = 1e-5
RMS_EPS = 1e-6

LANES = 128
HEAD_PAD = 256
ROPE_HALF = QK_ROPE_DIM // 2
LOG2_E = math.log2(math.e)
CHUNK = 128
TOK_TILE = 512
WIDE_TILES = 2
ATTN_Q_TILES = 8
ATTN_REF_KEYS = 128
ATTN_FAST_MARGIN = 60.0
VMEM_LIMIT = 56 * 1024 * 1024

NT_DIMS = (((1,), (1,)), ((), ()))
TN_DIMS = (((0,), (0,)), ((), ()))


def _dot(a, b):
    return jnp.dot(a, b, preferred_element_type=F32)


def _dot_nt(a, b):
    return lax.dot_general(a, b, NT_DIMS, preferred_element_type=F32)


def _dot_tn(a, b):
    return lax.dot_general(a, b, TN_DIMS, preferred_element_type=F32)


def _params(*sem):
    return pltpu.CompilerParams(dimension_semantics=sem, vmem_limit_bytes=VMEM_LIMIT)


def _layer_norm(r, g, b):
    mu = jnp.mean(r, axis=-1, keepdims=True)
    d = r - mu
    var = jnp.mean(d * d, axis=-1, keepdims=True)
    return d * lax.rsqrt(var + LN_EPS) * g + b


def _rms_norm(x, g):
    return x * lax.rsqrt(jnp.mean(x * x, axis=-1, keepdims=True) + RMS_EPS) * g


def _sigmoid(x):
    return 0.5 * jnp.tanh(0.5 * x) + 0.5


def _silu(x):
    h = 0.5 * x
    return h * jnp.tanh(h) + h


def _mla_proj_kernel(x_ref, wa_ref, gq_ref, gkv_ref, wqT_ref, wkn_ref, wvT_ref, wgT_ref,
                     cosT_ref, sinT_ref, cosk_ref, sina_ref, sinb_ref,
                     qT_ref, k_ref, vT_ref, sgT_ref, kn2_ref):
    xb = x_ref[0].astype(BF16)
    ha = _dot(xb, wa_ref[...])
    cq = _rms_norm(ha[:, :Q_LORA], gq_ref[...]).astype(BF16)
    ckv = _rms_norm(ha[:, Q_LORA:Q_LORA + KV_LORA], gkv_ref[...]).astype(BF16)
    kr = ha[:, Q_LORA + KV_LORA:]

    qT = _dot_nt(wqT_ref[...], cq) * (ATTN_SCALE * LOG2_E)
    cosT = cosT_ref[...]
    sinT = sinT_ref[...]
    tm = qT.shape[1]
    head = QK_NOPE_DIM + QK_ROPE_DIM
    zpad = jnp.zeros((HEAD_PAD - head, tm), F32)
    for h in range(N_HEADS):
        blk = qT[h * head:(h + 1) * head]
        t1 = blk[QK_NOPE_DIM:QK_NOPE_DIM + ROPE_HALF]
        t2 = blk[QK_NOPE_DIM + ROPE_HALF:QK_NOPE_DIM + QK_ROPE_DIM]
        out = jnp.concatenate(
            [blk[:QK_NOPE_DIM], t1 * cosT - t2 * sinT, t2 * cosT + t1 * sinT, zpad], axis=0)
        qT_ref[0, 0, h * HEAD_PAD:(h + 1) * HEAD_PAD, :] = out.astype(BF16)

    kn = _dot(ckv, wkn_ref[...])
    krr = (kr * cosk_ref[...]
           + pltpu.roll(kr, ROPE_HALF, 1) * sina_ref[...]
           + pltpu.roll(kr, LANES - ROPE_HALF, 1) * sinb_ref[...]).astype(BF16)
    krf = krr.astype(F32)
    kr2 = jnp.sum(krf * krf, axis=1, keepdims=True)
    kn2_rows = []
    for h in range(N_HEADS):
        kb = kn[:, h * QK_NOPE_DIM:(h + 1) * QK_NOPE_DIM].astype(BF16)
        k_ref[0, h, :, :QK_NOPE_DIM] = kb
        k_ref[0, h, :, QK_NOPE_DIM:] = krr
        kf = kb.astype(F32)
        n2 = jnp.max(jnp.sum(kf * kf, axis=1, keepdims=True) + kr2, axis=0, keepdims=True)
        kn2_rows.append(jnp.broadcast_to(n2, (1, LANES)))
    kn2_ref[0, 0] = jnp.concatenate(kn2_rows, axis=0)

    vT_ref[0, 0] = _dot_nt(wvT_ref[...], ckv).astype(BF16)
    sgT_ref[0, 0] = _silu(_dot_nt(wgT_ref[...], xb)).astype(BF16)


def _mla_proj(x, w, tabs, tm):
    B, L, _ = x.shape
    nl = L // tm
    const = lambda shape: pl.BlockSpec(shape, lambda b, i: (0,) * len(shape))
    return pl.pallas_call(
        _mla_proj_kernel,
        grid=(B, nl),
        in_specs=[
            pl.BlockSpec((1, tm, D_MODEL), lambda b, i: (b, i, 0)),
            const(w['wa'].shape), const(w['gq'].shape), const(w['gkv'].shape),
            const(w['wqT'].shape), const(w['wkn'].shape), const(w['wvT'].shape), const(w['wgT'].shape),
            pl.BlockSpec((ROPE_HALF, tm), lambda b, i: (0, i)),
            pl.BlockSpec((ROPE_HALF, tm), lambda b, i: (0, i)),
            pl.BlockSpec((tm, LANES), lambda b, i: (i, 0)),
            pl.BlockSpec((tm, LANES), lambda b, i: (i, 0)),
            pl.BlockSpec((tm, LANES), lambda b, i: (i, 0)),
        ],
        out_specs=[
            pl.BlockSpec((1, 1, N_HEADS * HEAD_PAD, tm), lambda b, i: (b, i, 0, 0)),
            pl.BlockSpec((1, N_HEADS, tm, HEAD_PAD), lambda b, i: (b, 0, i, 0)),
            pl.BlockSpec((1, 1, N_HEADS * V_DIM, tm), lambda b, i: (b, i, 0, 0)),
            pl.BlockSpec((1, 1, N_HEADS * V_DIM, tm), lambda b, i: (b, i, 0, 0)),
            pl.BlockSpec((1, 1, N_HEADS, LANES), lambda b, i: (b, i, 0, 0)),
        ],
        out_shape=[
            jax.ShapeDtypeStruct((B, nl, N_HEADS * HEAD_PAD, tm), BF16),
            jax.ShapeDtypeStruct((B, N_HEADS, L, HEAD_PAD), BF16),
            jax.ShapeDtypeStruct((B, nl, N_HEADS * V_DIM, tm), BF16),
            jax.ShapeDtypeStruct((B, nl, N_HEADS * V_DIM, tm), BF16),
            jax.ShapeDtypeStruct((B, nl, N_HEADS, LANES), F32),
        ],
        compiler_params=_params("parallel", "parallel"),
        name="mla_proj",
    )(x, w['wa'], w['gq'], w['gkv'], w['wqT'], w['wkn'], w['wvT'], w['wgT'],
      tabs['cosT'], tabs['sinT'], tabs['cosk'], tabs['sina'], tabs['sinb'])


def _attn_kernel(qT_ref, k_ref, vT_ref, sgT_ref, kn2_ref, o_ref, m_sc, l_sc, acc_sc, *, tk, nk, nq):
    tq = qT_ref.shape[3]
    qTs = [qT_ref[0, t] for t in range(nq)]

    def k_tile(j):
        return k_ref[0, 0, pl.ds(pl.multiple_of(j * tk, tk), tk), :]

    k_pre = k_ref[0, 0, 0:ATTN_REF_KEYS, :]
    q2 = []
    for t in range(nq):
        cols = slice(t * tq, (t + 1) * tq)
        m_sc[:, cols] = jnp.max(_dot(k_pre, qTs[t]), axis=0, keepdims=True)
        qf = qTs[t].astype(F32)
        q2.append(jnp.sum(qf * qf, axis=0, keepdims=True))
    acc_sc[...] = jnp.zeros(acc_sc.shape, F32)
    l_sc[...] = jnp.zeros(l_sc.shape, F32)

    kn2 = jnp.max(kn2_ref[0, :, pl.ds(pl.program_id(1), 1), :])
    bound = jnp.sqrt(jnp.concatenate(q2, axis=1) * kn2)
    fast_ok = jnp.max(bound - m_sc[...]) <= ATTN_FAST_MARGIN

    @pl.when(fast_ok)
    def _fixed_reference():
        ms = [m_sc[:, t * tq:(t + 1) * tq] for t in range(nq)]

        def body(j, carry):
            ks = k_tile(j)
            ps = []
            for t in range(nq):
                cols = slice(t * tq, (t + 1) * tq)
                p = jnp.exp2(_dot(ks, qTs[t]) - ms[t])
                l_sc[:, cols] = l_sc[:, cols] + jnp.sum(p, axis=0, keepdims=True)
                ps.append(p.astype(BF16))
            for t in range(nq):
                cols = slice(t * tq, (t + 1) * tq)
                acc_sc[:, cols] = acc_sc[:, cols] + _dot(vT_ref[0, j], ps[t])
            return carry

        lax.fori_loop(0, nk, body, 0, unroll=8)

    @pl.when(jnp.logical_not(fast_ok))
    def _online():
        qT = jnp.concatenate(qTs, axis=1)

        def body(j, carry):
            sT = _dot(k_tile(j), qT)
            m_prev = m_sc[...]
            m_new = jnp.maximum(m_prev, jnp.max(sT, axis=0, keepdims=True))
            alpha = jnp.exp2(m_prev - m_new)
            p = jnp.exp2(sT - m_new)
            l_sc[...] = l_sc[...] * alpha + jnp.sum(p, axis=0, keepdims=True)
            acc_sc[...] = acc_sc[...] * alpha + _dot(vT_ref[0, j], p.astype(BF16))
            m_sc[...] = m_new
            return carry

        lax.fori_loop(0, nk, body, 0)

    for t in range(nq):
        cols = slice(t * tq, (t + 1) * tq)
        o = acc_sc[:, cols] / l_sc[:, cols] * sgT_ref[0, t].astype(F32)
        o_ref[0, t] = o.astype(BF16)


def _attention(qT, k, vT, sgT, kn2, tm):
    B, nl, _, _ = qT.shape
    L = nl * tm
    nq = min(ATTN_Q_TILES, nl)
    assert nl % nq == 0 and L >= ATTN_REF_KEYS
    kern = functools.partial(_attn_kernel, tk=tm, nk=nl, nq=nq)
    return pl.pallas_call(
        kern,
        grid=(B, N_HEADS, nl // nq),
        in_specs=[
            pl.BlockSpec((1, nq, HEAD_PAD, tm), lambda b, h, i: (b, i, h, 0)),
            pl.BlockSpec((1, 1, L, HEAD_PAD), lambda b, h, i: (b, h, 0, 0)),
            pl.BlockSpec((1, nl, V_DIM, tm), lambda b, h, i: (b, 0, h, 0)),
            pl.BlockSpec((1, nq, V_DIM, tm), lambda b, h, i: (b, i, h, 0)),
            pl.BlockSpec((1, nl, N_HEADS, LANES), lambda b, h, i: (b, 0, 0, 0)),
        ],
        out_specs=pl.BlockSpec((1, nq, V_DIM, tm), lambda b, h, i: (b, i, h, 0)),
        out_shape=jax.ShapeDtypeStruct((B, nl, N_HEADS * V_DIM, tm), BF16),
        scratch_shapes=[pltpu.VMEM((1, nq * tm), F32), pltpu.VMEM((1, nq * tm), F32),
                        pltpu.VMEM((V_DIM, nq * tm), F32)],
        compiler_params=_params("parallel", "parallel", "arbitrary"),
        name="mla_attn",
    )(qT, k, vT, sgT, kn2)


def _out_kernel(zT_ref, x_ref, wo_ref, g_ref, b_ref, o_ref):
    nt, tm = zT_ref.shape[1], zT_ref.shape[3]
    half = tm // 2
    pieces = [(t, i) for t in range(nt) for i in range(2)]
    ys = [_dot_tn(zT_ref[0, t, :, i * half:(i + 1) * half], wo_ref[...]) for t, i in pieces]
    for (t, i), y in zip(pieces, ys):
        rows = slice(t * tm + i * half, t * tm + (i + 1) * half)
        o_ref[0, rows] = _layer_norm(ALPHA * x_ref[0, rows] + y, g_ref[...], b_ref[...])


def _mla_out(ogT, x, woT, ln_g, ln_b, tm):
    B, L, _ = x.shape
    nl = L // tm
    nt = min(WIDE_TILES, nl)
    return pl.pallas_call(
        _out_kernel,
        grid=(B, nl // nt),
        in_specs=[
            pl.BlockSpec((1, nt, D_MODEL, tm), lambda b, i: (b, i, 0, 0)),
            pl.BlockSpec((1, nt * tm, D_MODEL), lambda b, i: (b, i, 0)),
            pl.BlockSpec((D_MODEL, D_MODEL), lambda b, i: (0, 0)),
            pl.BlockSpec((1, D_MODEL), lambda b, i: (0, 0)),
            pl.BlockSpec((1, D_MODEL), lambda b, i: (0, 0)),
        ],
        out_specs=pl.BlockSpec((1, nt * tm, D_MODEL), lambda b, i: (b, i, 0)),
        out_shape=jax.ShapeDtypeStruct((B, L, D_MODEL), F32),
        compiler_params=_params("parallel", "parallel"),
        name="mla_out",
    )(ogT, x, woT, ln_g, ln_b)


def _s5_in_kernel(x_ref, wuT_ref, wgT_ref, u_ref, sgT_ref):
    xb = x_ref[0].astype(BF16)
    uT = _dot_nt(wuT_ref[...], xb)
    for c in range(u_ref.shape[0]):
        u_ref[c, 0] = uT[:, c * CHUNK:(c + 1) * CHUNK].astype(BF16)
    sgT_ref[...] = _silu(_dot_nt(wgT_ref[...], xb)).astype(BF16)


def _chunk_spec(tm):
    return pl.BlockSpec((tm // CHUNK, 1, D_MODEL, CHUNK), lambda b, i: (i, b, 0, 0))


def _s5_in(x, wuT, wgT, tm):
    B, L, _ = x.shape
    nl = L // tm
    tok = pl.BlockSpec((D_MODEL, tm), lambda b, i: (0, b * nl + i))
    wspec = pl.BlockSpec((D_MODEL, D_MODEL), lambda b, i: (0, 0))
    return pl.pallas_call(
        _s5_in_kernel,
        grid=(B, nl),
        in_specs=[pl.BlockSpec((1, tm, D_MODEL), lambda b, i: (b, i, 0)), wspec, wspec],
        out_specs=[_chunk_spec(tm), tok],
        out_shape=[jax.ShapeDtypeStruct((L // CHUNK, B, D_MODEL, CHUNK), BF16),
                   jax.ShapeDtypeStruct((D_MODEL, B * L), BF16)],
        compiler_params=_params("parallel", "parallel"),
        name="s5_in",
    )(x, wuT, wgT)


def _toep_kernel(cbT_ref, pwT_ref, o_ref, kv_sc):
    kv_sc[...] = jnp.dot(cbT_ref[0], pwT_ref[0], preferred_element_type=F32,
                         precision=lax.Precision.HIGHEST)

    def body(pi, carry):
        row0 = pl.multiple_of(pi * CHUNK, CHUNK)
        for po in range(S5_GROUP):
            kv = kv_sc[pl.ds(pi * S5_GROUP + po, 1), :]
            rolled = pltpu.roll(jnp.broadcast_to(kv, (CHUNK, 2 * CHUNK)), 0, 1, stride=1, stride_axis=0)
            o_ref[0, pl.ds(row0, CHUNK), po * CHUNK:(po + 1) * CHUNK] = rolled[:, CHUNK:].astype(BF16)
        return carry

    lax.fori_loop(0, S5_GROUP, body, 0)


def _toeplitz(cbT, pwT):
    G = cbT.shape[0]
    n = S5_GROUP * CHUNK
    blk = pl.BlockSpec((1, 2 * CHUNK, 2 * CHUNK), lambda g: (g, 0, 0))
    return pl.pallas_call(
        _toep_kernel,
        grid=(G,),
        in_specs=[blk, blk],
        out_specs=pl.BlockSpec((1, n, n), lambda g: (g, 0, 0)),
        out_shape=jax.ShapeDtypeStruct((G, n, n), BF16),
        scratch_shapes=[pltpu.VMEM((2 * CHUNK, 2 * CHUNK), F32)],
        compiler_params=_params("parallel"),
        name="s5_toeplitz",
    )(cbT, pwT)


def _sublane_transpose8(xs):
    xs = list(xs)
    sub = lax.broadcasted_iota(jnp.int32, xs[0].shape, 1)
    for k in (4, 2, 1):
        keep = (sub & k) == 0
        for i in range(8):
            if i & k:
                continue
            lo, hi = xs[i], xs[i + k]
            xs[i] = jnp.where(keep, lo, pltpu.roll(hi, k, 1))
            xs[i + k] = jnp.where(keep, pltpu.roll(lo, 8 - k, 1), hi)
    return xs


def _scan_kernel(u_ref, tm_ref, rs_ref, ri_ref, ac_ref, dg_ref, y_ref,
                 sre_sc, sim_sc, hfre_sc, hfim_sc, hbre_sc, hbim_sc, *, nseq, nchunk):
    R = 8 * u_ref.shape[0]
    rows8 = [u_ref[:, i].astype(F32) for i in range(8)]
    cols = []
    for half in range(S5_GROUP // 8):
        xs = [x[:, 8 * half:8 * half + 8, :] for x in rows8]
        cols += [x.reshape(R, CHUNK) for x in _sublane_transpose8(xs)]
    u = jnp.concatenate(cols, axis=1)
    lhs = u.astype(BF16)
    s = _dot(lhs, rs_ref[0])
    y = _dot(lhs, tm_ref[0])
    tile = 2 * S5_STATE
    sre_sc[...] = s[:, :tile]
    sim_sc[...] = s[:, tile:]
    ac = ac_ref[0]
    are, aim = ac[0:1], ac[1:2]
    is_fwd = lax.broadcasted_iota(jnp.int32, (nseq, tile), 1) < S5_STATE

    ere = jnp.zeros((nseq, tile), F32)
    eim = jnp.zeros((nseq, tile), F32)
    for c in range(nchunk):
        rf = pl.ds(c * nseq, nseq)
        rb = pl.ds((nchunk - 1 - c) * nseq, nseq)
        hfre_sc[rf, :] = ere
        hfim_sc[rf, :] = eim
        hbre_sc[rb, :] = ere
        hbim_sc[rb, :] = eim
        s_re = jnp.where(is_fwd, sre_sc[rf, :], sre_sc[rb, :])
        s_im = jnp.where(is_fwd, sim_sc[rf, :], sim_sc[rb, :])
        ere, eim = are * ere - aim * eim + s_re, are * eim + aim * ere + s_im

    fwd_rows = lax.broadcasted_iota(jnp.int32, hfre_sc.shape, 1) < S5_STATE
    h = jnp.concatenate([jnp.where(fwd_rows, hfre_sc[...], hbre_sc[...]),
                         jnp.where(fwd_rows, hfim_sc[...], hbim_sc[...])], axis=1).astype(BF16)
    y = y + _dot(h, ri_ref[0]) + dg_ref[0] * u
    halves = []
    for half in range(S5_GROUP // 8):
        ys = [y[:, (8 * half + p) * CHUNK:(8 * half + p + 1) * CHUNK].reshape(R // 8, 8, CHUNK) for p in range(8)]
        halves.append(_sublane_transpose8(ys))
    for i in range(8):
        y_ref[:, i] = jnp.concatenate([h[i] for h in halves], axis=1).astype(BF16)


def _s5_scan(u4, toep, rs, ri, ac, dg):
    nchunk, nseq, W, _ = u4.shape
    R = nchunk * nseq
    n = S5_GROUP * CHUNK
    kern = functools.partial(_scan_kernel, nseq=nseq, nchunk=nchunk)
    y3 = pl.pallas_call(
        kern,
        grid=(S5_GROUPS,),
        in_specs=[
            pl.BlockSpec((R // 8, 8, S5_GROUP, CHUNK), lambda g: (0, 0, g, 0)),
            pl.BlockSpec((1, n, n), lambda g: (g, 0, 0)),
            pl.BlockSpec((1, n, 4 * S5_STATE), lambda g: (g, 0, 0)),
            pl.BlockSpec((1, 4 * S5_STATE, n), lambda g: (g, 0, 0)),
            pl.BlockSpec((1, 8, 2 * S5_STATE), lambda g: (g, 0, 0)),
            pl.BlockSpec((1, 1, n), lambda g: (g, 0, 0)),
        ],
        out_specs=pl.BlockSpec((R // 8, 8, S5_GROUP, CHUNK), lambda g: (0, 0, g, 0)),
        out_shape=jax.ShapeDtypeStruct((R // 8, 8, W, CHUNK), BF16),
        scratch_shapes=[pltpu.VMEM((R, 2 * S5_STATE), F32)] * 6,
        compiler_params=_params("parallel"),
        name="s5_scan",
    )(u4.reshape(R // 8, 8, W, CHUNK), toep, rs, ri, ac, dg)
    return y3.reshape(nchunk, nseq, W, CHUNK)


def _s5_out_kernel(ys_ref, sg_ref, x_ref, wgluT_ref, bglu_ref, wo_ref, g_ref, b_ref, o_ref):
    rep = ys_ref.shape[0] // 2
    half = rep * CHUNK
    bglu = jnp.tile(bglu_ref[...], (1, rep))
    y = [jax.nn.gelu(jnp.concatenate([ys_ref[i * rep + c, 0] for c in range(rep)], axis=1).astype(F32))
         for i in range(2)]
    z = [_dot(wgluT_ref[...], y[i].astype(BF16)) + bglu for i in range(2)]
    v = [(y[i] * _sigmoid(z[i]) * sg_ref[:, i * half:(i + 1) * half].astype(F32)).astype(BF16) for i in range(2)]
    o = [_dot_tn(v[i], wo_ref[...]) for i in range(2)]
    for i in range(2):
        rows = slice(i * half, (i + 1) * half)
        o_ref[0, rows] = _layer_norm(ALPHA * x_ref[0, rows] + o[i], g_ref[...], b_ref[...])


def _s5_out(ys4, sgT, x, w, ln_g, ln_b, tm):
    B, L, _ = x.shape
    nl = L // tm
    tok = pl.BlockSpec((D_MODEL, tm), lambda b, i: (0, b * nl + i))
    sq = pl.BlockSpec((D_MODEL, D_MODEL), lambda b, i: (0, 0))
    col = pl.BlockSpec((D_MODEL, LANES), lambda b, i: (0, 0))
    row = pl.BlockSpec((1, D_MODEL), lambda b, i: (0, 0))
    return pl.pallas_call(
        _s5_out_kernel,
        grid=(B, nl),
        in_specs=[_chunk_spec(tm), tok, pl.BlockSpec((1, tm, D_MODEL), lambda b, i: (b, i, 0)),
                  sq, col, sq, row, row],
        out_specs=pl.BlockSpec((1, tm, D_MODEL), lambda b, i: (b, i, 0)),
        out_shape=jax.ShapeDtypeStruct((B, L, D_MODEL), F32),
        compiler_params=_params("parallel", "parallel"),
        name="s5_out",
    )(ys4, sgT, x, w['wgluT'], w['bglu'], w['wo'], ln_g, ln_b)


def _prep_mla(w_in, g_q, w_q_up, g_kv, w_kv_up, w_out):
    nkv = Q_LORA + KV_LORA
    wa = jnp.concatenate([w_in[:, :nkv + QK_ROPE_DIM],
                          jnp.zeros((D_MODEL, LANES - QK_ROPE_DIM), F32)], axis=1)
    wkv = w_kv_up.reshape(KV_LORA, N_HEADS, QK_NOPE_DIM + V_DIM)
    return dict(
        wa=wa.astype(BF16),
        gq=g_q.reshape(1, Q_LORA), gkv=g_kv.reshape(1, KV_LORA),
        wqT=w_q_up.T.astype(BF16),
        wkn=wkv[:, :, :QK_NOPE_DIM].reshape(KV_LORA, N_HEADS * QK_NOPE_DIM).astype(BF16),
        wvT=wkv[:, :, QK_NOPE_DIM:].reshape(KV_LORA, N_HEADS * V_DIM).T.astype(BF16),
        wgT=w_in[:, nkv + QK_ROPE_DIM:].T.astype(BF16),
        wo=w_out.astype(BF16),
    )


def _rope_tables(L):
    inv = ROPE_THETA ** (-jnp.arange(0, QK_ROPE_DIM, 2, dtype=F32) / QK_ROPE_DIM)
    ang = jnp.arange(L, dtype=F32)[:, None] * inv[None, :]
    cos, sin = jnp.cos(ang), jnp.sin(ang)
    z = jnp.zeros_like(cos)
    z2 = jnp.zeros((L, LANES - QK_ROPE_DIM), F32)
    return dict(
        cosT=cos.T, sinT=sin.T,
        cosk=jnp.concatenate([cos, cos, z2], axis=1),
        sina=jnp.concatenate([z, sin, z2], axis=1),
        sinb=jnp.concatenate([-sin, z, z2], axis=1),
    )


def _powers(lam_bar, n):
    pw = jnp.stack([jnp.ones_like(lam_bar), lam_bar], axis=-2)
    top = lam_bar
    while pw.shape[-2] - 1 < n:
        pw = jnp.concatenate([pw, pw[..., 1:, :] * top[..., None, :]], axis=-2)
        top = top * top
    return pw


def _reim(z, axis):
    return jnp.concatenate([jnp.real(z), jnp.imag(z)], axis=axis)


def _prep_s5(w_in, a_re, a_im, log_step, b_re, b_im, c_re, c_im, d, w_glu, b_glu, w_out):
    G, N, P, T = S5_GROUPS, S5_STATE, S5_GROUP, CHUNK
    lam = lax.complex(a_re, a_im)
    step = jnp.exp(log_step)[..., None]
    lam_bar = jnp.exp(lam * step)
    b_bar = ((lam_bar - 1.0) / lam)[..., None] * lax.complex(b_re, b_im)
    c = lax.complex(c_re, c_im)
    pw = _powers(lam_bar, T)

    cb = c[:, :, None, :, :] * jnp.swapaxes(b_bar, -1, -2)[:, :, :, None, :]
    cb = cb.reshape(2, G, P * P, N)
    cbT = jnp.concatenate([_reim(cb[0], -1), _reim(cb[1], -1)], axis=-1)
    pf = jnp.swapaxes(pw[0, :, :T], -1, -2)
    pb = jnp.swapaxes(pw[1, :, :T], -1, -2)[..., ::-1]
    zf = jnp.zeros((G, N, T), pf.dtype)
    pf = jnp.concatenate([zf, pf], axis=-1)
    pb = jnp.concatenate([zf[..., :1], pb, zf[..., :T - 1]], axis=-1)
    pwT = jnp.concatenate([jnp.real(pf), -jnp.imag(pf), jnp.real(pb), -jnp.imag(pb)], axis=1)

    sf = pw[0, :, T - 1::-1]
    sb = pw[1, :, :T]
    rs_f = jnp.swapaxes(b_bar[0], -1, -2)[:, :, None, :] * sf[:, None, :, :]
    rs_b = jnp.swapaxes(b_bar[1], -1, -2)[:, :, None, :] * sb[:, None, :, :]
    rs = jnp.concatenate([jnp.real(rs_f), jnp.real(rs_b), jnp.imag(rs_f), jnp.imag(rs_b)], axis=-1)
    rs = rs.reshape(G, P * T, 4 * N).astype(BF16)

    of = pw[0, :, 1:T + 1]
    ob = pw[1, :, T:0:-1]
    ri_f = c[0][:, :, None, :] * of[:, None, :, :]
    ri_b = c[1][:, :, None, :] * ob[:, None, :, :]
    ri = jnp.concatenate([jnp.real(ri_f), jnp.real(ri_b), -jnp.imag(ri_f), -jnp.imag(ri_b)], axis=-1)
    ri = jnp.swapaxes(ri.reshape(G, P * T, 4 * N), -1, -2).astype(BF16)

    a = pw[:, :, T]
    rows = [jnp.concatenate([jnp.real(a[0]), jnp.real(a[1])], -1),
            jnp.concatenate([jnp.imag(a[0]), jnp.imag(a[1])], -1)]
    ac = jnp.stack(rows + [jnp.zeros_like(rows[0])] * 6, axis=1)

    return dict(
        wuT=w_in[:, :D_MODEL].T.astype(BF16), wgT=w_in[:, D_MODEL:].T.astype(BF16),
        cbT=cbT, pwT=pwT, rs=rs, ri=ri, ac=ac,
        dg=jnp.repeat(d.reshape(G, 1, P), T, axis=-1),
        wgluT=w_glu.T.astype(BF16),
        bglu=jnp.broadcast_to(b_glu[:, None], (D_MODEL, LANES)),
        wo=w_out.astype(BF16),
    )


def _mla_layer(x, w, tabs, ln_g, ln_b, tm):
    qT, k, vT, sgT, kn2 = _mla_proj(x, w, tabs, tm)
    ogT = _attention(qT, k, vT, sgT, kn2, tm)
    return _mla_out(ogT, x, w['wo'], ln_g, ln_b, tm)


def _s5_layer(x, w, toep, ln_g, ln_b, tm):
    wide = min(WIDE_TILES * tm, x.shape[1])
    u4, sgT = _s5_in(x, w['wuT'], w['wgT'], wide)
    ys4 = _s5_scan(u4, toep, w['rs'], w['ri'], w['ac'], w['dg'])
    return _s5_out(ys4, sgT, x, w, ln_g, ln_b, wide)


def _trunk(x, mla_w, s5_w, toeps, ln_g, ln_b):
    L = x.shape[1]
    tm = min(TOK_TILE, L)
    tabs = _rope_tables(L)
    for i in range(DEPTH):
        g, b = ln_g[i].reshape(1, D_MODEL), ln_b[i].reshape(1, D_MODEL)
        if i % 2 == 0:
            x = _mla_layer(x, mla_w[i // 2], tabs, g, b, tm)
        else:
            x = _s5_layer(x, s5_w[i // 2], toeps[i // 2], g, b, tm)
    return x


def kernel(x_prompt, x_sample, mla_w_in, mla_g_q, mla_w_q_up, mla_g_kv, mla_w_kv_up, mla_w_out,
           s5_w_in, s5_a_re, s5_a_im, s5_log_step, s5_b_re, s5_b_im, s5_c_re, s5_c_im, s5_d,
           s5_w_glu, s5_b_glu, s5_w_out, ln_g, ln_b):
    mla = (mla_w_in, mla_g_q, mla_w_q_up, mla_g_kv, mla_w_kv_up, mla_w_out)
    s5 = (s5_w_in, s5_a_re, s5_a_im, s5_log_step, s5_b_re, s5_b_im, s5_c_re, s5_c_im, s5_d,
          s5_w_glu, s5_b_glu, s5_w_out)
    mla_w = [_prep_mla(*[w[j] for w in mla]) for j in range(mla_w_in.shape[0])]
    s5_w = [_prep_s5(*[w[j] for w in s5]) for j in range(s5_w_in.shape[0])]
    toeps = [_toeplitz(w['cbT'], w['pwT']) for w in s5_w]
    y_prompt = _trunk(x_prompt, mla_w, s5_w, toeps, ln_g, ln_b)
    y_sample = _trunk(x_sample, mla_w, s5_w, toeps, ln_g, ln_b)
    return (y_prompt, y_sample)
```

```python
import functools
import math

import jax
import jax.numpy as jnp
from jax import lax
from jax.experimental import pallas as pl
from jax.experimental.pallas import tpu as pltpu

F32 = jnp.float32
BF16 = jnp.bfloat16

D_MODEL = 1024
DEPTH = 4
N_HEADS = 8
QK_NOPE_DIM = 128
QK_ROPE_DIM = 64
V_DIM = 128
Q_LORA = 384
KV_LORA = 256
ROPE_THETA = 10000.0
ATTN_SCALE = 1.0 / math.sqrt(QK_NOPE_DIM + QK_ROPE_DIM)
S5_GROUP = 16
S5_GROUPS = D_MODEL // S5_GROUP
S5_STATE = 64
ALPHA = (2 * DEPTH) ** 0.25
LN_EPS = 1e-5
RMS_EPS = 1e-6

LANES = 128
HEAD_PAD = 256
ROPE_HALF = QK_ROPE_DIM // 2
LOG2_E = math.log2(math.e)
CHUNK = 128
TOK_TILE = 512
WIDE_TILES = 2
ATTN_Q_TILES = 8
ATTN_REF_KEYS = 128
ATTN_FAST_MARGIN = 60.0
VMEM_LIMIT = 56 * 1024 * 1024

NT_DIMS = (((1,), (1,)), ((), ()))
TN_DIMS = (((0,), (0,)), ((), ()))


def _dot(a, b):
    return jnp.dot(a, b, preferred_element_type=F32)


def _dot_nt(a, b):
    return lax.dot_general(a, b, NT_DIMS, preferred_element_type=F32)


def _dot_tn(a, b):
    return lax.dot_general(a, b, TN_DIMS, preferred_element_type=F32)


def _params(*sem):
    return pltpu.CompilerParams(dimension_semantics=sem, vmem_limit_bytes=VMEM_LIMIT)


def _layer_norm(r, g, b):
    mu = jnp.mean(r, axis=-1, keepdims=True)
    d = r - mu
    var = jnp.mean(d * d, axis=-1, keepdims=True)
    return d * lax.rsqrt(var + LN_EPS) * g + b


def _rms_norm(x, g):
    return x * lax.rsqrt(jnp.mean(x * x, axis=-1, keepdims=True) + RMS_EPS) * g


def _sigmoid(x):
    return 0.5 * jnp.tanh(0.5 * x) + 0.5


def _silu(x):
    h = 0.5 * x
    return h * jnp.tanh(h) + h


def _mla_proj_kernel(x_ref, wa_ref, gq_ref, gkv_ref, wqT_ref, wkn_ref, wvT_ref, wgT_ref,
                     cosT_ref, sinT_ref, cosk_ref, sina_ref, sinb_ref,
                     qT_ref, k_ref, vT_ref, sgT_ref, kn2_ref):
    def put(ref, rows, val):
        w = ref.shape[3]
        for c in range(ref.shape[1]):
            ref[0, c, rows, :] = val[:, c * w:(c + 1) * w]

    xb = x_ref[0].astype(BF16)
    ha = _dot(xb, wa_ref[...])
    cq = _rms_norm(ha[:, :Q_LORA], gq_ref[...]).astype(BF16)
    ckv = _rms_norm(ha[:, Q_LORA:Q_LORA + KV_LORA], gkv_ref[...]).astype(BF16)
    kr = ha[:, Q_LORA + KV_LORA:]

    qT = _dot_nt(wqT_ref[...], cq) * (ATTN_SCALE * LOG2_E)
    cosT = cosT_ref[...]
    sinT = sinT_ref[...]
    tm = qT.shape[1]
    head = QK_NOPE_DIM + QK_ROPE_DIM
    zpad = jnp.zeros((HEAD_PAD - head, tm), F32)
    for h in range(N_HEADS):
        blk = qT[h * head:(h + 1) * head]
        t1 = blk[QK_NOPE_DIM:QK_NOPE_DIM + ROPE_HALF]
        t2 = blk[QK_NOPE_DIM + ROPE_HALF:QK_NOPE_DIM + QK_ROPE_DIM]
        out = jnp.concatenate(
            [blk[:QK_NOPE_DIM], t1 * cosT - t2 * sinT, t2 * cosT + t1 * sinT, zpad], axis=0)
        put(qT_ref, slice(h * HEAD_PAD, (h + 1) * HEAD_PAD), out.astype(BF16))

    kn = _dot(ckv, wkn_ref[...])
    krr = (kr * cosk_ref[...]
           + pltpu.roll(kr, ROPE_HALF, 1) * sina_ref[...]
           + pltpu.roll(kr, LANES - ROPE_HALF, 1) * sinb_ref[...]).astype(BF16)
    krf = krr.astype(F32)
    kr2 = jnp.sum(krf * krf, axis=1, keepdims=True)
    kn2_rows = []
    for h in range(N_HEADS):
        kb = kn[:, h * QK_NOPE_DIM:(h + 1) * QK_NOPE_DIM].astype(BF16)
        k_ref[0, h, :, :QK_NOPE_DIM] = kb
        k_ref[0, h, :, QK_NOPE_DIM:] = krr
        kf = kb.astype(F32)
        n2 = jnp.max(jnp.sum(kf * kf, axis=1, keepdims=True) + kr2, axis=0, keepdims=True)
        kn2_rows.append(jnp.broadcast_to(n2, (1, LANES)))
    kn2 = jnp.concatenate(kn2_rows, axis=0)
    for c in range(kn2_ref.shape[1]):
        kn2_ref[0, c] = kn2

    put(vT_ref, slice(None), _dot_nt(wvT_ref[...], ckv).astype(BF16))
    put(sgT_ref, slice(None), _silu(_dot_nt(wgT_ref[...], xb)).astype(BF16))


def _mla_proj(x, w, tabs, tm):
    B, L, _ = x.shape
    nl = L // tm
    nt = min(WIDE_TILES, nl)
    wide = nt * tm
    const = lambda shape: pl.BlockSpec(shape, lambda b, i: (0,) * len(shape))
    return pl.pallas_call(
        _mla_proj_kernel,
        grid=(B, nl // nt),
        in_specs=[
            pl.BlockSpec((1, wide, D_MODEL), lambda b, i: (b, i, 0)),
            const(w['wa'].shape), const(w['gq'].shape), const(w['gkv'].shape),
            const(w['wqT'].shape), const(w['wkn'].shape), const(w['wvT'].shape), const(w['wgT'].shape),
            pl.BlockSpec((ROPE_HALF, wide), lambda b, i: (0, i)),
            pl.BlockSpec((ROPE_HALF, wide), lambda b, i: (0, i)),
            pl.BlockSpec((wide, LANES), lambda b, i: (i, 0)),
            pl.BlockSpec((wide, LANES), lambda b, i: (i, 0)),
            pl.BlockSpec((wide, LANES), lambda b, i: (i, 0)),
        ],
        out_specs=[
            pl.BlockSpec((1, nt, N_HEADS * HEAD_PAD, tm), lambda b, i: (b, i, 0, 0)),
            pl.BlockSpec((1, N_HEADS, wide, HEAD_PAD), lambda b, i: (b, 0, i, 0)),
            pl.BlockSpec((1, nt, N_HEADS * V_DIM, tm), lambda b, i: (b, i, 0, 0)),
            pl.BlockSpec((1, nt, N_HEADS * V_DIM, tm), lambda b, i: (b, i, 0, 0)),
            pl.BlockSpec((1, nt, N_HEADS, LANES), lambda b, i: (b, i, 0, 0)),
        ],
        out_shape=[
            jax.ShapeDtypeStruct((B, nl, N_HEADS * HEAD_PAD, tm), BF16),
            jax.ShapeDtypeStruct((B, N_HEADS, L, HEAD_PAD), BF16),
            jax.ShapeDtypeStruct((B, nl, N_HEADS * V_DIM, tm), BF16),
            jax.ShapeDtypeStruct((B, nl, N_HEADS * V_DIM, tm), BF16),
            jax.ShapeDtypeStruct((B, nl, N_HEADS, LANES), F32),
        ],
        compiler_params=_params("parallel", "parallel"),
        name="mla_proj",
    )(x, w['wa'], w['gq'], w['gkv'], w['wqT'], w['wkn'], w['wvT'], w['wgT'],
      tabs['cosT'], tabs['sinT'], tabs['cosk'], tabs['sina'], tabs['sinb'])


def _attn_kernel(qT_ref, k_ref, vT_ref, sgT_ref, kn2_ref, o_ref, m_sc, l_sc, acc_sc, *, tk, nk, nq):
    tq = qT_ref.shape[3]
    qTs = [qT_ref[0, t] for t in range(nq)]

    def k_tile(j):
        return k_ref[0, 0, pl.ds(pl.multiple_of(j * tk, tk), tk), :]

    k_pre = k_ref[0, 0, 0:ATTN_REF_KEYS, :]
    q2 = []
    for t in range(nq):
        cols = slice(t * tq, (t + 1) * tq)
        m_sc[:, cols] = jnp.max(_dot(k_pre, qTs[t]), axis=0, keepdims=True)
        qf = qTs[t].astype(F32)
        q2.append(jnp.sum(qf * qf, axis=0, keepdims=True))
    acc_sc[...] = jnp.zeros(acc_sc.shape, F32)
    l_sc[...] = jnp.zeros(l_sc.shape, F32)

    kn2 = jnp.max(kn2_ref[0, :, pl.ds(pl.program_id(1), 1), :])
    bound = jnp.sqrt(jnp.concatenate(q2, axis=1) * kn2)
    fast_ok = jnp.max(bound - m_sc[...]) <= ATTN_FAST_MARGIN

    @pl.when(fast_ok)
    def _fixed_reference():
        ms = [m_sc[:, t * tq:(t + 1) * tq] for t in range(nq)]

        def body(j, carry):
            ks = k_tile(j)
            ps = []
            for t in range(nq):
                cols = slice(t * tq, (t + 1) * tq)
                p = jnp.exp2(_dot(ks, qTs[t]) - ms[t])
                l_sc[:, cols] = l_sc[:, cols] + jnp.sum(p, axis=0, keepdims=True)
                ps.append(p.astype(BF16))
            for t in range(nq):
                cols = slice(t * tq, (t + 1) * tq)
                acc_sc[:, cols] = acc_sc[:, cols] + _dot(vT_ref[0, j], ps[t])
            return carry

        lax.fori_loop(0, nk, body, 0, unroll=8)

    @pl.when(jnp.logical_not(fast_ok))
    def _online():
        qT = jnp.concatenate(qTs, axis=1)

        def body(j, carry):
            sT = _dot(k_tile(j), qT)
            m_prev = m_sc[...]
            m_new = jnp.maximum(m_prev, jnp.max(sT, axis=0, keepdims=True))
            alpha = jnp.exp2(m_prev - m_new)
            p = jnp.exp2(sT - m_new)
            l_sc[...] = l_sc[...] * alpha + jnp.sum(p, axis=0, keepdims=True)
            acc_sc[...] = acc_sc[...] * alpha + _dot(vT_ref[0, j], p.astype(BF16))
            m_sc[...] = m_new
            return carry

        lax.fori_loop(0, nk, body, 0)

    for t in range(nq):
        cols = slice(t * tq, (t + 1) * tq)
        o = acc_sc[:, cols] / l_sc[:, cols] * sgT_ref[0, t].astype(F32)
        o_ref[0, t] = o.astype(BF16)


def _attention(qT, k, vT, sgT, kn2, tm):
    B, nl, _, _ = qT.shape
    L = nl * tm
    nq = min(ATTN_Q_TILES, nl)
    assert nl % nq == 0 and L >= ATTN_REF_KEYS
    kern = functools.partial(_attn_kernel, tk=tm, nk=nl, nq=nq)
    return pl.pallas_call(
        kern,
        grid=(B, N_HEADS, nl // nq),
        in_specs=[
            pl.BlockSpec((1, nq, HEAD_PAD, tm), lambda b, h, i: (b, i, h, 0)),
            pl.BlockSpec((1, 1, L, HEAD_PAD), lambda b, h, i: (b, h, 0, 0)),
            pl.BlockSpec((1, nl, V_DIM, tm), lambda b, h, i: (b, 0, h, 0)),
            pl.BlockSpec((1, nq, V_DIM, tm), lambda b, h, i: (b, i, h, 0)),
            pl.BlockSpec((1, nl, N_HEADS, LANES), lambda b, h, i: (b, 0, 0, 0)),
        ],
        out_specs=pl.BlockSpec((1, nq, V_DIM, tm), lambda b, h, i: (b, i, h, 0)),
        out_shape=jax.ShapeDtypeStruct((B, nl, N_HEADS * V_DIM, tm), BF16),
        scratch_shapes=[pltpu.VMEM((1, nq * tm), F32), pltpu.VMEM((1, nq * tm), F32),
                        pltpu.VMEM((V_DIM, nq * tm), F32)],
        compiler_params=_params("parallel", "parallel", "arbitrary"),
        name="mla_attn",
    )(qT, k, vT, sgT, kn2)


def _out_kernel(zT_ref, x_ref, wo_ref, g_ref, b_ref, o_ref):
    nt, tm = zT_ref.shape[1], zT_ref.shape[3]
    half = tm // 2
    pieces = [(t, i) for t in range(nt) for i in range(2)]
    ys = [_dot_tn(zT_ref[0, t, :, i * half:(i + 1) * half], wo_ref[...]) for t, i in pieces]
    for (t, i), y in zip(pieces, ys):
        rows = slice(t * tm + i * half, t * tm + (i + 1) * half)
        o_ref[0, rows] = _layer_norm(ALPHA * x_ref[0, rows] + y, g_ref[...], b_ref[...])


def _mla_out(ogT, x, woT, ln_g, ln_b, tm):
    B, L, _ = x.shape
    nl = L // tm
    nt = min(WIDE_TILES, nl)
    return pl.pallas_call(
        _out_kernel,
        grid=(B, nl // nt),
        in_specs=[
            pl.BlockSpec((1, nt, D_MODEL, tm), lambda b, i: (b, i, 0, 0)),
            pl.BlockSpec((1, nt * tm, D_MODEL), lambda b, i: (b, i, 0)),
            pl.BlockSpec((D_MODEL, D_MODEL), lambda b, i: (0, 0)),
            pl.BlockSpec((1, D_MODEL), lambda b, i: (0, 0)),
            pl.BlockSpec((1, D_MODEL), lambda b, i: (0, 0)),
        ],
        out_specs=pl.BlockSpec((1, nt * tm, D_MODEL), lambda b, i: (b, i, 0)),
        out_shape=jax.ShapeDtypeStruct((B, L, D_MODEL), F32),
        compiler_params=_params("parallel", "parallel"),
        name="mla_out",
    )(ogT, x, woT, ln_g, ln_b)


def _s5_in_kernel(x_ref, wuT_ref, wgT_ref, u_ref, sgT_ref):
    xb = x_ref[0].astype(BF16)
    uT = _dot_nt(wuT_ref[...], xb)
    for c in range(u_ref.shape[0]):
        u_ref[c, 0] = uT[:, c * CHUNK:(c + 1) * CHUNK].astype(BF16)
    sgT_ref[...] = _silu(_dot_nt(wgT_ref[...], xb)).astype(BF16)


def _chunk_spec(tm):
    return pl.BlockSpec((tm // CHUNK, 1, D_MODEL, CHUNK), lambda b, i: (i, b, 0, 0))


def _s5_in(x, wuT, wgT, tm):
    B, L, _ = x.shape
    nl = L // tm
    tok = pl.BlockSpec((D_MODEL, tm), lambda b, i: (0, b * nl + i))
    wspec = pl.BlockSpec((D_MODEL, D_MODEL), lambda b, i: (0, 0))
    return pl.pallas_call(
        _s5_in_kernel,
        grid=(B, nl),
        in_specs=[pl.BlockSpec((1, tm, D_MODEL), lambda b, i: (b, i, 0)), wspec, wspec],
        out_specs=[_chunk_spec(tm), tok],
        out_shape=[jax.ShapeDtypeStruct((L // CHUNK, B, D_MODEL, CHUNK), BF16),
                   jax.ShapeDtypeStruct((D_MODEL, B * L), BF16)],
        compiler_params=_params("parallel", "parallel"),
        name="s5_in",
    )(x, wuT, wgT)


def _toep_kernel(cbT_ref, pwT_ref, o_ref, kv_sc):
    kv_sc[...] = jnp.dot(cbT_ref[0], pwT_ref[0], preferred_element_type=F32,
                         precision=lax.Precision.HIGHEST)

    def body(pi, carry):
        row0 = pl.multiple_of(pi * CHUNK, CHUNK)
        for po in range(S5_GROUP):
            kv = kv_sc[pl.ds(pi * S5_GROUP + po, 1), :]
            rolled = pltpu.roll(jnp.broadcast_to(kv, (CHUNK, 2 * CHUNK)), 0, 1, stride=1, stride_axis=0)
            o_ref[0, pl.ds(row0, CHUNK), po * CHUNK:(po + 1) * CHUNK] = rolled[:, CHUNK:].astype(BF16)
        return carry

    lax.fori_loop(0, S5_GROUP, body, 0)


def _toeplitz(cbT, pwT):
    G = cbT.shape[0]
    n = S5_GROUP * CHUNK
    blk = pl.BlockSpec((1, 2 * CHUNK, 2 * CHUNK), lambda g: (g, 0, 0))
    return pl.pallas_call(
        _toep_kernel,
        grid=(G,),
        in_specs=[blk, blk],
        out_specs=pl.BlockSpec((1, n, n), lambda g: (g, 0, 0)),
        out_shape=jax.ShapeDtypeStruct((G, n, n), BF16),
        scratch_shapes=[pltpu.VMEM((2 * CHUNK, 2 * CHUNK), F32)],
        compiler_params=_params("parallel"),
        name="s5_toeplitz",
    )(cbT, pwT)


def _sublane_transpose8(xs):
    xs = list(xs)
    sub = lax.broadcasted_iota(jnp.int32, xs[0].shape, 1)
    for k in (4, 2, 1):
        keep = (sub & k) == 0
        for i in range(8):
            if i & k:
                continue
            lo, hi = xs[i], xs[i + k]
            xs[i] = jnp.where(keep, lo, pltpu.roll(hi, k, 1))
            xs[i + k] = jnp.where(keep, pltpu.roll(lo, 8 - k, 1), hi)
    return xs


def _scan_kernel(u_ref, tm_ref, rs_ref, ri_ref, ac_ref, dg_ref, y_ref,
                 sre_sc, sim_sc, hfre_sc, hfim_sc, hbre_sc, hbim_sc, *, nseq, nchunk):
    R = 8 * u_ref.shape[0]
    rows8 = [u_ref[:, i].astype(F32) for i in range(8)]
    cols = []
    for half in range(S5_GROUP // 8):
        xs = [x[:, 8 * half:8 * half + 8, :] for x in rows8]
        cols += [x.reshape(R, CHUNK) for x in _sublane_transpose8(xs)]
    u = jnp.concatenate(cols, axis=1)
    lhs = u.astype(BF16)
    s = _dot(lhs, rs_ref[0])
    y = _dot(lhs, tm_ref[0])
    tile = 2 * S5_STATE
    sre_sc[...] = s[:, :tile]
    sim_sc[...] = s[:, tile:]
    ac = ac_ref[0]
    are, aim = ac[0:1], ac[1:2]
    is_fwd = lax.broadcasted_iota(jnp.int32, (nseq, tile), 1) < S5_STATE

    ere = jnp.zeros((nseq, tile), F32)
    eim = jnp.zeros((nseq, tile), F32)
    for c in range(nchunk):
        rf = pl.ds(c * nseq, nseq)
        rb = pl.ds((nchunk - 1 - c) * nseq, nseq)
        hfre_sc[rf, :] = ere
        hfim_sc[rf, :] = eim
        hbre_sc[rb, :] = ere
        hbim_sc[rb, :] = eim
        s_re = jnp.where(is_fwd, sre_sc[rf, :], sre_sc[rb, :])
        s_im = jnp.where(is_fwd, sim_sc[rf, :], sim_sc[rb, :])
        ere, eim = are * ere - aim * eim + s_re, are * eim + aim * ere + s_im

    fwd_rows = lax.broadcasted_iota(jnp.int32, hfre_sc.shape, 1) < S5_STATE
    h = jnp.concatenate([jnp.where(fwd_rows, hfre_sc[...], hbre_sc[...]),
                         jnp.where(fwd_rows, hfim_sc[...], hbim_sc[...])], axis=1).astype(BF16)
    y = y + _dot(h, ri_ref[0]) + dg_ref[0] * u
    halves = []
    for half in range(S5_GROUP // 8):
        ys = [y[:, (8 * half + p) * CHUNK:(8 * half + p + 1) * CHUNK].reshape(R // 8, 8, CHUNK) for p in range(8)]
        halves.append(_sublane_transpose8(ys))
    for i in range(8):
        y_ref[:, i] = jnp.concatenate([h[i] for h in halves], axis=1).astype(BF16)


def _s5_scan(u4, toep, rs, ri, ac, dg):
    nchunk, nseq, W, _ = u4.shape
    R = nchunk * nseq
    n = S5_GROUP * CHUNK
    kern = functools.partial(_scan_kernel, nseq=nseq, nchunk=nchunk)
    y3 = pl.pallas_call(
        kern,
        grid=(S5_GROUPS,),
        in_specs=[
            pl.BlockSpec((R // 8, 8, S5_GROUP, CHUNK), lambda g: (0, 0, g, 0)),
            pl.BlockSpec((1, n, n), lambda g: (g, 0, 0)),
            pl.BlockSpec((1, n, 4 * S5_STATE), lambda g: (g, 0, 0)),
            pl.BlockSpec((1, 4 * S5_STATE, n), lambda g: (g, 0, 0)),
            pl.BlockSpec((1, 8, 2 * S5_STATE), lambda g: (g, 0, 0)),
            pl.BlockSpec((1, 1, n), lambda g: (g, 0, 0)),
        ],
        out_specs=pl.BlockSpec((R // 8, 8, S5_GROUP, CHUNK), lambda g: (0, 0, g, 0)),
        out_shape=jax.ShapeDtypeStruct((R // 8, 8, W, CHUNK), BF16),
        scratch_shapes=[pltpu.VMEM((R, 2 * S5_STATE), F32)] * 6,
        compiler_params=_params("parallel"),
        name="s5_scan",
    )(u4.reshape(R // 8, 8, W, CHUNK), toep, rs, ri, ac, dg)
    return y3.reshape(nchunk, nseq, W, CHUNK)


def _s5_out_kernel(ys_ref, sg_ref, x_ref, wgluT_ref, bglu_ref, wo_ref, g_ref, b_ref, o_ref):
    rep = ys_ref.shape[0] // 2
    half = rep * CHUNK
    bglu = jnp.tile(bglu_ref[...], (1, rep))
    y = [jax.nn.gelu(jnp.concatenate([ys_ref[i * rep + c, 0] for c in range(rep)], axis=1).astype(F32))
         for i in range(2)]
    z = [_dot(wgluT_ref[...], y[i].astype(BF16)) + bglu for i in range(2)]
    v = [(y[i] * _sigmoid(z[i]) * sg_ref[:, i * half:(i + 1) * half].astype(F32)).astype(BF16) for i in range(2)]
    o = [_dot_tn(v[i], wo_ref[...]) for i in range(2)]
    for i in range(2):
        rows = slice(i * half, (i + 1) * half)
        o_ref[0, rows] = _layer_norm(ALPHA * x_ref[0, rows] + o[i], g_ref[...], b_ref[...])


def _s5_out(ys4, sgT, x, w, ln_g, ln_b, tm):
    B, L, _ = x.shape
    nl = L // tm
    tok = pl.BlockSpec((D_MODEL, tm), lambda b, i: (0, b * nl + i))
    sq = pl.BlockSpec((D_MODEL, D_MODEL), lambda b, i: (0, 0))
    col = pl.BlockSpec((D_MODEL, LANES), lambda b, i: (0, 0))
    row = pl.BlockSpec((1, D_MODEL), lambda b, i: (0, 0))
    return pl.pallas_call(
        _s5_out_kernel,
        grid=(B, nl),
        in_specs=[_chunk_spec(tm), tok, pl.BlockSpec((1, tm, D_MODEL), lambda b, i: (b, i, 0)),
                  sq, col, sq, row, row],
        out_specs=pl.BlockSpec((1, tm, D_MODEL), lambda b, i: (b, i, 0)),
        out_shape=jax.ShapeDtypeStruct((B, L, D_MODEL), F32),
        compiler_params=_params("parallel", "parallel"),
        name="s5_out",
    )(ys4, sgT, x, w['wgluT'], w['bglu'], w['wo'], ln_g, ln_b)


def _prep_mla(w_in, g_q, w_q_up, g_kv, w_kv_up, w_out):
    nkv = Q_LORA + KV_LORA
    wa = jnp.concatenate([w_in[:, :nkv + QK_ROPE_DIM],
                          jnp.zeros((D_MODEL, LANES - QK_ROPE_DIM), F32)], axis=1)
    wkv = w_kv_up.reshape(KV_LORA, N_HEADS, QK_NOPE_DIM + V_DIM)
    return dict(
        wa=wa.astype(BF16),
        gq=g_q.reshape(1, Q_LORA), gkv=g_kv.reshape(1, KV_LORA),
        wqT=w_q_up.T.astype(BF16),
        wkn=wkv[:, :, :QK_NOPE_DIM].reshape(KV_LORA, N_HEADS * QK_NOPE_DIM).astype(BF16),
        wvT=wkv[:, :, QK_NOPE_DIM:].reshape(KV_LORA, N_HEADS * V_DIM).T.astype(BF16),
        wgT=w_in[:, nkv + QK_ROPE_DIM:].T.astype(BF16),
        wo=w_out.astype(BF16),
    )


def _rope_tables(L):
    inv = ROPE_THETA ** (-jnp.arange(0, QK_ROPE_DIM, 2, dtype=F32) / QK_ROPE_DIM)
    ang = jnp.arange(L, dtype=F32)[:, None] * inv[None, :]
    cos, sin = jnp.cos(ang), jnp.sin(ang)
    z = jnp.zeros_like(cos)
    z2 = jnp.zeros((L, LANES - QK_ROPE_DIM), F32)
    return dict(
        cosT=cos.T, sinT=sin.T,
        cosk=jnp.concatenate([cos, cos, z2], axis=1),
        sina=jnp.concatenate([z, sin, z2], axis=1),
        sinb=jnp.concatenate([-sin, z, z2], axis=1),
    )


def _powers(lam_bar, n):
    pw = jnp.stack([jnp.ones_like(lam_bar), lam_bar], axis=-2)
    top = lam_bar
    while pw.shape[-2] - 1 < n:
        pw = jnp.concatenate([pw, pw[..., 1:, :] * top[..., None, :]], axis=-2)
        top = top * top
    return pw


def _reim(z, axis):
    return jnp.concatenate([jnp.real(z), jnp.imag(z)], axis=axis)


def _prep_s5(w_in, a_re, a_im, log_step, b_re, b_im, c_re, c_im, d, w_glu, b_glu, w_out):
    G, N, P, T = S5_GROUPS, S5_STATE, S5_GROUP, CHUNK
    lam = lax.complex(a_re, a_im)
    step = jnp.exp(log_step)[..., None]
    lam_bar = jnp.exp(lam * step)
    b_bar = ((lam_bar - 1.0) / lam)[..., None] * lax.complex(b_re, b_im)
    c = lax.complex(c_re, c_im)
    pw = _powers(lam_bar, T)

    cb = c[:, :, None, :, :] * jnp.swapaxes(b_bar, -1, -2)[:, :, :, None, :]
    cb = cb.reshape(2, G, P * P, N)
    cbT = jnp.concatenate([_reim(cb[0], -1), _reim(cb[1], -1)], axis=-1)
    pf = jnp.swapaxes(pw[0, :, :T], -1, -2)
    pb = jnp.swapaxes(pw[1, :, :T], -1, -2)[..., ::-1]
    zf = jnp.zeros((G, N, T), pf.dtype)
    pf = jnp.concatenate([zf, pf], axis=-1)
    pb = jnp.concatenate([zf[..., :1], pb, zf[..., :T - 1]], axis=-1)
    pwT = jnp.concatenate([jnp.real(pf), -jnp.imag(pf), jnp.real(pb), -jnp.imag(pb)], axis=1)

    sf = pw[0, :, T - 1::-1]
    sb = pw[1, :, :T]
    rs_f = jnp.swapaxes(b_bar[0], -1, -2)[:, :, None, :] * sf[:, None, :, :]
    rs_b = jnp.swapaxes(b_bar[1], -1, -2)[:, :, None, :] * sb[:, None, :, :]
    rs = jnp.concatenate([jnp.real(rs_f), jnp.real(rs_b), jnp.imag(rs_f), jnp.imag(rs_b)], axis=-1)
    rs = rs.reshape(G, P * T, 4 * N).astype(BF16)

    of = pw[0, :, 1:T + 1]
    ob = pw[1, :, T:0:-1]
    ri_f = c[0][:, :, None, :] * of[:, None, :, :]
    ri_b = c[1][:, :, None, :] * ob[:, None, :, :]
    ri = jnp.concatenate([jnp.real(ri_f), jnp.real(ri_b), -jnp.imag(ri_f), -jnp.imag(ri_b)], axis=-1)
    ri = jnp.swapaxes(ri.reshape(G, P * T, 4 * N), -1, -2).astype(BF16)

    a = pw[:, :, T]
    rows = [jnp.concatenate([jnp.real(a[0]), jnp.real(a[1])], -1),
            jnp.concatenate([jnp.imag(a[0]), jnp.imag(a[1])], -1)]
    ac = jnp.stack(rows + [jnp.zeros_like(rows[0])] * 6, axis=1)

    return dict(
        wuT=w_in[:, :D_MODEL].T.astype(BF16), wgT=w_in[:, D_MODEL:].T.astype(BF16),
        cbT=cbT, pwT=pwT, rs=rs, ri=ri, ac=ac,
        dg=jnp.repeat(d.reshape(G, 1, P), T, axis=-1),
        wgluT=w_glu.T.astype(BF16),
        bglu=jnp.broadcast_to(b_glu[:, None], (D_MODEL, LANES)),
        wo=w_out.astype(BF16),
    )


def _mla_layer(x, w, tabs, ln_g, ln_b, tm):
    qT, k, vT, sgT, kn2 = _mla_proj(x, w, tabs, tm)
    ogT = _attention(qT, k, vT, sgT, kn2, tm)
    return _mla_out(ogT, x, w['wo'], ln_g, ln_b, tm)


def _s5_layer(x, w, toep, ln_g, ln_b, tm):
    wide = min(WIDE_TILES * tm, x.shape[1])
    u4, sgT = _s5_in(x, w['wuT'], w['wgT'], wide)
    ys4 = _s5_scan(u4, toep, w['rs'], w['ri'], w['ac'], w['dg'])
    return _s5_out(ys4, sgT, x, w, ln_g, ln_b, wide)


def _trunk(x, mla_w, s5_w, toeps, ln_g, ln_b):
    L = x.shape[1]
    tm = min(TOK_TILE, L)
    tabs = _rope_tables(L)
    for i in range(DEPTH):
        g, b = ln_g[i].reshape(1, D_MODEL), ln_b[i].reshape(1, D_MODEL)
        if i % 2 == 0:
            x = _mla_layer(x, mla_w[i // 2], tabs, g, b, tm)
        else:
            x = _s5_layer(x, s5_w[i // 2], toeps[i // 2], g, b, tm)
    return x


def kernel(x_prompt, x_sample, mla_w_in, mla_g_q, mla_w_q_up, mla_g_kv, mla_w_kv_up, mla_w_out,
           s5_w_in, s5_a_re, s5_a_im, s5_log_step, s5_b_re, s5_b_im, s5_c_re, s5_c_im, s5_d,
           s5_w_glu, s5_b_glu, s5_w_out, ln_g, ln_b):
    mla = (mla_w_in, mla_g_q, mla_w_q_up, mla_g_kv, mla_w_kv_up, mla_w_out)
    s5 = (s5_w_in, s5_a_re, s5_a_im, s5_log_step, s5_b_re, s5_b_im, s5_c_re, s5_c_im, s5_d,
          s5_w_glu, s5_b_glu, s5_w_out)
    mla_w = [_prep_mla(*[w[j] for w in mla]) for j in range(mla_w_in.shape[0])]
    s5_w = [_prep_s5(*[w[j] for w in s5]) for j in range(s5_w_in.shape[0])]
    toeps = [_toeplitz(w['cbT'], w['pwT']) for w in s5_w]
    y_prompt = _trunk(x_prompt, mla_w, s5_w, toeps, ln_g, ln_b)
    y_sample = _trunk(x_sample, mla_w, s5_w, toeps, ln_g, ln_b)
    return (y_prompt, y_sample)
```

```python
import functools
import math

import jax
import jax.numpy as jnp
from jax import lax
from jax.experimental import pallas as pl
from jax.experimental.pallas import tpu as pltpu

F32 = jnp.float32
BF16 = jnp.bfloat16

D_MODEL = 1024
DEPTH = 4
N_HEADS = 8
QK_NOPE_DIM = 128
QK_ROPE_DIM = 64
V_DIM = 128
Q_LORA = 384
KV_LORA = 256
ROPE_THETA = 10000.0
ATTN_SCALE = 1.0 / math.sqrt(QK_NOPE_DIM + QK_ROPE_DIM)
S5_GROUP = 16
S5_GROUPS = D_MODEL // S5_GROUP
S5_STATE = 64
ALPHA = (2 * DEPTH) ** 0.25
LN_EPS = 1e-5
RMS_EPS = 1e-6

LANES = 128
HEAD_PAD = 256
ROPE_HALF = QK_ROPE_DIM // 2
LOG2_E = math.log2(math.e)
CHUNK = 128
TOK_TILE = 512
WIDE_TILES = 2
ATTN_Q_TILES = 8
ATTN_REF_KEYS = 128
ATTN_FAST_MARGIN = 60.0
VMEM_LIMIT = 56 * 1024 * 1024

NT_DIMS = (((1,), (1,)), ((), ()))
TN_DIMS = (((0,), (0,)), ((), ()))


def _dot(a, b):
    return jnp.dot(a, b, preferred_element_type=F32)


def _dot_nt(a, b):
    return lax.dot_general(a, b, NT_DIMS, preferred_element_type=F32)


def _dot_tn(a, b):
    return lax.dot_general(a, b, TN_DIMS, preferred_element_type=F32)


def _params(*sem):
    return pltpu.CompilerParams(dimension_semantics=sem, vmem_limit_bytes=VMEM_LIMIT)


def _layer_norm(r, g, b):
    mu = jnp.mean(r, axis=-1, keepdims=True)
    d = r - mu
    var = jnp.mean(d * d, axis=-1, keepdims=True)
    return d * lax.rsqrt(var + LN_EPS) * g + b


def _rms_norm(x, g):
    return x * lax.rsqrt(jnp.mean(x * x, axis=-1, keepdims=True) + RMS_EPS) * g


def _sigmoid(x):
    return 0.5 * jnp.tanh(0.5 * x) + 0.5


def _silu(x):
    h = 0.5 * x
    return h * jnp.tanh(h) + h


def _mla_proj_kernel(x_ref, wa_ref, gq_ref, gkv_ref, wqT_ref, wkn_ref, wvT_ref, wgT_ref,
                     cosT_ref, sinT_ref, cosk_ref, sina_ref, sinb_ref,
                     qT_ref, k_ref, vT_ref, sgT_ref, kn2_ref):
    def put(ref, rows, val):
        w = ref.shape[3]
        for c in range(ref.shape[1]):
            ref[0, c, rows, :] = val[:, c * w:(c + 1) * w]

    xb = x_ref[0].astype(BF16)
    ha = _dot(xb, wa_ref[...])
    cq = _rms_norm(ha[:, :Q_LORA], gq_ref[...]).astype(BF16)
    ckv = _rms_norm(ha[:, Q_LORA:Q_LORA + KV_LORA], gkv_ref[...]).astype(BF16)
    kr = ha[:, Q_LORA + KV_LORA:]

    qT = _dot_nt(wqT_ref[...], cq) * (ATTN_SCALE * LOG2_E)
    cosT = cosT_ref[...]
    sinT = sinT_ref[...]
    tm = qT.shape[1]
    head = QK_NOPE_DIM + QK_ROPE_DIM
    zpad = jnp.zeros((HEAD_PAD - head, tm), F32)
    for h in range(N_HEADS):
        blk = qT[h * head:(h + 1) * head]
        t1 = blk[QK_NOPE_DIM:QK_NOPE_DIM + ROPE_HALF]
        t2 = blk[QK_NOPE_DIM + ROPE_HALF:QK_NOPE_DIM + QK_ROPE_DIM]
        out = jnp.concatenate(
            [blk[:QK_NOPE_DIM], t1 * cosT - t2 * sinT, t2 * cosT + t1 * sinT, zpad], axis=0)
        put(qT_ref, slice(h * HEAD_PAD, (h + 1) * HEAD_PAD), out.astype(BF16))

    kn = _dot(ckv, wkn_ref[...])
    krr = (kr * cosk_ref[...]
           + pltpu.roll(kr, ROPE_HALF, 1) * sina_ref[...]
           + pltpu.roll(kr, LANES - ROPE_HALF, 1) * sinb_ref[...]).astype(BF16)
    krf = krr.astype(F32)
    kr2 = jnp.sum(krf * krf, axis=1, keepdims=True)
    kn2_rows = []
    for h in range(N_HEADS):
        kb = kn[:, h * QK_NOPE_DIM:(h + 1) * QK_NOPE_DIM].astype(BF16)
        k_ref[0, h, :, :QK_NOPE_DIM] = kb
        k_ref[0, h, :, QK_NOPE_DIM:] = krr
        kf = kb.astype(F32)
        n2 = jnp.max(jnp.sum(kf * kf, axis=1, keepdims=True) + kr2, axis=0, keepdims=True)
        kn2_rows.append(jnp.broadcast_to(n2, (1, LANES)))
    kn2 = jnp.concatenate(kn2_rows, axis=0)
    for c in range(kn2_ref.shape[1]):
        kn2_ref[0, c] = kn2

    put(vT_ref, slice(None), _dot_nt(wvT_ref[...], ckv).astype(BF16))
    put(sgT_ref, slice(None), _silu(_dot_nt(wgT_ref[...], xb)).astype(BF16))


def _mla_proj(x, w, tabs, tm):
    B, L, _ = x.shape
    nl = L // tm
    nt = min(WIDE_TILES, nl)
    wide = nt * tm
    const = lambda shape: pl.BlockSpec(shape, lambda b, i: (0,) * len(shape))
    return pl.pallas_call(
        _mla_proj_kernel,
        grid=(B, nl // nt),
        in_specs=[
            pl.BlockSpec((1, wide, D_MODEL), lambda b, i: (b, i, 0)),
            const(w['wa'].shape), const(w['gq'].shape), const(w['gkv'].shape),
            const(w['wqT'].shape), const(w['wkn'].shape), const(w['wvT'].shape), const(w['wgT'].shape),
            pl.BlockSpec((ROPE_HALF, wide), lambda b, i: (0, i)),
            pl.BlockSpec((ROPE_HALF, wide), lambda b, i: (0, i)),
            pl.BlockSpec((wide, LANES), lambda b, i: (i, 0)),
            pl.BlockSpec((wide, LANES), lambda b, i: (i, 0)),
            pl.BlockSpec((wide, LANES), lambda b, i: (i, 0)),
        ],
        out_specs=[
            pl.BlockSpec((1, nt, N_HEADS * HEAD_PAD, tm), lambda b, i: (b, i, 0, 0)),
            pl.BlockSpec((1, N_HEADS, wide, HEAD_PAD), lambda b, i: (b, 0, i, 0)),
            pl.BlockSpec((1, nt, N_HEADS * V_DIM, tm), lambda b, i: (b, i, 0, 0)),
            pl.BlockSpec((1, nt, N_HEADS * V_DIM, tm), lambda b, i: (b, i, 0, 0)),
            pl.BlockSpec((1, nt, N_HEADS, LANES), lambda b, i: (b, i, 0, 0)),
        ],
        out_shape=[
            jax.ShapeDtypeStruct((B, nl, N_HEADS * HEAD_PAD, tm), BF16),
            jax.ShapeDtypeStruct((B, N_HEADS, L, HEAD_PAD), BF16),
            jax.ShapeDtypeStruct((B, nl, N_HEADS * V_DIM, tm), BF16),
            jax.ShapeDtypeStruct((B, nl, N_HEADS * V_DIM, tm), BF16),
            jax.ShapeDtypeStruct((B, nl, N_HEADS, LANES), F32),
        ],
        compiler_params=_params("parallel", "parallel"),
        name="mla_proj",
    )(x, w['wa'], w['gq'], w['gkv'], w['wqT'], w['wkn'], w['wvT'], w['wgT'],
      tabs['cosT'], tabs['sinT'], tabs['cosk'], tabs['sina'], tabs['sinb'])


def _attn_kernel(qT_ref, k_ref, vT_ref, sgT_ref, kn2_ref, o_ref, m_sc, l_sc, acc_sc, *, tk, nk, nq):
    tq = qT_ref.shape[3]
    qTs = [qT_ref[0, t] for t in range(nq)]

    def k_tile(j):
        return k_ref[0, 0, pl.ds(pl.multiple_of(j * tk, tk), tk), :]

    k_pre = k_ref[0, 0, 0:ATTN_REF_KEYS, :]
    q2 = []
    for t in range(nq):
        cols = slice(t * tq, (t + 1) * tq)
        m_sc[:, cols] = jnp.max(_dot(k_pre, qTs[t]), axis=0, keepdims=True)
        qf = qTs[t].astype(F32)
        q2.append(jnp.sum(qf * qf, axis=0, keepdims=True))
    acc_sc[...] = jnp.zeros(acc_sc.shape, F32)
    l_sc[...] = jnp.zeros(l_sc.shape, F32)

    kn2 = jnp.max(kn2_ref[0, :, pl.ds(pl.program_id(1), 1), :])
    bound = jnp.sqrt(jnp.concatenate(q2, axis=1) * kn2)
    fast_ok = jnp.max(bound - m_sc[...]) <= ATTN_FAST_MARGIN

    @pl.when(fast_ok)
    def _fixed_reference():
        ms = [m_sc[:, t * tq:(t + 1) * tq] for t in range(nq)]

        def body(j, carry):
            ks = k_tile(j)
            ps = []
            for t in range(nq):
                cols = slice(t * tq, (t + 1) * tq)
                p = jnp.exp2(_dot(ks, qTs[t]) - ms[t])
                l_sc[:, cols] = l_sc[:, cols] + jnp.sum(p, axis=0, keepdims=True)
                ps.append(p.astype(BF16))
            for t in range(nq):
                cols = slice(t * tq, (t + 1) * tq)
                acc_sc[:, cols] = acc_sc[:, cols] + _dot(vT_ref[0, j], ps[t])
            return carry

        lax.fori_loop(0, nk, body, 0, unroll=8)

    @pl.when(jnp.logical_not(fast_ok))
    def _online():
        qT = jnp.concatenate(qTs, axis=1)

        def body(j, carry):
            sT = _dot(k_tile(j), qT)
            m_prev = m_sc[...]
            m_new = jnp.maximum(m_prev, jnp.max(sT, axis=0, keepdims=True))
            alpha = jnp.exp2(m_prev - m_new)
            p = jnp.exp2(sT - m_new)
            l_sc[...] = l_sc[...] * alpha + jnp.sum(p, axis=0, keepdims=True)
            acc_sc[...] = acc_sc[...] * alpha + _dot(vT_ref[0, j], p.astype(BF16))
            m_sc[...] = m_new
            return carry

        lax.fori_loop(0, nk, body, 0)

    for t in range(nq):
        cols = slice(t * tq, (t + 1) * tq)
        o = acc_sc[:, cols] / l_sc[:, cols] * sgT_ref[0, t].astype(F32)
        o_ref[0, t] = o.astype(BF16)


def _attention(qT, k, vT, sgT, kn2, tm):
    B, nl, _, _ = qT.shape
    L = nl * tm
    nq = min(ATTN_Q_TILES, nl)
    assert nl % nq == 0 and L >= ATTN_REF_KEYS
    kern = functools.partial(_attn_kernel, tk=tm, nk=nl, nq=nq)
    return pl.pallas_call(
        kern,
        grid=(B, N_HEADS, nl // nq),
        in_specs=[
            pl.BlockSpec((1, nq, HEAD_PAD, tm), lambda b, h, i: (b, i, h, 0)),
            pl.BlockSpec((1, 1, L, HEAD_PAD), lambda b, h, i: (b, h, 0, 0)),
            pl.BlockSpec((1, nl, V_DIM, tm), lambda b, h, i: (b, 0, h, 0)),
            pl.BlockSpec((1, nq, V_DIM, tm), lambda b, h, i: (b, i, h, 0)),
            pl.BlockSpec((1, nl, N_HEADS, LANES), lambda b, h, i: (b, 0, 0, 0)),
        ],
        out_specs=pl.BlockSpec((1, nq, V_DIM, tm), lambda b, h, i: (b, i, h, 0)),
        out_shape=jax.ShapeDtypeStruct((B, nl, N_HEADS * V_DIM, tm), BF16),
        scratch_shapes=[pltpu.VMEM((1, nq * tm), F32), pltpu.VMEM((1, nq * tm), F32),
                        pltpu.VMEM((V_DIM, nq * tm), F32)],
        compiler_params=_params("parallel", "parallel", "arbitrary"),
        name="mla_attn",
    )(qT, k, vT, sgT, kn2)


def _out_kernel(zT_ref, x_ref, wo_ref, g_ref, b_ref, o_ref):
    nt, tm = zT_ref.shape[1], zT_ref.shape[3]
    half = tm // 2
    pieces = [(t, i) for t in range(nt) for i in range(2)]
    ys = [_dot_tn(zT_ref[0, t, :, i * half:(i + 1) * half], wo_ref[...]) for t, i in pieces]
    for (t, i), y in zip(pieces, ys):
        rows = slice(t * tm + i * half, t * tm + (i + 1) * half)
        o_ref[0, rows] = _layer_norm(ALPHA * x_ref[0, rows] + y, g_ref[...], b_ref[...])


def _mla_out(ogT, x, woT, ln_g, ln_b, tm):
    B, L, _ = x.shape
    nl = L // tm
    nt = min(WIDE_TILES, nl)
    return pl.pallas_call(
        _out_kernel,
        grid=(B, nl // nt),
        in_specs=[
            pl.BlockSpec((1, nt, D_MODEL, tm), lambda b, i: (b, i, 0, 0)),
            pl.BlockSpec((1, nt * tm, D_MODEL), lambda b, i: (b, i, 0)),
            pl.BlockSpec((D_MODEL, D_MODEL), lambda b, i: (0, 0)),
            pl.BlockSpec((1, D_MODEL), lambda b, i: (0, 0)),
            pl.BlockSpec((1, D_MODEL), lambda b, i: (0, 0)),
        ],
        out_specs=pl.BlockSpec((1, nt * tm, D_MODEL), lambda b, i: (b, i, 0)),
        out_shape=jax.ShapeDtypeStruct((B, L, D_MODEL), F32),
        compiler_params=_params("parallel", "parallel"),
        name="mla_out",
    )(ogT, x, woT, ln_g, ln_b)


def _s5_in_kernel(x_ref, wuT_ref, wgT_ref, u_ref, sgT_ref):
    xb = x_ref[0].astype(BF16)
    uT = _dot_nt(wuT_ref[...], xb)
    for c in range(u_ref.shape[0]):
        u_ref[c, 0] = uT[:, c * CHUNK:(c + 1) * CHUNK].astype(BF16)
    sgT_ref[...] = _silu(_dot_nt(wgT_ref[...], xb)).astype(BF16)


def _chunk_spec(tm):
    return pl.BlockSpec((tm // CHUNK, 1, D_MODEL, CHUNK), lambda b, i: (i, b, 0, 0))


def _s5_in(x, wuT, wgT, tm):
    B, L, _ = x.shape
    nl = L // tm
    tok = pl.BlockSpec((D_MODEL, tm), lambda b, i: (0, b * nl + i))
    wspec = pl.BlockSpec((D_MODEL, D_MODEL), lambda b, i: (0, 0))
    return pl.pallas_call(
        _s5_in_kernel,
        grid=(B, nl),
        in_specs=[pl.BlockSpec((1, tm, D_MODEL), lambda b, i: (b, i, 0)), wspec, wspec],
        out_specs=[_chunk_spec(tm), tok],
        out_shape=[jax.ShapeDtypeStruct((L // CHUNK, B, D_MODEL, CHUNK), BF16),
                   jax.ShapeDtypeStruct((D_MODEL, B * L), BF16)],
        compiler_params=_params("parallel", "parallel"),
        name="s5_in",
    )(x, wuT, wgT)


def _toep_kernel(cbT_ref, pwT_ref, o_ref, kv_sc):
    kv_sc[...] = jnp.dot(cbT_ref[0], pwT_ref[0], preferred_element_type=F32,
                         precision=lax.Precision.HIGHEST)

    def body(pi, carry):
        row0 = pl.multiple_of(pi * CHUNK, CHUNK)
        for po in range(S5_GROUP):
            kv = kv_sc[pl.ds(pi * S5_GROUP + po, 1), :]
            rolled = pltpu.roll(jnp.broadcast_to(kv, (CHUNK, 2 * CHUNK)), 0, 1, stride=1, stride_axis=0)
            o_ref[0, pl.ds(row0, CHUNK), po * CHUNK:(po + 1) * CHUNK] = rolled[:, CHUNK:].astype(BF16)
        return carry

    lax.fori_loop(0, S5_GROUP, body, 0)


def _toeplitz(cbT, pwT):
    G = cbT.shape[0]
    n = S5_GROUP * CHUNK
    blk = pl.BlockSpec((1, 2 * CHUNK, 2 * CHUNK), lambda g: (g, 0, 0))
    return pl.pallas_call(
        _toep_kernel,
        grid=(G,),
        in_specs=[blk, blk],
        out_specs=pl.BlockSpec((1, n, n), lambda g: (g, 0, 0)),
        out_shape=jax.ShapeDtypeStruct((G, n, n), BF16),
        scratch_shapes=[pltpu.VMEM((2 * CHUNK, 2 * CHUNK), F32)],
        compiler_params=_params("parallel"),
        name="s5_toeplitz",
    )(cbT, pwT)


def _sublane_transpose8(xs):
    xs = list(xs)
    sub = lax.broadcasted_iota(jnp.int32, xs[0].shape, 1)
    for k in (4, 2, 1):
        keep = (sub & k) == 0
        for i in range(8):
            if i & k:
                continue
            lo, hi = xs[i], xs[i + k]
            xs[i] = jnp.where(keep, lo, pltpu.roll(hi, k, 1))
            xs[i + k] = jnp.where(keep, pltpu.roll(lo, 8 - k, 1), hi)
    return xs


def _scan_kernel(u_ref, tm_ref, rs_ref, ri_ref, ac_ref, dg_ref, y_ref,
                 sre_sc, sim_sc, hfre_sc, hfim_sc, hbre_sc, hbim_sc, *, nseq, nchunk):
    R = 8 * u_ref.shape[0]
    rows8 = [u_ref[:, i].astype(F32) for i in range(8)]
    cols = []
    for half in range(S5_GROUP // 8):
        xs = [x[:, 8 * half:8 * half + 8, :] for x in rows8]
        cols += [x.reshape(R, CHUNK) for x in _sublane_transpose8(xs)]
    u = jnp.concatenate(cols, axis=1)
    lhs = u.astype(BF16)
    s = _dot(lhs, rs_ref[0])
    y = _dot(lhs, tm_ref[0])
    tile = 2 * S5_STATE
    sre_sc[...] = s[:, :tile]
    sim_sc[...] = s[:, tile:]
    ac = ac_ref[0]
    are, aim = ac[0:1], ac[1:2]
    is_fwd = lax.broadcasted_iota(jnp.int32, (nseq, tile), 1) < S5_STATE

    ere = jnp.zeros((nseq, tile), F32)
    eim = jnp.zeros((nseq, tile), F32)
    for c in range(nchunk):
        rf = pl.ds(c * nseq, nseq)
        rb = pl.ds((nchunk - 1 - c) * nseq, nseq)
        hfre_sc[rf, :] = ere
        hfim_sc[rf, :] = eim
        hbre_sc[rb, :] = ere
        hbim_sc[rb, :] = eim
        s_re = jnp.where(is_fwd, sre_sc[rf, :], sre_sc[rb, :])
        s_im = jnp.where(is_fwd, sim_sc[rf, :], sim_sc[rb, :])
        ere, eim = are * ere - aim * eim + s_re, are * eim + aim * ere + s_im

    fwd_rows = lax.broadcasted_iota(jnp.int32, hfre_sc.shape, 1) < S5_STATE
    h = jnp.concatenate([jnp.where(fwd_rows, hfre_sc[...], hbre_sc[...]),
                         jnp.where(fwd_rows, hfim_sc[...], hbim_sc[...])], axis=1).astype(BF16)
    y = y + _dot(h, ri_ref[0]) + dg_ref[0] * u
    halves = []
    for half in range(S5_GROUP // 8):
        ys = [y[:, (8 * half + p) * CHUNK:(8 * half + p + 1) * CHUNK].reshape(R // 8, 8, CHUNK) for p in range(8)]
        halves.append(_sublane_transpose8(ys))
    for i in range(8):
        y_ref[:, i] = jnp.concatenate([h[i] for h in halves], axis=1).astype(BF16)


def _s5_scan(u4, toep, rs, ri, ac, dg):
    nchunk, nseq, W, _ = u4.shape
    R = nchunk * nseq
    n = S5_GROUP * CHUNK
    kern = functools.partial(_scan_kernel, nseq=nseq, nchunk=nchunk)
    y3 = pl.pallas_call(
        kern,
        grid=(S5_GROUPS,),
        in_specs=[
            pl.BlockSpec((R // 8, 8, S5_GROUP, CHUNK), lambda g: (0, 0, g, 0)),
            pl.BlockSpec((1, n, n), lambda g: (g, 0, 0)),
            pl.BlockSpec((1, n, 4 * S5_STATE), lambda g: (g, 0, 0)),
            pl.BlockSpec((1, 4 * S5_STATE, n), lambda g: (g, 0, 0)),
            pl.BlockSpec((1, 8, 2 * S5_STATE), lambda g: (g, 0, 0)),
            pl.BlockSpec((1, 1, n), lambda g: (g, 0, 0)),
        ],
        out_specs=pl.BlockSpec((R // 8, 8, S5_GROUP, CHUNK), lambda g: (0, 0, g, 0)),
        out_shape=jax.ShapeDtypeStruct((R // 8, 8, W, CHUNK), BF16),
        scratch_shapes=[pltpu.VMEM((R, 2 * S5_STATE), F32)] * 6,
        compiler_params=_params("parallel"),
        name="s5_scan",
    )(u4.reshape(R // 8, 8, W, CHUNK), toep, rs, ri, ac, dg)
    return y3.reshape(nchunk, nseq, W, CHUNK)


def _s5_out_kernel(ys_ref, sg_ref, x_ref, wgluT_ref, bglu_ref, wo_ref, g_ref, b_ref, o_ref):
    rep = 2
    n = ys_ref.shape[0] // rep
    half = rep * CHUNK
    bglu = jnp.tile(bglu_ref[...], (1, rep))
    y = [jax.nn.gelu(jnp.concatenate([ys_ref[i * rep + c, 0] for c in range(rep)], axis=1).astype(F32))
         for i in range(n)]
    z = [_dot(wgluT_ref[...], y[i].astype(BF16)) + bglu for i in range(n)]
    v = [(y[i] * _sigmoid(z[i]) * sg_ref[:, i * half:(i + 1) * half].astype(F32)).astype(BF16) for i in range(n)]
    o = [_dot_tn(v[i], wo_ref[...]) for i in range(n)]
    for i in range(n):
        rows = slice(i * half, (i + 1) * half)
        o_ref[0, rows] = _layer_norm(ALPHA * x_ref[0, rows] + o[i], g_ref[...], b_ref[...])


def _s5_out(ys4, sgT, x, w, ln_g, ln_b, tm):
    B, L, _ = x.shape
    nl = L // tm
    tok = pl.BlockSpec((D_MODEL, tm), lambda b, i: (0, b * nl + i))
    sq = pl.BlockSpec((D_MODEL, D_MODEL), lambda b, i: (0, 0))
    col = pl.BlockSpec((D_MODEL, LANES), lambda b, i: (0, 0))
    row = pl.BlockSpec((1, D_MODEL), lambda b, i: (0, 0))
    return pl.pallas_call(
        _s5_out_kernel,
        grid=(B, nl),
        in_specs=[_chunk_spec(tm), tok, pl.BlockSpec((1, tm, D_MODEL), lambda b, i: (b, i, 0)),
                  sq, col, sq, row, row],
        out_specs=pl.BlockSpec((1, tm, D_MODEL), lambda b, i: (b, i, 0)),
        out_shape=jax.ShapeDtypeStruct((B, L, D_MODEL), F32),
        compiler_params=_params("parallel", "parallel"),
        name="s5_out",
    )(ys4, sgT, x, w['wgluT'], w['bglu'], w['wo'], ln_g, ln_b)


def _prep_mla(w_in, g_q, w_q_up, g_kv, w_kv_up, w_out):
    nkv = Q_LORA + KV_LORA
    wa = jnp.concatenate([w_in[:, :nkv + QK_ROPE_DIM],
                          jnp.zeros((D_MODEL, LANES - QK_ROPE_DIM), F32)], axis=1)
    wkv = w_kv_up.reshape(KV_LORA, N_HEADS, QK_NOPE_DIM + V_DIM)
    return dict(
        wa=wa.astype(BF16),
        gq=g_q.reshape(1, Q_LORA), gkv=g_kv.reshape(1, KV_LORA),
        wqT=w_q_up.T.astype(BF16),
        wkn=wkv[:, :, :QK_NOPE_DIM].reshape(KV_LORA, N_HEADS * QK_NOPE_DIM).astype(BF16),
        wvT=wkv[:, :, QK_NOPE_DIM:].reshape(KV_LORA, N_HEADS * V_DIM).T.astype(BF16),
        wgT=w_in[:, nkv + QK_ROPE_DIM:].T.astype(BF16),
        wo=w_out.astype(BF16),
    )


def _rope_tables(L):
    inv = ROPE_THETA ** (-jnp.arange(0, QK_ROPE_DIM, 2, dtype=F32) / QK_ROPE_DIM)
    ang = jnp.arange(L, dtype=F32)[:, None] * inv[None, :]
    cos, sin = jnp.cos(ang), jnp.sin(ang)
    z = jnp.zeros_like(cos)
    z2 = jnp.zeros((L, LANES - QK_ROPE_DIM), F32)
    return dict(
        cosT=cos.T, sinT=sin.T,
        cosk=jnp.concatenate([cos, cos, z2], axis=1),
        sina=jnp.concatenate([z, sin, z2], axis=1),
        sinb=jnp.concatenate([-sin, z, z2], axis=1),
    )


def _powers(lam_bar, n):
    pw = jnp.stack([jnp.ones_like(lam_bar), lam_bar], axis=-2)
    top = lam_bar
    while pw.shape[-2] - 1 < n:
        pw = jnp.concatenate([pw, pw[..., 1:, :] * top[..., None, :]], axis=-2)
        top = top * top
    return pw


def _reim(z, axis):
    return jnp.concatenate([jnp.real(z), jnp.imag(z)], axis=axis)


def _prep_s5(w_in, a_re, a_im, log_step, b_re, b_im, c_re, c_im, d, w_glu, b_glu, w_out):
    G, N, P, T = S5_GROUPS, S5_STATE, S5_GROUP, CHUNK
    lam = lax.complex(a_re, a_im)
    step = jnp.exp(log_step)[..., None]
    lam_bar = jnp.exp(lam * step)
    b_bar = ((lam_bar - 1.0) / lam)[..., None] * lax.complex(b_re, b_im)
    c = lax.complex(c_re, c_im)
    pw = _powers(lam_bar, T)

    cb = c[:, :, None, :, :] * jnp.swapaxes(b_bar, -1, -2)[:, :, :, None, :]
    cb = cb.reshape(2, G, P * P, N)
    cbT = jnp.concatenate([_reim(cb[0], -1), _reim(cb[1], -1)], axis=-1)
    pf = jnp.swapaxes(pw[0, :, :T], -1, -2)
    pb = jnp.swapaxes(pw[1, :, :T], -1, -2)[..., ::-1]
    zf = jnp.zeros((G, N, T), pf.dtype)
    pf = jnp.concatenate([zf, pf], axis=-1)
    pb = jnp.concatenate([zf[..., :1], pb, zf[..., :T - 1]], axis=-1)
    pwT = jnp.concatenate([jnp.real(pf), -jnp.imag(pf), jnp.real(pb), -jnp.imag(pb)], axis=1)

    sf = pw[0, :, T - 1::-1]
    sb = pw[1, :, :T]
    rs_f = jnp.swapaxes(b_bar[0], -1, -2)[:, :, None, :] * sf[:, None, :, :]
    rs_b = jnp.swapaxes(b_bar[1], -1, -2)[:, :, None, :] * sb[:, None, :, :]
    rs = jnp.concatenate([jnp.real(rs_f), jnp.real(rs_b), jnp.imag(rs_f), jnp.imag(rs_b)], axis=-1)
    rs = rs.reshape(G, P * T, 4 * N).astype(BF16)

    of = pw[0, :, 1:T + 1]
    ob = pw[1, :, T:0:-1]
    ri_f = c[0][:, :, None, :] * of[:, None, :, :]
    ri_b = c[1][:, :, None, :] * ob[:, None, :, :]
    ri = jnp.concatenate([jnp.real(ri_f), jnp.real(ri_b), -jnp.imag(ri_f), -jnp.imag(ri_b)], axis=-1)
    ri = jnp.swapaxes(ri.reshape(G, P * T, 4 * N), -1, -2).astype(BF16)

    a = pw[:, :, T]
    rows = [jnp.concatenate([jnp.real(a[0]), jnp.real(a[1])], -1),
            jnp.concatenate([jnp.imag(a[0]), jnp.imag(a[1])], -1)]
    ac = jnp.stack(rows + [jnp.zeros_like(rows[0])] * 6, axis=1)

    return dict(
        wuT=w_in[:, :D_MODEL].T.astype(BF16), wgT=w_in[:, D_MODEL:].T.astype(BF16),
        cbT=cbT, pwT=pwT, rs=rs, ri=ri, ac=ac,
        dg=jnp.repeat(d.reshape(G, 1, P), T, axis=-1),
        wgluT=w_glu.T.astype(BF16),
        bglu=jnp.broadcast_to(b_glu[:, None], (D_MODEL, LANES)),
        wo=w_out.astype(BF16),
    )


def _mla_layer(x, w, tabs, ln_g, ln_b, tm):
    qT, k, vT, sgT, kn2 = _mla_proj(x, w, tabs, tm)
    ogT = _attention(qT, k, vT, sgT, kn2, tm)
    return _mla_out(ogT, x, w['wo'], ln_g, ln_b, tm)


def _s5_layer(x, w, toep, ln_g, ln_b, tm):
    wide = min(WIDE_TILES * tm, x.shape[1])
    u4, sgT = _s5_in(x, w['wuT'], w['wgT'], wide)
    ys4 = _s5_scan(u4, toep, w['rs'], w['ri'], w['ac'], w['dg'])
    return _s5_out(ys4, sgT, x, w, ln_g, ln_b, wide)


def _trunk(x, mla_w, s5_w, toeps, ln_g, ln_b):
    L = x.shape[1]
    tm = min(TOK_TILE, L)
    tabs = _rope_tables(L)
    for i in range(DEPTH):
        g, b = ln_g[i].reshape(1, D_MODEL), ln_b[i].reshape(1, D_MODEL)
        if i % 2 == 0:
            x = _mla_layer(x, mla_w[i // 2], tabs, g, b, tm)
        else:
            x = _s5_layer(x, s5_w[i // 2], toeps[i // 2], g, b, tm)
    return x


def kernel(x_prompt, x_sample, mla_w_in, mla_g_q, mla_w_q_up, mla_g_kv, mla_w_kv_up, mla_w_out,
           s5_w_in, s5_a_re, s5_a_im, s5_log_step, s5_b_re, s5_b_im, s5_c_re, s5_c_im, s5_d,
           s5_w_glu, s5_b_glu, s5_w_out, ln_g, ln_b):
    mla = (mla_w_in, mla_g_q, mla_w_q_up, mla_g_kv, mla_w_kv_up, mla_w_out)
    s5 = (s5_w_in, s5_a_re, s5_a_im, s5_log_step, s5_b_re, s5_b_im, s5_c_re, s5_c_im, s5_d,
          s5_w_glu, s5_b_glu, s5_w_out)
    mla_w = [_prep_mla(*[w[j] for w in mla]) for j in range(mla_w_in.shape[0])]
    s5_w = [_prep_s5(*[w[j] for w in s5]) for j in range(s5_w_in.shape[0])]
    toeps = [_toeplitz(w['cbT'], w['pwT']) for w in s5_w]
    y_prompt = _trunk(x_prompt, mla_w, s5_w, toeps, ln_g, ln_b)
    y_sample = _trunk(x_sample, mla_w, s5_w, toeps, ln_g, ln_b)
    return (y_prompt, y_sample)
```
